```python
import math
import jax
import jax.numpy as jnp
from jax import lax
import numpy as np

D_MODEL = 1024
BATCH = 16
SEQ = 2048
DEPTH = 2

GRID_W = 64
CTX_LEN = 256
HEAD_DIM = 64
D_MIX = D_MODEL
RWKV_WIDTH = D_MIX // 4
RWKV_HEADS = RWKV_WIDTH // HEAD_DIM
RWKV_LORA_W = 64
RWKV_LORA_A = 64
RWKV_LORA_G = 128
RWKV_IN = 3 * RWKV_WIDTH + RWKV_LORA_W + RWKV_LORA_A + RWKV_LORA_G
RWKV_SPLITS = (RWKV_WIDTH, 2 * RWKV_WIDTH, 3 * RWKV_WIDTH, 3 * RWKV_WIDTH + RWKV_LORA_W, 3 * RWKV_WIDTH + RWKV_LORA_W + RWKV_LORA_A)
CONV_W = 3
GN_EPS = 64e-5
S5_WIDTH = D_MIX // 4
S5_CH = 16
S5_GROUPS = S5_WIDTH // S5_CH
S5_STATE = 64
ATT_WIDTH = D_MIX - RWKV_WIDTH - S5_WIDTH
ATT_HEADS = ATT_WIDTH // HEAD_DIM
ATT_KV_HEADS = 2
ATT_GQ = ATT_HEADS // ATT_KV_HEADS
WINDOW = 128
ATT_BLOCK = 128
ROPE_BASE = 10000.0
IN_SPLITS = (RWKV_IN, RWKV_IN + S5_WIDTH, RWKV_IN + S5_WIDTH + ATT_WIDTH, RWKV_IN + S5_WIDTH + ATT_WIDTH + ATT_KV_HEADS * HEAD_DIM)
D_IN = IN_SPLITS[-1] + ATT_KV_HEADS * HEAD_DIM
N_GROUPS = 4
EXPERTS_PER_GROUP = 8
N_EXPERTS = N_GROUPS * EXPERTS_PER_GROUP
TOP_K_EXPERT = 2
D_EXPERT = 512
MOE_BLOCK = 128
N_MOD = 6
DEEPNORM_ALPHA = (2.0 * DEPTH) ** 0.25
DEEPNORM_BETA = (8.0 * DEPTH) ** -0.25
LN_EPS = 1e-5
F32 = jnp.float32

kernel_name = 'hybrid_rwkv7_s5_swa_hmoe_dit_block'


def layer_norm(x, g, b):
    xf = x.astype(F32)
    mu = jnp.mean(xf, axis=-1, keepdims=True)
    var = jnp.mean(jnp.square(xf - mu), axis=-1, keepdims=True)
    return ((xf - mu) * lax.rsqrt(var + LN_EPS) * g + b).astype(x.dtype)


def centred_conv(u, w):
    half = CONV_W // 2
    L = u.shape[1]
    up = jnp.pad(u, ((0, 0), (half, half), (0, 0)))
    return sum(up[:, j:j + L] * w[j] for j in range(CONV_W))


def rwkv_features(p, conv_w, w0, w2, a0, a2, k_k, k_a):
    p = centred_conv(p, conv_w)
    r, k, v, w_lo, a_lo, g_lo = jnp.split(p, RWKV_SPLITS, axis=-1)
    B, L = r.shape[:2]
    heads = lambda t: t.reshape(B, L, RWKV_HEADS, HEAD_DIM)
    kk = heads(k * k_k).astype(F32)
    kk = kk * lax.rsqrt(jnp.sum(kk * kk, axis=-1, keepdims=True) + 1e-12)
    dirs = []
    for d in range(2):
        w_log = -jax.nn.softplus(-(w0[d] + jnp.tanh(w_lo) @ w2[d])) - 0.5
        decay = jnp.exp(-jnp.exp(w_log))
        a = jax.nn.sigmoid(a0[d] + a_lo @ a2[d])
        k_d = k * (1.0 + (a - 1.0) * k_a)
        dirs.append((heads(decay), heads(k_d), heads(a)))
    return heads(r), heads(v), kk, g_lo, dirs


def rwkv_scan(r, decay, k, v, kk, kk_a, s0, reverse, emit):
    xs = tuple(jnp.moveaxis(t.astype(F32), 1, 0) for t in (r, decay, k, v, kk, kk_a))

    def step(s, inp):
        r_t, w_t, k_t, v_t, kk_t, b_t = inp
        s_kk = jnp.einsum('bhvk,bhk->bhv', s, kk_t)
        s = s * w_t[:, :, None, :] - s_kk[..., None] * b_t[:, :, None, :] + v_t[..., None] * k_t[:, :, None, :]
        return s, (jnp.einsum('bhvk,bhk->bhv', s, r_t) if emit else None)

    s_final, ys = lax.scan(step, s0, xs, reverse=reverse)
    return s_final, (jnp.moveaxis(ys, 0, 1) if emit else None)


def rwkv_output(y, feats, g2, r_k, gn_w, gn_b):
    r, v, kk, g_lo, dirs = feats
    B, L = y.shape[:2]
    mu = jnp.mean(y, axis=-1, keepdims=True)
    var = jnp.mean(jnp.square(y - mu), axis=-1, keepdims=True)
    yn = ((y - mu) * lax.rsqrt(var + GN_EPS)).reshape(B, L, RWKV_WIDTH) * gn_w + gn_b
    k_sum = dirs[0][1] + dirs[1][1]
    bonus = jnp.sum(r * k_sum * r_k, axis=-1, keepdims=True) * v
    gate = jax.nn.sigmoid(g_lo) @ g2
    return (yn.astype(v.dtype) + bonus.reshape(B, L, RWKV_WIDTH)) * gate


def rwkv_mixer(p_ctx, p_lat, emit_ctx, conv_w, w0, w2, a0, a2, g2, k_k, k_a, r_k, gn_w, gn_b):
    fc = rwkv_features(p_ctx, conv_w, w0, w2, a0, a2, k_k, k_a)
    fl = rwkv_features(p_lat, conv_w, w0, w2, a0, a2, k_k, k_a)
    B = p_lat.shape[0]
    s0 = jnp.zeros((B, RWKV_HEADS, HEAD_DIM, HEAD_DIM), F32)
    y_lat, y_ctx = [], []
    for d, rev in enumerate((False, True)):
        rc, vc, kkc, _, dc = fc
        rl, vl, kkl, _, dl = fl
        s_ctx, yc = rwkv_scan(rc, dc[d][0], dc[d][1], vc, kkc, kkc * dc[d][2], s0, rev, emit_ctx)
        _, yl = rwkv_scan(rl, dl[d][0], dl[d][1], vl, kkl, kkl * dl[d][2], s_ctx, rev, True)
        y_lat.append(yl)
        y_ctx.append(yc)
    out_lat = rwkv_output(y_lat[0] + y_lat[1], fl, g2, r_k, gn_w, gn_b)
    out_ctx = rwkv_output(y_ctx[0] + y_ctx[1], fc, g2, r_k, gn_w, gn_b) if emit_ctx else None
    return out_lat, out_ctx


def s5_discretise(lam_re, lam_im, log_dt):
    dt = jnp.exp(log_dt)[:, None]
    mag = jnp.exp(lam_re * dt)
    ab_re = mag * jnp.cos(lam_im * dt)
    ab_im = mag * jnp.sin(lam_im * dt)
    nr, ni = ab_re - 1.0, ab_im
    den = lam_re * lam_re + lam_im * lam_im
    coef_re = (nr * lam_re + ni * lam_im) / den
    coef_im = (ni * lam_re - nr * lam_im) / den
    return ab_re, ab_im, coef_re, coef_im


def s5_scan(u, ab_re, ab_im, coef_re, coef_im, b_re, b_im, h0_re, h0_im):
    b_re, b_im = b_re.astype(F32), b_im.astype(F32)
    bb_re = coef_re[..., None] * b_re - coef_im[..., None] * b_im
    bb_im = coef_re[..., None] * b_im + coef_im[..., None] * b_re
    bu_re = jnp.einsum('gpc,blgc->blgp', bb_re, u)
    bu_im = jnp.einsum('gpc,blgc->blgp', bb_im, u)
    bu_re = bu_re.at[:, 0].add(ab_re * h0_re - ab_im * h0_im)
    bu_im = bu_im.at[:, 0].add(ab_re * h0_im + ab_im * h0_re)
    L = u.shape[1]
    a_re = jnp.broadcast_to(ab_re, (1, L) + ab_re.shape)
    a_im = jnp.broadcast_to(ab_im, (1, L) + ab_im.shape)

    def combine(e1, e2):
        a1r, a1i, b1r, b1i = e1
        a2r, a2i, b2r, b2i = e2
        return (a2r * a1r - a2i * a1i, a2r * a1i + a2i * a1r,
                a2r * b1r - a2i * b1i + b2r, a2r * b1i + a2i * b1r + b2i)

    _, _, h_re, h_im = lax.associative_scan(combine, (a_re, a_im, bu_re, bu_im), axis=1)
    return h_re, h_im


def s5_readout(h_re, h_im, c_re, c_im):
    y = jnp.einsum('gcp,blgp->blgc', c_re.astype(F32), h_re) - jnp.einsum('gcp,blgp->blgc', c_im.astype(F32), h_im)
    return y.reshape(y.shape[0], y.shape[1], S5_WIDTH)


def s5_glu(y, glu_w, glu_b):
    z = jax.nn.gelu(y)
    return z * jax.nn.sigmoid(z @ glu_w + glu_b)


def s5_mixer(u_ctx, u_lat, emit_ctx, lam_re, lam_im, log_dt, b_re, b_im, c_re, c_im, d_skip, glu_w, glu_b):
    B, C = u_ctx.shape[:2]
    L = u_lat.shape[1]
    uc = u_ctx.astype(F32).reshape(B, C, S5_GROUPS, S5_CH)
    ul = u_lat.astype(F32).reshape(B, L, S5_GROUPS, S5_CH)
    h0 = jnp.zeros((B, S5_GROUPS, S5_STATE), F32)
    y_lat = d_skip * u_lat.astype(F32)
    y_ctx = d_skip * u_ctx.astype(F32) if emit_ctx else None
    for d in range(2):
        order = (lambda t: jnp.flip(t, axis=1)) if d == 1 else (lambda t: t)
        disc = s5_discretise(lam_re[d].astype(F32), lam_im[d].astype(F32), log_dt[d].astype(F32))
        hc_re, hc_im = s5_scan(order(uc), *disc, b_re, b_im, h0, h0)
        hl_re, hl_im = s5_scan(order(ul), *disc, b_re, b_im, hc_re[:, -1], hc_im[:, -1])
        y_lat = y_lat + order(s5_readout(hl_re, hl_im, c_re, c_im))
        if emit_ctx:
            y_ctx = y_ctx + order(s5_readout(hc_re, hc_im, c_re, c_im))
    out_lat = s5_glu(y_lat, glu_w, glu_b).astype(u_lat.dtype)
    out_ctx = s5_glu(y_ctx, glu_w, glu_b).astype(u_ctx.dtype) if emit_ctx else None
    return out_lat, out_ctx


def axial_rope_tables(L):
    rows = L // GRID_W
    row_id, col_id = jnp.meshgrid(jnp.arange(rows, dtype=F32), jnp.arange(GRID_W, dtype=F32), indexing='ij')
    n_freq = HEAD_DIM // 4
    inv_freq = ROPE_BASE ** (-jnp.arange(n_freq, dtype=F32) / n_freq)
    ang = jnp.concatenate([row_id.reshape(-1, 1) * inv_freq, col_id.reshape(-1, 1) * inv_freq], axis=-1)
    return jnp.cos(ang), jnp.sin(ang)


def apply_rope(t, cos, sin):
    extra = t.ndim - 3
    cos = cos.reshape(cos.shape[:1] + (1,) * extra + cos.shape[1:])
    sin = sin.reshape(sin.shape[:1] + (1,) * extra + sin.shape[1:])
    t1, t2 = jnp.split(t, 2, axis=-1)
    return jnp.concatenate([t1 * cos - t2 * sin, t2 * cos + t1 * sin], axis=-1).astype(t.dtype)


def softmax_with_sink(scores, sink):
    s_sink = jnp.broadcast_to(sink[None, :, :, None, None], scores.shape[:-1] + (1,))
    p = jax.nn.softmax(jnp.concatenate([scores, s_sink], axis=-1), axis=-1)
    return p[..., :-1]


def attention_mixer(q_ctx, k_ctx, v_ctx, q_lat, k_lat, v_lat, sink, emit_ctx):
    B, L, _ = q_lat.shape
    C = k_ctx.shape[1]
    nb = L // ATT_BLOCK
    scale = HEAD_DIM ** -0.5
    cos, sin = axial_rope_tables(L)
    q = apply_rope(q_lat.reshape(B, L, ATT_KV_HEADS, ATT_GQ, HEAD_DIM), cos, sin)
    k = apply_rope(k_lat.reshape(B, L, ATT_KV_HEADS, HEAD_DIM), cos, sin)
    v = v_lat.reshape(B, L, ATT_KV_HEADS, HEAD_DIM)
    kc = k_ctx.reshape(B, C, ATT_KV_HEADS, HEAD_DIM)
    vc = v_ctx.reshape(B, C, ATT_KV_HEADS, HEAD_DIM)
    sink = sink.reshape(ATT_KV_HEADS, ATT_GQ).astype(F32)

    def neighbours(t):
        tp = jnp.pad(t, ((0, 0), (ATT_BLOCK, ATT_BLOCK), (0, 0), (0, 0)))
        tp = tp.reshape(B, nb + 2, ATT_BLOCK, ATT_KV_HEADS, HEAD_DIM)
        return jnp.moveaxis(jnp.concatenate([tp[:, :-2], tp[:, 1:-1], tp[:, 2:]], axis=2), 1, 0)

    qb = jnp.moveaxis(q.reshape(B, nb, ATT_BLOCK, ATT_KV_HEADS, ATT_GQ, HEAD_DIM), 1, 0)
    kb, vb = neighbours(k), neighbours(v)
    blk = jnp.arange(nb)[:, None, None] * ATT_BLOCK
    q_pos = blk + jnp.arange(ATT_BLOCK)[None, :, None]
    k_pos = blk - ATT_BLOCK + jnp.arange(3 * ATT_BLOCK)[None, None, :]
    band_mask = (jnp.abs(q_pos - k_pos) <= WINDOW) & (k_pos >= 0) & (k_pos < L)
    n_band = 3 * ATT_BLOCK

    def attend_block(inp):
        q_blk, k_blk, v_blk, m = inp
        s_lat = jnp.einsum('bqhgd,bkhd->bhgqk', q_blk, k_blk, preferred_element_type=F32) * scale
        s_lat = jnp.where(m, s_lat, -jnp.inf)
        s_ctx = jnp.einsum('bqhgd,bchd->bhgqc', q_blk, kc, preferred_element_type=F32) * scale
        p = softmax_with_sink(jnp.concatenate([s_lat, s_ctx], axis=-1), sink)
        o = jnp.einsum('bhgqk,bkhd->bqhgd', p[..., :n_band].astype(v_blk.dtype), v_blk)
        return o + jnp.einsum('bhgqc,bchd->bqhgd', p[..., n_band:].astype(vc.dtype), vc)

    ob = lax.map(attend_block, (qb, kb, vb, band_mask))
    out_lat = jnp.moveaxis(ob, 0, 1).reshape(B, L, ATT_WIDTH).astype(q_lat.dtype)
    out_ctx = None
    if emit_ctx:
        qc = q_ctx.reshape(B, C, ATT_KV_HEADS, ATT_GQ, HEAD_DIM)
        s_cc = jnp.einsum('bqhgd,bchd->bhgqc', qc, kc, preferred_element_type=F32) * scale
        p_cc = softmax_with_sink(s_cc, sink)
        out_ctx = jnp.einsum('bhgqc,bchd->bqhgd', p_cc.astype(vc.dtype), vc).reshape(B, C, ATT_WIDTH).astype(q_ctx.dtype)
    return out_lat, out_ctx


def routed_experts(h, experts, weights, w_gate, w_up, w_down):
    T, D = h.shape
    A = T * TOP_K_EXPERT
    P = -(-A // MOE_BLOCK) * MOE_BLOCK + N_EXPERTS * MOE_BLOCK
    n_blk = P // MOE_BLOCK
    flat_e = experts.reshape(-1)
    flat_tok = jnp.repeat(jnp.arange(T, dtype=jnp.int32), TOP_K_EXPERT)
    flat_w = weights.reshape(-1)
    order = jnp.argsort(flat_e)
    se = flat_e[order]
    counts = jnp.zeros((N_EXPERTS,), jnp.int32).at[flat_e].add(1)
    start = jnp.cumsum(counts) - counts
    padded = (counts + MOE_BLOCK - 1) // MOE_BLOCK * MOE_BLOCK
    pend = jnp.cumsum(padded)
    pstart = pend - padded
    dest = pstart[se] + (jnp.arange(A, dtype=jnp.int32) - start[se])
    tok_buf = jnp.zeros((P,), jnp.int32).at[dest].set(flat_tok[order])
    w_buf = jnp.zeros((P,), F32).at[dest].set(flat_w[order])
    blk_exp = jnp.minimum(jnp.searchsorted(pend, jnp.arange(n_blk) * MOE_BLOCK, side='right'), N_EXPERTS - 1)
    xb = h[tok_buf].reshape(n_blk, MOE_BLOCK, D)

    def run_block(inp):
        x_blk, e = inp
        return (jax.nn.silu(x_blk @ w_gate[e]) * (x_blk @ w_up[e])) @ w_down[e]

    yb = lax.map(run_block, (xb, blk_exp)).reshape(P, D)
    return jnp.zeros_like(h).at[tok_buf].add(yb * w_buf[:, None].astype(h.dtype))


def hier_moe(h, rg_w, rg_b, re_w, re_b, w_gate, w_up, w_down):
    T = h.shape[0]
    g_logits = (h @ rg_w + rg_b).astype(F32)
    g_idx = jnp.argmax(g_logits, axis=-1)
    g_prob = jnp.take_along_axis(jax.nn.softmax(g_logits, axis=-1), g_idx[:, None], axis=-1)
    e_logits = (h @ re_w + re_b).astype(F32).reshape(T, N_GROUPS, EXPERTS_PER_GROUP)
    e_logits = jnp.take_along_axis(e_logits, g_idx[:, None, None], axis=1)[:, 0]
    top_logits, top_idx = lax.top_k(e_logits, TOP_K_EXPERT)
    weights = g_prob * jax.nn.softmax(top_logits, axis=-1)
    experts = g_idx[:, None] * EXPERTS_PER_GROUP + top_idx
    return routed_experts(h, experts, weights, w_gate, w_up, w_down)


def trunk_layer(x, xc, c, c_ctx, last, w_mod, b_mod, w_in, rwkv_conv, rwkv_w0, rwkv_w2, rwkv_a0, rwkv_a2, rwkv_g2,
                rwkv_k_k, rwkv_k_a, rwkv_r_k, rwkv_gn_w, rwkv_gn_b, s5_lam_re, s5_lam_im, s5_log_dt, s5_b_re, s5_b_im,
                s5_c_re, s5_c_im, s5_d, s5_glu_w, s5_glu_b, attn_sink, w_out, ln1_g, ln1_b, ln2_g, ln2_b,
                router_group_w, router_group_b, router_expert_w, router_expert_b, expert_w_gate, expert_w_up, expert_w_down):
    emit_ctx = not last
    B, L, D = x.shape
    mod = jax.nn.silu(c) @ w_mod + b_mod
    mod_c = jax.nn.silu(c_ctx) @ w_mod + b_mod
    sh1, sc1, gt1, sh2, sc2, gt2 = jnp.split(mod[:, None, :], N_MOD, axis=-1)
    csh1, csc1, cgt1, csh2, csc2, cgt2 = jnp.split(mod_c, N_MOD, axis=-1)

    proj = (x * (1.0 + sc1) + sh1) @ w_in
    proj_c = (xc * (1.0 + csc1) + csh1) @ w_in
    pr, ps, pq, pk, pv = jnp.split(proj, IN_SPLITS, axis=-1)
    cr, cs, cq, ck, cv = jnp.split(proj_c, IN_SPLITS, axis=-1)
    yr, yr_c = rwkv_mixer(cr, pr, emit_ctx, rwkv_conv, rwkv_w0, rwkv_w2, rwkv_a0, rwkv_a2, rwkv_g2,
                          rwkv_k_k, rwkv_k_a, rwkv_r_k, rwkv_gn_w, rwkv_gn_b)
    ys, ys_c = s5_mixer(cs, ps, emit_ctx, s5_lam_re, s5_lam_im, s5_log_dt, s5_b_re, s5_b_im, s5_c_re, s5_c_im,
                        s5_d, s5_glu_w, s5_glu_b)
    ya, ya_c = attention_mixer(cq, ck, cv, pq, pk, pv, attn_sink, emit_ctx)
    mix = jnp.concatenate([yr, ys, ya], axis=-1) @ w_out
    x = layer_norm(DEEPNORM_ALPHA * x + gt1 * mix, ln1_g, ln1_b)
    if emit_ctx:
        mix_c = jnp.concatenate([yr_c, ys_c, ya_c], axis=-1) @ w_out
        xc = layer_norm(DEEPNORM_ALPHA * xc + cgt1 * mix_c, ln1_g, ln1_b)

    moe_params = (router_group_w, router_group_b, router_expert_w, router_expert_b, expert_w_gate, expert_w_up, expert_w_down)
    h = x * (1.0 + sc2) + sh2
    if emit_ctx:
        hc = xc * (1.0 + csc2) + csh2
        y = hier_moe(jnp.concatenate([h.reshape(-1, D), hc.reshape(-1, D)], axis=0), *moe_params)
        y_lat = y[:B * L].reshape(B, L, D)
        y_ctx = y[B * L:].reshape(xc.shape)
        xc = layer_norm(DEEPNORM_ALPHA * xc + cgt2 * y_ctx, ln2_g, ln2_b)
    else:
        y_lat = hier_moe(h.reshape(-1, D), *moe_params).reshape(B, L, D)
        xc = None
    x = layer_norm(DEEPNORM_ALPHA * x + gt2 * y_lat, ln2_g, ln2_b)
    return x, xc


def setup_inputs(seed: int = 0) -> dict:
    key = jax.random.key(seed)
    ks = iter(jax.random.split(key, 48))
    nrm = lambda shape, s: s * jax.random.normal(next(ks), shape, F32)
    Dp, D = DEPTH, D_MODEL
    ratio = jnp.arange(RWKV_WIDTH, dtype=F32) / (RWKV_WIDTH - 1)
    n_idx = jnp.arange(S5_STATE, dtype=F32)
    side = jnp.array([0.25, 0.5, 0.25], F32)[:, None]
    return {
        'x': nrm((BATCH, SEQ, D), 1.0),
        'c': nrm((BATCH, D), 1.0),
        'ctx': nrm((BATCH, CTX_LEN, D), 1.0),
        'c_ctx': nrm((D,), 1.0),
        'w_mod': nrm((Dp, D, N_MOD * D), 0.5 * D ** -0.5),
        'b_mod': nrm((Dp, N_MOD * D), 0.01),
        'w_in': nrm((Dp, D, D_IN), D ** -0.5),
        'rwkv_conv': side + nrm((Dp, CONV_W, RWKV_IN), 0.05),
        'rwkv_w0': -6.0 + 5.0 * ratio ** 0.85 + nrm((Dp, 2, RWKV_WIDTH), 0.1),
        'rwkv_w2': nrm((Dp, 2, RWKV_LORA_W, RWKV_WIDTH), 0.1),
        'rwkv_a0': nrm((Dp, 2, RWKV_WIDTH), 0.1),
        'rwkv_a2': nrm((Dp, 2, RWKV_LORA_A, RWKV_WIDTH), 0.1),
        'rwkv_g2': nrm((Dp, RWKV_LORA_G, RWKV_WIDTH), RWKV_LORA_G ** -0.5),
        'rwkv_k_k': 0.85 + nrm((Dp, RWKV_WIDTH), 0.02),
        'rwkv_k_a': 1.0 + nrm((Dp, RWKV_WIDTH), 0.02),
        'rwkv_r_k': nrm((Dp, RWKV_HEADS, HEAD_DIM), 0.1),
        'rwkv_gn_w': 1.0 + nrm((Dp, RWKV_WIDTH), 0.02),
        'rwkv_gn_b': nrm((Dp, RWKV_WIDTH), 0.01),
        's5_lam_re': -0.5 + nrm((Dp, 2, S5_GROUPS, S5_STATE), 0.01),
        's5_lam_im': math.pi * n_idx + nrm((Dp, 2, S5_GROUPS, S5_STATE), 0.01),
        's5_log_dt': jax.random.uniform(next(ks), (Dp, 2, S5_GROUPS), F32, math.log(1e-3), math.log(1e-1)),
        's5_b_re': nrm((Dp, S5_GROUPS, S5_STATE, S5_CH), (2 * S5_CH) ** -0.5),
        's5_b_im': nrm((Dp, S5_GROUPS, S5_STATE, S5_CH), (2 * S5_CH) ** -0.5),
        's5_c_re': nrm((Dp, S5_GROUPS, S5_CH, S5_STATE), 0.5),
        's5_c_im': nrm((Dp, S5_GROUPS, S5_CH, S5_STATE), 0.5),
        's5_d': nrm((Dp, S5_WIDTH), 1.0),
        's5_glu_w': nrm((Dp, S5_WIDTH, S5_WIDTH), S5_WIDTH ** -0.5),
        's5_glu_b': nrm((Dp, S5_WIDTH), 0.01),
        'attn_sink': nrm((Dp, ATT_HEADS), 0.5),
        'w_out': nrm((Dp, D_MIX, D), DEEPNORM_BETA * D_MIX ** -0.5),
        'ln1_g': 1.0 + nrm((Dp, D), 0.05),
        'ln1_b': nrm((Dp, D), 0.01),
        'ln2_g': 1.0 + nrm((Dp, D), 0.05),
        'ln2_b': nrm((Dp, D), 0.01),
        'router_group_w': nrm((Dp, D, N_GROUPS), D ** -0.5),
        'router_group_b': nrm((Dp, N_GROUPS), 0.01),
        'router_expert_w': nrm((Dp, D, N_EXPERTS), D ** -0.5),
        'router_expert_b': nrm((Dp, N_EXPERTS), 0.01),
        'expert_w_gate': nrm((Dp, N_EXPERTS, D, D_EXPERT), D ** -0.5),
        'expert_w_up': nrm((Dp, N_EXPERTS, D, D_EXPERT), D ** -0.5),
        'expert_w_down': nrm((Dp, N_EXPERTS, D_EXPERT, D), DEEPNORM_BETA * D_EXPERT ** -0.5),
    }


def reference(x, c, ctx, c_ctx, w_mod, b_mod, w_in, rwkv_conv, rwkv_w0, rwkv_w2, rwkv_a0, rwkv_a2, rwkv_g2,
              rwkv_k_k, rwkv_k_a, rwkv_r_k, rwkv_gn_w, rwkv_gn_b, s5_lam_re, s5_lam_im, s5_log_dt, s5_b_re, s5_b_im,
              s5_c_re, s5_c_im, s5_d, s5_glu_w, s5_glu_b, attn_sink, w_out, ln1_g, ln1_b, ln2_g, ln2_b,
              router_group_w, router_group_b, router_expert_w, router_expert_b, expert_w_gate, expert_w_up, expert_w_down):
    stacked = (w_mod, b_mod, w_in, rwkv_conv, rwkv_w0, rwkv_w2, rwkv_a0, rwkv_a2, rwkv_g2, rwkv_k_k, rwkv_k_a,
               rwkv_r_k, rwkv_gn_w, rwkv_gn_b, s5_lam_re, s5_lam_im, s5_log_dt, s5_b_re, s5_b_im, s5_c_re, s5_c_im,
               s5_d, s5_glu_w, s5_glu_b, attn_sink, w_out, ln1_g, ln1_b, ln2_g, ln2_b, router_group_w,
               router_group_b, router_expert_w, router_expert_b, expert_w_gate, expert_w_up, expert_w_down)
    xc = ctx
    for i in range(DEPTH):
        layer_params = [p[i] for p in stacked]
        x, xc = trunk_layer(x, xc, c, c_ctx, i == DEPTH - 1, *layer_params)
    return x
```

```python
import functools
import math

import jax
import jax.numpy as jnp
from jax import lax
from jax.experimental import pallas as pl
from jax.experimental.pallas import tpu as pltpu

F32 = jnp.float32
BF16 = jnp.bfloat16

HEAD_DIM = 64
RWKV_WIDTH = 256
RWKV_HEADS = 4
RWKV_LORA = 128
S5_WIDTH = 256
S5_GROUPS = 16
S5_CH = 16
S5_STATE = 64
S5_LANES = S5_GROUPS * S5_STATE
ATT_WIDTH = 512
ATT_HEADS = 8
ATT_KV_HEADS = 2
ATT_GQ = 4
KV_WIDTH = ATT_KV_HEADS * HEAD_DIM
WINDOW = 128
ATT_BLOCK = 128
GRID_W = 64
ROPE_BASE = 10000.0
N_GROUPS = 4
EXPERTS_PER_GROUP = 8
N_EXPERTS = 32
D_EXPERT = 512
MOE_BLOCK = 128
N_MOD = 6
DEPTH = 2
DEEPNORM_ALPHA = (2.0 * DEPTH) ** 0.25
LN_EPS = 1e-5
GN_EPS = 64e-5
LANES = 128
TQ = 32
VMEM_LIMIT = 48 * 1024 * 1024


def _params(*sem):
    return pltpu.CompilerParams(dimension_semantics=sem, vmem_limit_bytes=VMEM_LIMIT)


def _dot(a, b):
    return jnp.dot(a.astype(BF16), b.astype(BF16), preferred_element_type=F32)


def _dot_hi(a, b):
    return jnp.dot(a, b, precision=lax.Precision.HIGHEST, preferred_element_type=F32)


def _sigmoid(x):
    return 1.0 / (1.0 + jnp.exp(-x))


def _layer_norm(x, g, b):
    mu = jnp.mean(x, axis=-1, keepdims=True)
    xc = x - mu
    var = jnp.mean(xc * xc, axis=-1, keepdims=True)
    return xc * lax.rsqrt(var + LN_EPS) * g + b


def _mod_kernel(c_ref, w_ref, b_ref, o_ref):
    c = c_ref[...]
    o_ref[...] = _dot_hi(c * _sigmoid(c), w_ref[...]) + b_ref[...]


def _modulation(c_all, w_mod, b_mod):
    rows, d = c_all.shape
    n = w_mod.shape[1]
    return pl.pallas_call(
        _mod_kernel,
        grid=(n // d,),
        in_specs=[pl.BlockSpec((rows, d), lambda j: (0, 0)),
                  pl.BlockSpec((d, d), lambda j: (0, j)),
                  pl.BlockSpec((1, d), lambda j: (0, j))],
        out_specs=pl.BlockSpec((rows, d), lambda j: (0, j)),
        out_shape=jax.ShapeDtypeStruct((rows, n), F32),
        compiler_params=_params("arbitrary"),
    )(c_all, w_mod, b_mod.reshape(1, n))


def _swap_halves(t):
    n = t.shape[-1]
    lane = lax.broadcasted_iota(jnp.int32, t.shape, t.ndim - 1)
    first = (lane % HEAD_DIM) < (HEAD_DIM // 2)
    return jnp.where(first, pltpu.roll(t, n - HEAD_DIM // 2, t.ndim - 1), pltpu.roll(t, HEAD_DIM // 2, t.ndim - 1))


def _inproj_kernel(x_ref, sc_ref, sh_ref, w_ref, cq_ref, sq_ref, ck_ref, sk_ref,
                   pr_ref, ps_ref, q_ref, k_ref, v_ref):
    tq, b, d = x_ref.shape
    h = x_ref[...] * (1.0 + sc_ref[0]) + sh_ref[0]
    p = _dot(h.reshape(tq * b, d), w_ref[...])
    o_r = RWKV_WIDTH * 3 + RWKV_LORA + 128
    o_s = o_r + S5_WIDTH
    o_q = o_s + ATT_WIDTH
    o_k = o_q + KV_WIDTH
    pr_ref[...] = p[:, :o_r].reshape(tq, b, o_r)
    ps_ref[...] = p[:, o_r:o_s].reshape(tq, b, S5_WIDTH)
    q = p[:, o_s:o_q]
    k = p[:, o_q:o_k]
    bc = lambda r, w: jnp.broadcast_to(r[...], (tq, b, w)).reshape(tq * b, w)
    q = q * bc(cq_ref, ATT_WIDTH) + _swap_halves(q) * bc(sq_ref, ATT_WIDTH)
    k = k * bc(ck_ref, KV_WIDTH) + _swap_halves(k) * bc(sk_ref, KV_WIDTH)
    q_ref[...] = q.reshape(tq, b, ATT_WIDTH)
    k_ref[...] = k.reshape(tq, b, KV_WIDTH)
    v_ref[...] = p[:, o_k:].reshape(tq, b, KV_WIDTH)


def _input_projection(xt, mod_tab, w_in_bf, rope, n_ctx_blocks):
    t, b, d = xt.shape
    d_in = w_in_bf.shape[1]
    seg = lambda i: (i >= n_ctx_blocks).astype(jnp.int32)
    cq, sq, ck, sk = rope
    widths = (RWKV_WIDTH * 3 + RWKV_LORA + 128, S5_WIDTH, ATT_WIDTH, KV_WIDTH, KV_WIDTH)
    tab = lambda w: pl.BlockSpec((TQ, 1, w), lambda i: (i, 0, 0))
    return pl.pallas_call(
        _inproj_kernel,
        grid=(t // TQ,),
        in_specs=[pl.BlockSpec((TQ, b, d), lambda i: (i, 0, 0)),
                  pl.BlockSpec((1, b, d), lambda i: (seg(i), 0, 1)),
                  pl.BlockSpec((1, b, d), lambda i: (seg(i), 0, 0)),
                  pl.BlockSpec((d, d_in), lambda i: (0, 0)),
                  tab(ATT_WIDTH), tab(ATT_WIDTH), tab(KV_WIDTH), tab(KV_WIDTH)],
        out_specs=[pl.BlockSpec((TQ, b, w), lambda i: (i, 0, 0)) for w in widths],
        out_shape=[jax.ShapeDtypeStruct((t, b, w), F32) for w in widths],
        compiler_params=_params("arbitrary"),
    )(xt, mod_tab, mod_tab, w_in_bf, cq, sq, ck, sk)


def _rope_tables(n_ctx, n_lat):
    rows = n_lat // GRID_W
    row_id, col_id = jnp.meshgrid(jnp.arange(rows, dtype=F32), jnp.arange(GRID_W, dtype=F32), indexing='ij')
    n_freq = HEAD_DIM // 4
    inv_freq = ROPE_BASE ** (-jnp.arange(n_freq, dtype=F32) / n_freq)
    ang = jnp.concatenate([row_id.reshape(-1, 1) * inv_freq, col_id.reshape(-1, 1) * inv_freq], axis=-1)
    cos, sin = jnp.cos(ang), jnp.sin(ang)
    cos_h = jnp.concatenate([jnp.ones((n_ctx, HEAD_DIM), F32), jnp.concatenate([cos, cos], axis=-1)], axis=0)
    sin_h = jnp.concatenate([jnp.zeros((n_ctx, HEAD_DIM), F32), jnp.concatenate([-sin, sin], axis=-1)], axis=0)
    scale = HEAD_DIM ** -0.5
    expand = lambda tb, n, s: (jnp.tile(tb, (1, n)) * s)[:, None, :]
    return (expand(cos_h, ATT_HEADS, scale), expand(sin_h, ATT_HEADS, scale),
            expand(cos_h, ATT_KV_HEADS, 1.0), expand(sin_h, ATT_KV_HEADS, 1.0))


def _rwkv_feat_kernel(n_ctx_blocks, cur_ref, prev_ref, next_ref, conv_ref, w0_ref, a0_ref, w2_ref, a2_ref, g2_ref,
                      kk_w_ref, ka_ref, rk_ref, hsum_ref,
                      r_ref, v_ref, kk_ref, d0_ref, d1_ref, k0_ref, k1_ref, b0_ref, b1_ref, bonus_ref, gate_ref):
    i = pl.program_id(0)
    tq, b, w = cur_ref.shape
    rows = tq * b
    cur = cur_ref[...]
    keep_prev = jnp.where((i == 0) | (i == n_ctx_blocks), 0.0, 1.0)
    keep_next = jnp.where((i == n_ctx_blocks - 1) | (i == pl.num_programs(0) - 1), 0.0, 1.0)
    before = jnp.concatenate([prev_ref[...] * keep_prev, cur[:-1]], axis=0)
    after = jnp.concatenate([cur[1:], next_ref[...] * keep_next], axis=0)
    pc = (before * conv_ref[0:1, :] + cur * conv_ref[1:2, :] + after * conv_ref[2:3, :]).reshape(rows, w)
    W = RWKV_WIDTH
    r, k, v = pc[:, :W], pc[:, W:2 * W], pc[:, 2 * W:3 * W]
    lo = pc[:, 3 * W:3 * W + RWKV_LORA]
    g_lo = pc[:, 3 * W + RWKV_LORA:]
    hsum = hsum_ref[...]
    kk = k * kk_w_ref[...]
    kk = kk * lax.rsqrt(_dot_hi(kk * kk, hsum) + 1e-12)
    wl = _dot_hi(jnp.tanh(lo), w2_ref[...])
    al = _dot_hi(lo, a2_ref[...])
    k_sum = None
    for d, (d_ref, kd_ref, b_ref) in enumerate(((d0_ref, k0_ref, b0_ref), (d1_ref, k1_ref, b1_ref))):
        z = -(w0_ref[d:d + 1, :] + wl[:, d * W:(d + 1) * W])
        softplus = jnp.maximum(z, 0.0) + jnp.log(1.0 + jnp.exp(-jnp.abs(z)))
        decay = jnp.exp(-jnp.exp(-softplus - 0.5))
        a = _sigmoid(a0_ref[d:d + 1, :] + al[:, d * W:(d + 1) * W])
        k_d = k * (1.0 + (a - 1.0) * ka_ref[...])
        d_ref[...] = decay.reshape(tq, b, W)
        kd_ref[...] = k_d.reshape(tq, b, W)
        b_ref[...] = (kk * a).reshape(tq, b, W)
        k_sum = k_d if k_sum is None else k_sum + k_d
    r_ref[...] = r.reshape(tq, b, W)
    v_ref[...] = v.reshape(tq, b, W)
    kk_ref[...] = kk.reshape(tq, b, W)
    bonus_ref[...] = (_dot_hi(r * k_sum * rk_ref[...], hsum) * v).reshape(tq, b, W)
    gate_ref[...] = _dot(_sigmoid(g_lo), g2_ref[...]).reshape(tq, b, W)


def _rwkv_features(pr, lp, n_ctx_blocks):
    t, b, w = pr.shape
    nb = t // TQ
    W = RWKV_WIDTH
    full = lambda a: pl.BlockSpec(a.shape, lambda i: (0,) * a.ndim)
    consts = (lp['conv'], lp['w0'], lp['a0'], lp['w2'], lp['a2'], lp['g2'], lp['k_k'], lp['k_a'], lp['r_k'], lp['hsum'])
    return pl.pallas_call(
        functools.partial(_rwkv_feat_kernel, n_ctx_blocks),
        grid=(nb,),
        in_specs=[pl.BlockSpec((TQ, b, w), lambda i: (i, 0, 0)),
                  pl.BlockSpec((1, b, w), lambda i: (jnp.maximum(i * TQ - 1, 0), 0, 0)),
                  pl.BlockSpec((1, b, w), lambda i: (jnp.minimum((i + 1) * TQ, t - 1), 0, 0))]
                 + [full(a) for a in consts],
        out_specs=[pl.BlockSpec((TQ, b, W), lambda i: (i, 0, 0))] * 11,
        out_shape=[jax.ShapeDtypeStruct((t, b, W), F32)] * 11,
        compiler_params=_params("arbitrary"),
    )(pr, pr, pr, *consts)


def _rwkv_scan_kernel(r_ref, w_ref, k_ref, v_ref, kk_ref, b_ref, y_ref, s_ref):
    @pl.when(pl.program_id(0) == 0)
    def _():
        s_ref[...] = jnp.zeros_like(s_ref)

    n = HEAD_DIM

    def step(t, carry):
        r = r_ref[t]
        w = w_ref[t]
        k = k_ref[t]
        v = v_ref[t]
        kk = kk_ref[t]
        b = b_ref[t]
        wr = w * r
        s_kk = jnp.zeros_like(r)
        z = jnp.zeros_like(r)
        for j in range(n):
            s = s_ref[j]
            s_kk = s_kk + s * kk[j:j + 1, :]
            z = z + s * wr[j:j + 1, :]
        for j in range(n):
            s_ref[j] = s_ref[j] * w[j:j + 1, :] - s_kk * b[j:j + 1, :] + v * k[j:j + 1, :]
        br = jnp.sum(b * r, axis=0, keepdims=True)
        kr = jnp.sum(k * r, axis=0, keepdims=True)
        y_ref[t] = z - s_kk * br + v * kr
        return carry

    lax.fori_loop(0, r_ref.shape[0], step, 0)


def _rwkv_scan(streams, tc=8):
    t, _, lanes = streams[0].shape
    spec = pl.BlockSpec((tc, HEAD_DIM, lanes), lambda i: (i, 0, 0))
    return pl.pallas_call(
        _rwkv_scan_kernel,
        grid=(t // tc,),
        in_specs=[spec] * 6,
        out_specs=spec,
        out_shape=jax.ShapeDtypeStruct((t, HEAD_DIM, lanes), F32),
        scratch_shapes=[pltpu.VMEM((HEAD_DIM, HEAD_DIM, lanes), F32)],
        compiler_params=_params("arbitrary"),
    )(*streams)


def _rev_segments(a, n_ctx):
    return jnp.concatenate([a[:n_ctx][::-1], a[n_ctx:][::-1]], axis=0)


def _to_scan_layout(fwd, bwd, n_ctx):
    t, b, _ = fwd.shape
    f = fwd.reshape(t, b, RWKV_HEADS, HEAD_DIM)
    g = _rev_segments(bwd, n_ctx).reshape(t, b, RWKV_HEADS, HEAD_DIM)
    s = jnp.stack([f, g], axis=1)
    return jnp.transpose(s, (0, 4, 1, 2, 3)).reshape(t, HEAD_DIM, 2 * b * RWKV_HEADS)


def _from_scan_layout(y, b, n_ctx):
    t = y.shape[0]
    y = jnp.transpose(y.reshape(t, HEAD_DIM, 2, b, RWKV_HEADS), (2, 0, 3, 4, 1)).reshape(2, t, b, RWKV_WIDTH)
    return y[0] + _rev_segments(y[1], n_ctx)


def _s5_disc_kernel(lre_ref, lim_ref, ldt_ref, bre_ref, bim_ref, are_ref, aim_ref, bbre_ref, bbim_ref):
    lre, lim = lre_ref[...], lim_ref[...]
    dt = jnp.exp(ldt_ref[...])
    mag = jnp.exp(lre * dt)
    ab_re = mag * jnp.cos(lim * dt)
    ab_im = mag * jnp.sin(lim * dt)
    nr, ni = ab_re - 1.0, ab_im
    den = lre * lre + lim * lim
    coef_re = (nr * lre + ni * lim) / den
    coef_im = (ni * lre - nr * lim) / den
    are_ref[...] = ab_re
    aim_ref[...] = ab_im
    for c in range(S5_CH):
        bre, bim = bre_ref[c], bim_ref[c]
        bbre_ref[c] = coef_re * bre - coef_im * bim
        bbim_ref[c] = coef_re * bim + coef_im * bre


def _s5_weights(lam_re, lam_im, log_dt, b_re, b_im, c_re, c_im):
    G, P, CH = S5_GROUPS, S5_STATE, S5_CH
    rows = 2 * G
    bt = lambda a: jnp.broadcast_to(jnp.transpose(a, (2, 0, 1))[:, None], (CH, 2, G, P)).reshape(CH, rows, P)
    shp = jax.ShapeDtypeStruct((rows, P), F32)
    shp_b = jax.ShapeDtypeStruct((CH, rows, P), F32)
    a_re, a_im, bb_re, bb_im = pl.pallas_call(
        _s5_disc_kernel, out_shape=[shp, shp, shp_b, shp_b],
    )(lam_re.reshape(rows, P), lam_im.reshape(rows, P), log_dt.reshape(rows, 1), bt(b_re), bt(b_im))
    eye = jnp.eye(G, dtype=F32)

    def in_map(bb):
        bb = jnp.transpose(bb.reshape(CH, 2, G, P), (1, 2, 0, 3))
        return jnp.einsum('dgcp,gh->dgchp', bb, eye).reshape(2, G * CH, G * P)

    w_in = jnp.concatenate([in_map(bb_re), in_map(bb_im)], axis=-1)
    out_map = lambda c: jnp.einsum('gcp,gh->gphc', c, eye).reshape(G * P, G * CH)
    w_out = jnp.concatenate([out_map(c_re), -out_map(c_im)], axis=0)
    return w_in.astype(BF16), a_re.reshape(2, 1, G * P), a_im.reshape(2, 1, G * P), w_out.astype(BF16)


def _s5_scan_kernel(b, u_ref, win_ref, are_ref, aim_ref, wout_ref, y_ref, h_ref, st_ref):
    d = pl.program_id(0)
    rows = u_ref.shape[0]
    tc = rows // b
    n = S5_LANES

    @pl.when(pl.program_id(1) == 0)
    def _():
        st_ref[...] = jnp.zeros_like(st_ref)

    h_ref[...] = _dot(u_ref[...], win_ref[0])
    chunk = 512
    for c in range(n // chunk):
        re = slice(c * chunk, (c + 1) * chunk)
        im = slice(n + c * chunk, n + (c + 1) * chunk)
        a_re = jnp.broadcast_to(are_ref[0, :, re], (b, chunk))
        a_im = jnp.broadcast_to(aim_ref[0, :, re], (b, chunk))

        def step(t, carry):
            h_re, h_im = carry
            t_eff = jnp.where(d == 0, t, tc - 1 - t)
            row = pl.ds(pl.multiple_of(t_eff * b, b), b)
            n_re = a_re * h_re - a_im * h_im + h_ref[row, re]
            n_im = a_re * h_im + a_im * h_re + h_ref[row, im]
            h_ref[row, re] = n_re
            h_ref[row, im] = n_im
            return n_re, n_im

        h_re, h_im = lax.fori_loop(0, tc, step, (st_ref[:, re], st_ref[:, im]))
        st_ref[:, re] = h_re
        st_ref[:, im] = h_im
    y_ref[0] = _dot(h_ref[...], wout_ref[...])


def _s5_scan(ps, weights, n_ctx_blocks, tc=TQ):
    t, b, w = ps.shape
    w_in, a_re, a_im, w_out = weights
    nb = t // tc
    n_lat_blocks = nb - n_ctx_blocks
    rows = tc * b

    def blk(d, j):
        bwd = jnp.where(j < n_ctx_blocks, n_ctx_blocks - 1 - j, n_ctx_blocks + nb - 1 - j)
        return jnp.where(d == 0, j, bwd)

    del n_lat_blocks
    return pl.pallas_call(
        functools.partial(_s5_scan_kernel, b),
        grid=(2, nb),
        in_specs=[pl.BlockSpec((rows, w), lambda d, j: (blk(d, j), 0)),
                  pl.BlockSpec((1, w, 2 * S5_LANES), lambda d, j: (d, 0, 0)),
                  pl.BlockSpec((1, 1, S5_LANES), lambda d, j: (d, 0, 0)),
                  pl.BlockSpec((1, 1, S5_LANES), lambda d, j: (d, 0, 0)),
                  pl.BlockSpec((2 * S5_LANES, w), lambda d, j: (0, 0))],
        out_specs=pl.BlockSpec((1, rows, w), lambda d, j: (d, blk(d, j), 0)),
        out_shape=jax.ShapeDtypeStruct((2, t * b, w), F32),
        scratch_shapes=[pltpu.VMEM((rows, 2 * S5_LANES), F32), pltpu.VMEM((b, 2 * S5_LANES), F32)],
        compiler_params=_params("arbitrary", "arbitrary"),
    )(ps.reshape(t * b, w), w_in, a_re, a_im, w_out)


def _attn_kernel(band, sink_ref, q_ref, *refs):
    if band:
        kp_ref, k0_ref, kn_ref, vp_ref, v0_ref, vn_ref, kc_ref, vc_ref, o_ref = refs
    else:
        kc_ref, vc_ref, o_ref = refs
    i = pl.program_id(1)
    nq = q_ref.shape[0]
    q = q_ref[...].astype(BF16)
    kc = kc_ref[...].astype(BF16)
    vc = vc_ref[...].astype(BF16)
    if band:
        kb = jnp.concatenate([kp_ref[...], k0_ref[...], kn_ref[...]], axis=0).astype(BF16)
        vb = jnp.concatenate([vp_ref[...], v0_ref[...], vn_ref[...]], axis=0).astype(BF16)
        nk = kb.shape[0]
        q_pos = i * nq + lax.broadcasted_iota(jnp.int32, (nq, nk), 0)
        k_pos = (i - 1) * nq + lax.broadcasted_iota(jnp.int32, (nq, nk), 1)
        n_lat = pl.num_programs(1) * nq
        mask = (jnp.abs(q_pos - k_pos) <= WINDOW) & (k_pos >= 0) & (k_pos < n_lat)
    qk = lambda a, bb: lax.dot_general(a, bb, (((1,), (1,)), ((), ())), preferred_element_type=F32)
    outs = []
    for h in range(ATT_HEADS):
        kv = h // ATT_GQ
        cols = slice(kv * HEAD_DIM, (kv + 1) * HEAD_DIM)
        qh = q[:, h * HEAD_DIM:(h + 1) * HEAD_DIM]
        sink = sink_ref[h]
        s_c = qk(qh, kc[:, cols])
        m = jnp.maximum(jnp.max(s_c, axis=1, keepdims=True), sink)
        if band:
            s_b = jnp.where(mask, qk(qh, kb[:, cols]), -jnp.inf)
            m = jnp.maximum(m, jnp.max(s_b, axis=1, keepdims=True))
        e_c = jnp.exp(s_c - m)
        den = jnp.sum(e_c, axis=1, keepdims=True) + jnp.exp(sink - m)
        o = jnp.dot(e_c.astype(BF16), vc[:, cols], preferred_element_type=F32)
        if band:
            e_b = jnp.exp(s_b - m)
            den = den + jnp.sum(e_b, axis=1, keepdims=True)
            o = o + jnp.dot(e_b.astype(BF16), vb[:, cols], preferred_element_type=F32)
        outs.append(o / den)
    o_ref[...] = jnp.concatenate(outs, axis=1)


def _attention(q, k, v, sink, n_ctx, band):
    t, b, _ = q.shape
    q2 = q.reshape(t, b * ATT_WIDTH)
    k2 = k.reshape(t, b * KV_WIDTH)
    v2 = v.reshape(t, b * KV_WIDTH)
    cb = n_ctx // ATT_BLOCK
    nq = (t - n_ctx) // ATT_BLOCK if band else cb
    off = cb if band else 0
    qspec = pl.BlockSpec((ATT_BLOCK, ATT_WIDTH), lambda bi, i: (off + i, bi))
    kvspec = lambda f: pl.BlockSpec((ATT_BLOCK, KV_WIDTH), lambda bi, i: (cb + f(i), bi))
    prev = lambda i: jnp.maximum(i - 1, 0)
    own = lambda i: i
    nxt = lambda i: jnp.minimum(i + 1, nq - 1)
    cspec = pl.BlockSpec((n_ctx, KV_WIDTH), lambda bi, i: (0, bi))
    in_specs = [pl.BlockSpec(memory_space=pltpu.SMEM), qspec]
    args = [sink, q2]
    if band:
        in_specs += [kvspec(prev), kvspec(own), kvspec(nxt)] * 2
        args += [k2, k2, k2, v2, v2, v2]
    in_specs += [cspec, cspec]
    args += [k2, v2]
    out = pl.pallas_call(
        functools.partial(_attn_kernel, band),
        grid=(b, nq),
        in_specs=in_specs,
        out_specs=pl.BlockSpec((ATT_BLOCK, ATT_WIDTH), lambda bi, i: (i, bi)),
        out_shape=jax.ShapeDtypeStruct((nq * ATT_BLOCK, b * ATT_WIDTH), F32),
        compiler_params=_params("arbitrary", "arbitrary"),
    )(*args)
    return out.reshape(nq * ATT_BLOCK, b, ATT_WIDTH)


def _gelu_tanh(x):
    return 0.5 * x * (1.0 + jnp.tanh(math.sqrt(2.0 / math.pi) * (x + 0.044715 * (x * x * x))))


def _outproj_kernel(x_ref, gt_ref, yr_ref, bonus_ref, gate_ref, u_ref, ys_ref, ya_ref,
                    hsum_ref, gnw_ref, gnb_ref, dskip_ref, gluw_ref, glub_ref, wo_ref, lng_ref, lnb_ref, o_ref):
    tq, b, d = x_ref.shape
    rows = tq * b
    flat = lambda ref: ref[...].reshape(rows, ref.shape[-1])
    hsum = hsum_ref[...]
    inv_n = 1.0 / HEAD_DIM
    y = flat(yr_ref)
    yc = y - _dot_hi(y, hsum) * inv_n
    var = _dot_hi(yc * yc, hsum) * inv_n
    out_r = (yc * lax.rsqrt(var + GN_EPS) * gnw_ref[...] + gnb_ref[...] + flat(bonus_ref)) * flat(gate_ref)
    z = _gelu_tanh(dskip_ref[...] * flat(u_ref) + ys_ref[0] + ys_ref[1])
    out_s = z * _sigmoid(_dot(z, gluw_ref[...]) + glub_ref[...])
    W = RWKV_WIDTH
    mix = _dot(out_r, wo_ref[0:W, :]) + _dot(out_s, wo_ref[W:W + S5_WIDTH, :]) + _dot(flat(ya_ref), wo_ref[W + S5_WIDTH:, :])
    x = x_ref[...]
    res = DEEPNORM_ALPHA * x + gt_ref[0] * mix.reshape(tq, b, d)
    o_ref[...] = _layer_norm(res, lng_ref[...], lnb_ref[...])


def _output_projection(xt, mod_tab, yr, bonus, gate, u, ys, ya, lp, n_ctx_blocks, first_block):
    t, b, d = xt.shape
    nb = t // TQ - first_block
    seg = lambda i: ((i + first_block) >= n_ctx_blocks).astype(jnp.int32)
    blk = lambda w: pl.BlockSpec((TQ, b, w), lambda i: (i + first_block, 0, 0))
    full = lambda a: pl.BlockSpec(a.shape, lambda i: (0,) * a.ndim)
    consts = (lp['hsum'], lp['gn_w'], lp['gn_b'], lp['s5_d'], lp['glu_w'], lp['glu_b'], lp['w_out'], lp['ln1_g'], lp['ln1_b'])
    return pl.pallas_call(
        _outproj_kernel,
        grid=(nb,),
        in_specs=[blk(d), pl.BlockSpec((1, b, d), lambda i: (seg(i), 0, 2)),
                  blk(RWKV_WIDTH), blk(RWKV_WIDTH), blk(RWKV_WIDTH), blk(S5_WIDTH),
                  pl.BlockSpec((2, TQ * b, S5_WIDTH), lambda i: (0, i + first_block, 0)),
                  pl.BlockSpec((TQ, b, ATT_WIDTH), lambda i: (i, 0, 0))] + [full(a) for a in consts],
        out_specs=pl.BlockSpec((TQ, b, d), lambda i: (i, 0, 0)),
        out_shape=jax.ShapeDtypeStruct((nb * TQ, b, d), F32),
        compiler_params=_params("arbitrary"),
    )(xt, mod_tab, yr, bonus, gate, u, ys, ya, *consts)


def _router_kernel(x_ref, sc_ref, sh_ref, w_ref, b_ref, h_ref, idx_ref, wt_ref):
    tq, b, d = x_ref.shape
    rows = tq * b
    h = (x_ref[...] * (1.0 + sc_ref[0]) + sh_ref[0]).reshape(rows, d)
    h_ref[...] = h
    logits = _dot_hi(h, w_ref[...]) + b_ref[...]
    lane = lax.broadcasted_iota(jnp.int32, logits.shape, 1)
    neg = -jnp.inf
    first_max = lambda vals, m: jnp.min(jnp.where(vals == m, lane, LANES), axis=1, keepdims=True)
    gl = jnp.where(lane < N_GROUPS, logits, neg)
    g_max = jnp.max(gl, axis=1, keepdims=True)
    g_idx = first_max(gl, g_max)
    g_prob = 1.0 / jnp.sum(jnp.exp(gl - g_max), axis=1, keepdims=True)
    lo = N_GROUPS + EXPERTS_PER_GROUP * g_idx
    el = jnp.where((lane >= lo) & (lane < lo + EXPERTS_PER_GROUP), logits, neg)
    m1 = jnp.max(el, axis=1, keepdims=True)
    i1 = first_max(el, m1)
    el2 = jnp.where(lane == i1, neg, el)
    m2 = jnp.max(el2, axis=1, keepdims=True)
    i2 = first_max(el2, m2)
    e2 = jnp.exp(m2 - m1)
    p1 = 1.0 / (1.0 + e2)
    p2 = e2 / (1.0 + e2)
    idx_ref[...] = jnp.where(lane == 0, i1 - N_GROUPS, jnp.where(lane == 1, i2 - N_GROUPS, 0))
    wt_ref[...] = jnp.where(lane == 0, g_prob * p1, jnp.where(lane == 1, g_prob * p2, 0.0))


def _router(x1, mod_tab, w_r, b_r, n_ctx_blocks, first_block):
    t, b, d = x1.shape
    nb = t // TQ
    seg = lambda i: ((i + first_block) >= n_ctx_blocks).astype(jnp.int32)
    rows = TQ * b
    n = nb * rows
    return pl.pallas_call(
        _router_kernel,
        grid=(nb,),
        in_specs=[pl.BlockSpec((TQ, b, d), lambda i: (i, 0, 0)),
                  pl.BlockSpec((1, b, d), lambda i: (seg(i), 0, 4)),
                  pl.BlockSpec((1, b, d), lambda i: (seg(i), 0, 3)),
                  pl.BlockSpec(w_r.shape, lambda i: (0, 0)),
                  pl.BlockSpec(b_r.shape, lambda i: (0, 0))],
        out_specs=[pl.BlockSpec((rows, d), lambda i: (i, 0)),
                   pl.BlockSpec((rows, LANES), lambda i: (i, 0)),
                   pl.BlockSpec((rows, LANES), lambda i: (i, 0))],
        out_shape=[jax.ShapeDtypeStruct((n, d), F32),
                   jax.ShapeDtypeStruct((n, LANES), jnp.int32),
                   jax.ShapeDtypeStruct((n, LANES), F32)],
        compiler_params=_params("arbitrary"),
    )(x1, mod_tab, mod_tab, w_r, b_r)


def _dispatch_plan(experts, n_tok):
    a = n_tok * 2
    p = -(-a // MOE_BLOCK) * MOE_BLOCK + N_EXPERTS * MOE_BLOCK
    n_blk = p // MOE_BLOCK
    flat_e = experts.reshape(-1)
    onehot = (flat_e[:, None] == jnp.arange(N_EXPERTS, dtype=jnp.int32)[None, :]).astype(jnp.int32)
    csum = jnp.cumsum(onehot, axis=0)
    rank = jnp.take_along_axis(csum, flat_e[:, None], axis=1)[:, 0] - 1
    counts = csum[-1]
    padded = (counts + MOE_BLOCK - 1) // MOE_BLOCK * MOE_BLOCK
    pend = jnp.cumsum(padded)
    dest = (pend - padded)[flat_e] + rank
    code = jnp.arange(a, dtype=jnp.int32)
    code = (code // 2) | ((code % 2) << 16) | (1 << 17)
    packed = jnp.zeros((p,), jnp.int32).at[dest].set(code)
    n_used = pend[-1] // MOE_BLOCK
    blk = jnp.minimum(jnp.arange(n_blk, dtype=jnp.int32), n_used - 1) * MOE_BLOCK
    blk_exp = jnp.minimum(jnp.searchsorted(pend, blk, side='right'), N_EXPERTS - 1).astype(jnp.int32)
    return blk_exp, n_used.reshape(1).astype(jnp.int32), packed, n_blk


def _expert_kernel(n_tok, blk_exp_ref, n_used_ref, idx_ref, h_hbm, wg_ref, wu_ref, wd_ref, y_hbm,
                   xbuf, ybuf, gsem, ssem):
    i = pl.program_id(0)
    n_used = n_used_ref[0]
    slot = i % 2

    def gather(blk, sl):
        def body(r, c):
            tok = idx_ref[blk * MOE_BLOCK + r] & 0xFFFF
            pltpu.make_async_copy(h_hbm.at[pl.ds(tok, 1), :], xbuf.at[sl, pl.ds(r, 1), :], gsem.at[sl]).start()
            return c
        lax.fori_loop(0, MOE_BLOCK, body, 0)

    def scatter(blk, sl):
        def body(r, c):
            code = idx_ref[blk * MOE_BLOCK + r]
            row = jnp.where((code >> 17) > 0, code & 0xFFFF, n_tok + sl * MOE_BLOCK + r)
            pltpu.make_async_copy(ybuf.at[sl, pl.ds(r, 1), :], y_hbm.at[(code >> 16) & 1, pl.ds(row, 1), :],
                                  ssem.at[sl]).start()
            return c
        lax.fori_loop(0, MOE_BLOCK, body, 0)

    wait_gather = lambda sl: pltpu.make_async_copy(xbuf.at[sl], xbuf.at[sl], gsem.at[sl]).wait()
    wait_scatter = lambda sl: pltpu.make_async_copy(ybuf.at[sl], ybuf.at[sl], ssem.at[sl]).wait()

    @pl.when(i == 0)
    def _():
        gather(0, 0)
        ybuf[...] = jnp.zeros_like(ybuf)
        for j in range(2):
            for sl in range(2):
                spare = pltpu.make_async_copy(
                    ybuf.at[sl], y_hbm.at[j, pl.ds(n_tok + sl * MOE_BLOCK, MOE_BLOCK), :], ssem.at[sl])
                spare.start()
                spare.wait()

    @pl.when(i < n_used)
    def _():
        wait_gather(slot)

        @pl.when(i + 1 < n_used)
        def _():
            gather(i + 1, 1 - slot)

        x = xbuf[slot].astype(BF16)
        g = jnp.dot(x, wg_ref[0], preferred_element_type=F32)
        u = jnp.dot(x, wu_ref[0], preferred_element_type=F32)
        act = (g * _sigmoid(g)) * u
        y = jnp.dot(act.astype(BF16), wd_ref[0], preferred_element_type=F32)

        @pl.when(i >= 2)
        def _():
            wait_scatter(slot)

        ybuf[slot] = y
        scatter(i, slot)

        @pl.when(i == n_used - 1)
        def _():
            wait_scatter(slot)

            @pl.when(i >= 1)
            def _():
                wait_scatter(1 - slot)


def _routed_experts(h, plan, wg, wu, wd):
    n_tok, d = h.shape
    blk_exp, n_used, packed, n_blk = plan
    wspec = lambda s: pl.BlockSpec((1,) + s, lambda i, be, nu, ix: (be[i], 0, 0))
    return pl.pallas_call(
        functools.partial(_expert_kernel, n_tok),
        grid_spec=pltpu.PrefetchScalarGridSpec(
            num_scalar_prefetch=3,
            grid=(n_blk,),
            in_specs=[pl.BlockSpec(memory_space=pl.ANY), wspec((d, D_EXPERT)), wspec((d, D_EXPERT)), wspec((D_EXPERT, d))],
            out_specs=pl.BlockSpec(memory_space=pl.ANY),
            scratch_shapes=[pltpu.VMEM((2, MOE_BLOCK, d), F32), pltpu.VMEM((2, MOE_BLOCK, d), F32),
                            pltpu.SemaphoreType.DMA((2,)), pltpu.SemaphoreType.DMA((2,))]),
        out_shape=jax.ShapeDtypeStruct((2, n_tok + 2 * MOE_BLOCK, d), F32),
        compiler_params=_params("arbitrary"),
    )(blk_exp, n_used, packed, h, wg, wu, wd)


def _combine_kernel(x_ref, gt_ref, y_ref, wt_ref, lng_ref, lnb_ref, o_ref):
    tq, b, d = x_ref.shape
    wt = wt_ref[...]
    y = y_ref[0] * wt[:, 0:1] + y_ref[1] * wt[:, 1:2]
    res = DEEPNORM_ALPHA * x_ref[...] + gt_ref[0] * y.reshape(tq, b, d)
    o_ref[...] = _layer_norm(res, lng_ref[...], lnb_ref[...])


def _moe_combine(x1, mod_tab, y, wts, ln_g, ln_b, n_ctx_blocks, first_block):
    t, b, d = x1.shape
    nb = t // TQ
    rows = TQ * b
    seg = lambda i: ((i + first_block) >= n_ctx_blocks).astype(jnp.int32)
    blk = pl.BlockSpec((TQ, b, d), lambda i: (i, 0, 0))
    return pl.pallas_call(
        _combine_kernel,
        grid=(nb,),
        in_specs=[blk, pl.BlockSpec((1, b, d), lambda i: (seg(i), 0, 5)),
                  pl.BlockSpec((2, rows, d), lambda i: (0, i, 0)),
                  pl.BlockSpec((rows, LANES), lambda i: (i, 0)),
                  pl.BlockSpec(ln_g.shape, lambda i: (0, 0)), pl.BlockSpec(ln_b.shape, lambda i: (0, 0))],
        out_specs=blk,
        out_shape=jax.ShapeDtypeStruct((t, b, d), F32),
        compiler_params=_params("arbitrary"),
    )(x1, mod_tab, y, wts, ln_g, ln_b)


def _layer(xt, c_all, lp, rope, n_ctx, last):
    t, b, d = xt.shape
    ncb = n_ctx // TQ
    first = ncb if last else 0
    mod = _modulation(c_all, lp['w_mod'], lp['b_mod'])
    mod_tab = jnp.stack([jnp.broadcast_to(mod[b:b + 1], (b, mod.shape[1])), mod[:b]], axis=0)

    pr, ps, q, k, v = _input_projection(xt, mod_tab, lp['w_in'], rope, ncb)

    r, vv, kk, d0, d1, k0, k1, b0, b1, bonus, gate = _rwkv_features(pr, lp, ncb)
    sl = lambda f, g: _to_scan_layout(f, g, n_ctx)
    y_scan = _rwkv_scan((sl(r, r), sl(d0, d1), sl(k0, k1), sl(vv, vv), sl(kk, kk), sl(b0, b1)))
    yr = _from_scan_layout(y_scan, b, n_ctx)

    ys = _s5_scan(ps, lp['s5'], ncb)

    ya = _attention(q, k, v, lp['sink'], n_ctx, True)
    if not last:
        ya = jnp.concatenate([_attention(q, k, v, lp['sink'], n_ctx, False), ya], axis=0)

    x1 = _output_projection(xt, mod_tab, yr, bonus, gate, ps, ys, ya, lp, ncb, first)
    h, idx, wts = _router(x1, mod_tab, lp['w_router'], lp['b_router'], ncb, first)
    plan = _dispatch_plan(idx[:, :2], h.shape[0])
    y = _routed_experts(h, plan, lp['w_gate'], lp['w_up'], lp['w_down'])
    return _moe_combine(x1, mod_tab, y, wts, lp['ln2_g'], lp['ln2_b'], ncb, first)


def kernel(x, c, ctx, c_ctx, w_mod, b_mod, w_in, rwkv_conv, rwkv_w0, rwkv_w2, rwkv_a0, rwkv_a2, rwkv_g2, rwkv_k_k, rwkv_k_a, rwkv_r_k, rwkv_gn_w, rwkv_gn_b, s5_lam_re, s5_lam_im, s5_log_dt, s5_b_re, s5_b_im, s5_c_re, s5_c_im, s5_d, s5_glu_w, s5_glu_b, attn_sink, w_out, ln1_g, ln1_b, ln2_g, ln2_b, router_group_w, router_group_b, router_expert_w, router_expert_b, expert_w_gate, expert_w_up, expert_w_down):
    bsz, seq, d = x.shape
    n_ctx = ctx.shape[1]
    depth = w_mod.shape[0]
    W = RWKV_WIDTH
    xt = jnp.transpose(jnp.concatenate([ctx, x], axis=1), (1, 0, 2))
    c_all = jnp.zeros((bsz + 8, d), F32).at[:bsz].set(c).at[bsz].set(c_ctx)
    rope = _rope_tables(n_ctx, seq)
    hsum = jnp.kron(jnp.eye(RWKV_HEADS, dtype=F32), jnp.ones((HEAD_DIM, HEAD_DIM), F32))
    row = lambda a: a.reshape(1, -1)
    zeros_lo = jnp.zeros((RWKV_LORA // 2, 2 * W), F32)
    for i in range(depth):
        n_r = N_GROUPS + N_EXPERTS
        lp = {
            'w_mod': w_mod[i], 'b_mod': b_mod[i], 'w_in': w_in[i].astype(BF16),
            'conv': rwkv_conv[i], 'w0': rwkv_w0[i], 'a0': rwkv_a0[i],
            'w2': jnp.concatenate([jnp.concatenate([rwkv_w2[i, 0], rwkv_w2[i, 1]], axis=1), zeros_lo], axis=0),
            'a2': jnp.concatenate([zeros_lo, jnp.concatenate([rwkv_a2[i, 0], rwkv_a2[i, 1]], axis=1)], axis=0),
            'g2': rwkv_g2[i].astype(BF16), 'k_k': row(rwkv_k_k[i]), 'k_a': row(rwkv_k_a[i]), 'r_k': row(rwkv_r_k[i]),
            'hsum': hsum, 'gn_w': row(rwkv_gn_w[i]), 'gn_b': row(rwkv_gn_b[i]),
            's5': _s5_weights(s5_lam_re[i], s5_lam_im[i], s5_log_dt[i], s5_b_re[i], s5_b_im[i], s5_c_re[i], s5_c_im[i]),
            's5_d': row(s5_d[i]), 'glu_w': s5_glu_w[i].astype(BF16), 'glu_b': row(s5_glu_b[i]),
            'sink': attn_sink[i], 'w_out': w_out[i].astype(BF16),
            'ln1_g': row(ln1_g[i]), 'ln1_b': row(ln1_b[i]), 'ln2_g': row(ln2_g[i]), 'ln2_b': row(ln2_b[i]),
            'w_router': jnp.zeros((d, LANES), F32).at[:, :N_GROUPS].set(router_group_w[i]).at[:, N_GROUPS:n_r].set(router_expert_w[i]),
            'b_router': jnp.zeros((1, LANES), F32).at[0, :N_GROUPS].set(router_group_b[i]).at[0, N_GROUPS:n_r].set(router_expert_b[i]),
            'w_gate': expert_w_gate[i].astype(BF16), 'w_up': expert_w_up[i].astype(BF16), 'w_down': expert_w_down[i].astype(BF16),
        }
        xt = _layer(xt, c_all, lp, rope, n_ctx, i == depth - 1)
    return jnp.transpose(xt, (1, 0, 2))
```

```python
import functools
import math

import jax
import jax.numpy as jnp
from jax import lax
from jax.experimental import pallas as pl
from jax.experimental.pallas import tpu as pltpu

F32 = jnp.float32
BF16 = jnp.bfloat16

HEAD_DIM = 64
RWKV_WIDTH = 256
RWKV_HEADS = 4
RWKV_LORA = 128
S5_WIDTH = 256
S5_GROUPS = 16
S5_CH = 16
S5_STATE = 64
S5_LANES = S5_GROUPS * S5_STATE
ATT_WIDTH = 512
ATT_HEADS = 8
ATT_KV_HEADS = 2
ATT_GQ = 4
KV_WIDTH = ATT_KV_HEADS * HEAD_DIM
WINDOW = 128
ATT_BLOCK = 128
GRID_W = 64
ROPE_BASE = 10000.0
N_GROUPS = 4
EXPERTS_PER_GROUP = 8
N_EXPERTS = 32
D_EXPERT = 512
MOE_BLOCK = 128
N_MOD = 6
DEPTH = 2
DEEPNORM_ALPHA = (2.0 * DEPTH) ** 0.25
LN_EPS = 1e-5
GN_EPS = 64e-5
LANES = 128
TQ = 32
SCAN_UNROLL = 8
VMEM_LIMIT = 48 * 1024 * 1024


def _params(*sem):
    return pltpu.CompilerParams(dimension_semantics=sem, vmem_limit_bytes=VMEM_LIMIT)


def _dot(a, b):
    return jnp.dot(a.astype(BF16), b.astype(BF16), preferred_element_type=F32)


def _dot_hi(a, b):
    return jnp.dot(a, b, precision=lax.Precision.HIGHEST, preferred_element_type=F32)


def _sigmoid(x):
    return 1.0 / (1.0 + jnp.exp(-x))


def _layer_norm(x, g, b):
    mu = jnp.mean(x, axis=-1, keepdims=True)
    xc = x - mu
    var = jnp.mean(xc * xc, axis=-1, keepdims=True)
    return xc * lax.rsqrt(var + LN_EPS) * g + b


def _mod_kernel(c_ref, w_ref, b_ref, o_ref):
    c = c_ref[...]
    o_ref[...] = _dot_hi(c * _sigmoid(c), w_ref[...]) + b_ref[...]


def _modulation(c_all, w_mod, b_mod):
    rows, d = c_all.shape
    n = w_mod.shape[1]
    return pl.pallas_call(
        _mod_kernel,
        grid=(n // d,),
        in_specs=[pl.BlockSpec((rows, d), lambda j: (0, 0)),
                  pl.BlockSpec((d, d), lambda j: (0, j)),
                  pl.BlockSpec((1, d), lambda j: (0, j))],
        out_specs=pl.BlockSpec((rows, d), lambda j: (0, j)),
        out_shape=jax.ShapeDtypeStruct((rows, n), F32),
        compiler_params=_params("arbitrary"),
    )(c_all, w_mod, b_mod.reshape(1, n))


def _swap_halves(t):
    n = t.shape[-1]
    lane = lax.broadcasted_iota(jnp.int32, t.shape, t.ndim - 1)
    first = (lane % HEAD_DIM) < (HEAD_DIM // 2)
    return jnp.where(first, pltpu.roll(t, n - HEAD_DIM // 2, t.ndim - 1), pltpu.roll(t, HEAD_DIM // 2, t.ndim - 1))


def _inproj_kernel(x_ref, sc_ref, sh_ref, w_ref, cq_ref, sq_ref, ck_ref, sk_ref,
                   pr_ref, ps_ref, q_ref, k_ref, v_ref):
    tq, b, d = x_ref.shape
    h = x_ref[...] * (1.0 + sc_ref[0]) + sh_ref[0]
    p = _dot(h.reshape(tq * b, d), w_ref[...])
    o_r = RWKV_WIDTH * 3 + RWKV_LORA + 128
    o_s = o_r + S5_WIDTH
    o_q = o_s + ATT_WIDTH
    o_k = o_q + KV_WIDTH
    pr_ref[...] = p[:, :o_r].reshape(tq, b, o_r)
    ps_ref[...] = p[:, o_r:o_s].reshape(tq, b, S5_WIDTH)
    q = p[:, o_s:o_q]
    k = p[:, o_q:o_k]
    bc = lambda r, w: jnp.broadcast_to(r[...], (tq, b, w)).reshape(tq * b, w)
    q = q * bc(cq_ref, ATT_WIDTH) + _swap_halves(q) * bc(sq_ref, ATT_WIDTH)
    k = k * bc(ck_ref, KV_WIDTH) + _swap_halves(k) * bc(sk_ref, KV_WIDTH)
    q_ref[...] = q.reshape(tq, b, ATT_WIDTH)
    k_ref[...] = k.reshape(tq, b, KV_WIDTH)
    v_ref[...] = p[:, o_k:].reshape(tq, b, KV_WIDTH)


def _input_projection(xt, mod_tab, w_in_bf, rope, n_ctx_blocks):
    t, b, d = xt.shape
    d_in = w_in_bf.shape[1]
    seg = lambda i: (i >= n_ctx_blocks).astype(jnp.int32)
    cq, sq, ck, sk = rope
    widths = (RWKV_WIDTH * 3 + RWKV_LORA + 128, S5_WIDTH, ATT_WIDTH, KV_WIDTH, KV_WIDTH)
    tab = lambda w: pl.BlockSpec((TQ, 1, w), lambda i: (i, 0, 0))
    return pl.pallas_call(
        _inproj_kernel,
        grid=(t // TQ,),
        in_specs=[pl.BlockSpec((TQ, b, d), lambda i: (i, 0, 0)),
                  pl.BlockSpec((1, b, d), lambda i: (seg(i), 0, 1)),
                  pl.BlockSpec((1, b, d), lambda i: (seg(i), 0, 0)),
                  pl.BlockSpec((d, d_in), lambda i: (0, 0)),
                  tab(ATT_WIDTH), tab(ATT_WIDTH), tab(KV_WIDTH), tab(KV_WIDTH)],
        out_specs=[pl.BlockSpec((TQ, b, w), lambda i: (i, 0, 0)) for w in widths],
        out_shape=[jax.ShapeDtypeStruct((t, b, w), F32) for w in widths],
        compiler_params=_params("arbitrary"),
    )(xt, mod_tab, mod_tab, w_in_bf, cq, sq, ck, sk)


def _rope_tables(n_ctx, n_lat):
    rows = n_lat // GRID_W
    row_id, col_id = jnp.meshgrid(jnp.arange(rows, dtype=F32), jnp.arange(GRID_W, dtype=F32), indexing='ij')
    n_freq = HEAD_DIM // 4
    inv_freq = ROPE_BASE ** (-jnp.arange(n_freq, dtype=F32) / n_freq)
    ang = jnp.concatenate([row_id.reshape(-1, 1) * inv_freq, col_id.reshape(-1, 1) * inv_freq], axis=-1)
    cos, sin = jnp.cos(ang), jnp.sin(ang)
    cos_h = jnp.concatenate([jnp.ones((n_ctx, HEAD_DIM), F32), jnp.concatenate([cos, cos], axis=-1)], axis=0)
    sin_h = jnp.concatenate([jnp.zeros((n_ctx, HEAD_DIM), F32), jnp.concatenate([-sin, sin], axis=-1)], axis=0)
    scale = HEAD_DIM ** -0.5
    expand = lambda tb, n, s: (jnp.tile(tb, (1, n)) * s)[:, None, :]
    return (expand(cos_h, ATT_HEADS, scale), expand(sin_h, ATT_HEADS, scale),
            expand(cos_h, ATT_KV_HEADS, 1.0), expand(sin_h, ATT_KV_HEADS, 1.0))


def _rwkv_feat_kernel(n_ctx_blocks, cur_ref, prev_ref, next_ref, conv_ref, w0_ref, a0_ref, w2_ref, a2_ref, g2_ref,
                      kk_w_ref, ka_ref, rk_ref, hsum_ref,
                      r_ref, v_ref, kk_ref, d0_ref, d1_ref, k0_ref, k1_ref, b0_ref, b1_ref, bonus_ref, gate_ref):
    i = pl.program_id(0)
    tq, b, w = cur_ref.shape
    rows = tq * b
    cur = cur_ref[...]
    keep_prev = jnp.where((i == 0) | (i == n_ctx_blocks), 0.0, 1.0)
    keep_next = jnp.where((i == n_ctx_blocks - 1) | (i == pl.num_programs(0) - 1), 0.0, 1.0)
    before = jnp.concatenate([prev_ref[...] * keep_prev, cur[:-1]], axis=0)
    after = jnp.concatenate([cur[1:], next_ref[...] * keep_next], axis=0)
    pc = (before * conv_ref[0:1, :] + cur * conv_ref[1:2, :] + after * conv_ref[2:3, :]).reshape(rows, w)
    W = RWKV_WIDTH
    r, k, v = pc[:, :W], pc[:, W:2 * W], pc[:, 2 * W:3 * W]
    lo = pc[:, 3 * W:3 * W + RWKV_LORA]
    g_lo = pc[:, 3 * W + RWKV_LORA:]
    hsum = hsum_ref[...]
    kk = k * kk_w_ref[...]
    kk = kk * lax.rsqrt(_dot_hi(kk * kk, hsum) + 1e-12)
    wl = _dot_hi(jnp.tanh(lo), w2_ref[...])
    al = _dot_hi(lo, a2_ref[...])
    k_sum = None
    for d, (d_ref, kd_ref, b_ref) in enumerate(((d0_ref, k0_ref, b0_ref), (d1_ref, k1_ref, b1_ref))):
        z = -(w0_ref[d:d + 1, :] + wl[:, d * W:(d + 1) * W])
        softplus = jnp.maximum(z, 0.0) + jnp.log(1.0 + jnp.exp(-jnp.abs(z)))
        decay = jnp.exp(-jnp.exp(-softplus - 0.5))
        a = _sigmoid(a0_ref[d:d + 1, :] + al[:, d * W:(d + 1) * W])
        k_d = k * (1.0 + (a - 1.0) * ka_ref[...])
        d_ref[...] = decay.reshape(tq, b, W)
        kd_ref[...] = k_d.reshape(tq, b, W)
        b_ref[...] = (kk * a).reshape(tq, b, W)
        k_sum = k_d if k_sum is None else k_sum + k_d
    r_ref[...] = r.reshape(tq, b, W)
    v_ref[...] = v.reshape(tq, b, W)
    kk_ref[...] = kk.reshape(tq, b, W)
    bonus_ref[...] = (_dot_hi(r * k_sum * rk_ref[...], hsum) * v).reshape(tq, b, W)
    gate_ref[...] = _dot(_sigmoid(g_lo), g2_ref[...]).reshape(tq, b, W)


def _rwkv_features(pr, lp, n_ctx_blocks):
    t, b, w = pr.shape
    nb = t // TQ
    W = RWKV_WIDTH
    full = lambda a: pl.BlockSpec(a.shape, lambda i: (0,) * a.ndim)
    consts = (lp['conv'], lp['w0'], lp['a0'], lp['w2'], lp['a2'], lp['g2'], lp['k_k'], lp['k_a'], lp['r_k'], lp['hsum'])
    return pl.pallas_call(
        functools.partial(_rwkv_feat_kernel, n_ctx_blocks),
        grid=(nb,),
        in_specs=[pl.BlockSpec((TQ, b, w), lambda i: (i, 0, 0)),
                  pl.BlockSpec((1, b, w), lambda i: (jnp.maximum(i * TQ - 1, 0), 0, 0)),
                  pl.BlockSpec((1, b, w), lambda i: (jnp.minimum((i + 1) * TQ, t - 1), 0, 0))]
                 + [full(a) for a in consts],
        out_specs=[pl.BlockSpec((TQ, b, W), lambda i: (i, 0, 0))] * 11,
        out_shape=[jax.ShapeDtypeStruct((t, b, W), F32)] * 11,
        compiler_params=_params("arbitrary"),
    )(pr, pr, pr, *consts)


def _rwkv_scan_kernel(rf_ref, vf_ref, kkf_ref, wf_ref, kf_ref, bf_ref, rb_ref, vb_ref, kkb_ref, wb_ref, kb_ref, bb_ref,
                      yf_ref, yb_ref, s_ref, x_ref):
    @pl.when(pl.program_id(0) == 0)
    def _():
        s_ref[...] = jnp.zeros_like(s_ref)

    n = HEAD_DIM
    tc, _, lanes = rf_ref.shape
    R, V, KK, W, K, B, WR = range(7)
    pairs = ((rf_ref, rb_ref), (vf_ref, vb_ref), (kkf_ref, kkb_ref), (wf_ref, wb_ref), (kf_ref, kb_ref), (bf_ref, bb_ref))
    is_fwd = lax.broadcasted_iota(jnp.int32, (n, lanes), 1) < lanes // 2

    def step(t, carry):
        tb = tc - 1 - t
        for q, (f_ref, b_ref) in enumerate(pairs):
            x_ref[q] = jnp.where(is_fwd, f_ref[t], b_ref[tb])
        x_ref[WR] = x_ref[W] * x_ref[R]

        def reduce_pass(jg, acc):
            s_kk, z = acc
            for jj in range(SCAN_UNROLL):
                j = jg * SCAN_UNROLL + jj
                s = s_ref[j]
                s_kk = s_kk + s * x_ref[KK, pl.ds(j, 1), :]
                z = z + s * x_ref[WR, pl.ds(j, 1), :]
            return s_kk, z

        zero = jnp.zeros((n, lanes), F32)
        s_kk, z = lax.fori_loop(0, n // SCAN_UNROLL, reduce_pass, (zero, zero))
        v = x_ref[V]

        def update_pass(jg, c):
            for jj in range(SCAN_UNROLL):
                j = jg * SCAN_UNROLL + jj
                s_ref[j] = (s_ref[j] * x_ref[W, pl.ds(j, 1), :] - s_kk * x_ref[B, pl.ds(j, 1), :]
                            + v * x_ref[K, pl.ds(j, 1), :])
            return c

        lax.fori_loop(0, n // SCAN_UNROLL, update_pass, 0)
        r = x_ref[R]
        br = jnp.sum(x_ref[B] * r, axis=0, keepdims=True)
        kr = jnp.sum(x_ref[K] * r, axis=0, keepdims=True)
        y = z - s_kk * br + v * kr
        yf_ref[t] = y
        yb_ref[tb] = y
        return carry

    lax.fori_loop(0, tc, step, 0)


def _rwkv_scan(streams, n_ctx, tc=8):
    t, _, lanes = streams[0].shape
    nb, ncb = t // tc, n_ctx // tc
    mirror = lambda i: jnp.where(i < ncb, ncb - 1 - i, ncb + nb - 1 - i)
    fspec = pl.BlockSpec((tc, HEAD_DIM, lanes), lambda i: (i, 0, 0))
    bspec = pl.BlockSpec((tc, HEAD_DIM, lanes), lambda i: (mirror(i), 0, 0))
    out = jax.ShapeDtypeStruct((t, HEAD_DIM, lanes), F32)
    return pl.pallas_call(
        _rwkv_scan_kernel,
        grid=(nb,),
        in_specs=[fspec] * 6 + [bspec] * 6,
        out_specs=[fspec, bspec],
        out_shape=[out, out],
        scratch_shapes=[pltpu.VMEM((HEAD_DIM, HEAD_DIM, lanes), F32), pltpu.VMEM((7, HEAD_DIM, lanes), F32)],
        compiler_params=_params("arbitrary"),
    )(*streams, *streams)


def _to_scan_layout(fwd, bwd):
    t, b, _ = fwd.shape
    tr = lambda a: jnp.transpose(a.reshape(t, b, RWKV_HEADS, HEAD_DIM), (0, 3, 1, 2)).reshape(t, HEAD_DIM, b * RWKV_HEADS)
    return jnp.concatenate([tr(fwd), tr(bwd)], axis=-1)


def _from_scan_layout(yf, yb, b):
    t, _, lanes = yf.shape
    y = yf[:, :, :lanes // 2] + yb[:, :, lanes // 2:]
    return jnp.transpose(y.reshape(t, HEAD_DIM, b, RWKV_HEADS), (0, 2, 3, 1)).reshape(t, b, RWKV_WIDTH)


def _s5_disc_kernel(lre_ref, lim_ref, ldt_ref, bre_ref, bim_ref, are_ref, aim_ref, bbre_ref, bbim_ref):
    lre, lim = lre_ref[...], lim_ref[...]
    dt = jnp.exp(ldt_ref[...])
    mag = jnp.exp(lre * dt)
    ab_re = mag * jnp.cos(lim * dt)
    ab_im = mag * jnp.sin(lim * dt)
    nr, ni = ab_re - 1.0, ab_im
    den = lre * lre + lim * lim
    coef_re = (nr * lre + ni * lim) / den
    coef_im = (ni * lre - nr * lim) / den
    are_ref[...] = ab_re
    aim_ref[...] = ab_im
    for c in range(S5_CH):
        bre, bim = bre_ref[c], bim_ref[c]
        bbre_ref[c] = coef_re * bre - coef_im * bim
        bbim_ref[c] = coef_re * bim + coef_im * bre


def _s5_weights(lam_re, lam_im, log_dt, b_re, b_im, c_re, c_im):
    G, P, CH = S5_GROUPS, S5_STATE, S5_CH
    rows = 2 * G
    bt = lambda a: jnp.broadcast_to(jnp.transpose(a, (2, 0, 1))[:, None], (CH, 2, G, P)).reshape(CH, rows, P)
    shp = jax.ShapeDtypeStruct((rows, P), F32)
    shp_b = jax.ShapeDtypeStruct((CH, rows, P), F32)
    a_re, a_im, bb_re, bb_im = pl.pallas_call(
        _s5_disc_kernel, out_shape=[shp, shp, shp_b, shp_b],
    )(lam_re.reshape(rows, P), lam_im.reshape(rows, P), log_dt.reshape(rows, 1), bt(b_re), bt(b_im))
    eye = jnp.eye(G, dtype=F32)

    def in_map(bb):
        bb = jnp.transpose(bb.reshape(CH, 2, G, P), (1, 2, 0, 3))
        return jnp.einsum('dgcp,gh->dgchp', bb, eye).reshape(2, G * CH, G * P)

    w_in = jnp.concatenate([in_map(bb_re), in_map(bb_im)], axis=-1)
    out_map = lambda c: jnp.einsum('gcp,gh->gphc', c, eye).reshape(G * P, G * CH)
    w_out = jnp.concatenate([out_map(c_re), -out_map(c_im)], axis=0)
    return w_in.astype(BF16), a_re.reshape(2, 1, G * P), a_im.reshape(2, 1, G * P), w_out.astype(BF16)


def _s5_scan_kernel(b, u_ref, win_ref, are_ref, aim_ref, wout_ref, y_ref, h_ref, st_ref):
    d = pl.program_id(0)
    rows = u_ref.shape[0]
    tc = rows // b
    n = S5_LANES

    @pl.when(pl.program_id(1) == 0)
    def _():
        st_ref[...] = jnp.zeros_like(st_ref)

    h_ref[...] = _dot(u_ref[...], win_ref[0])
    chunk = 512
    for c in range(n // chunk):
        re = slice(c * chunk, (c + 1) * chunk)
        im = slice(n + c * chunk, n + (c + 1) * chunk)
        a_re = jnp.broadcast_to(are_ref[0, :, re], (b, chunk))
        a_im = jnp.broadcast_to(aim_ref[0, :, re], (b, chunk))

        def step(t, carry):
            h_re, h_im = carry
            t_eff = jnp.where(d == 0, t, tc - 1 - t)
            row = pl.ds(pl.multiple_of(t_eff * b, b), b)
            n_re = a_re * h_re - a_im * h_im + h_ref[row, re]
            n_im = a_re * h_im + a_im * h_re + h_ref[row, im]
            h_ref[row, re] = n_re
            h_ref[row, im] = n_im
            return n_re, n_im

        h_re, h_im = lax.fori_loop(0, tc, step, (st_ref[:, re], st_ref[:, im]))
        st_ref[:, re] = h_re
        st_ref[:, im] = h_im
    y_ref[0] = _dot(h_ref[...], wout_ref[...])


def _s5_scan(ps, weights, n_ctx_blocks, tc=TQ):
    t, b, w = ps.shape
    w_in, a_re, a_im, w_out = weights
    nb = t // tc
    n_lat_blocks = nb - n_ctx_blocks
    rows = tc * b

    def blk(d, j):
        bwd = jnp.where(j < n_ctx_blocks, n_ctx_blocks - 1 - j, n_ctx_blocks + nb - 1 - j)
        return jnp.where(d == 0, j, bwd)

    del n_lat_blocks
    return pl.pallas_call(
        functools.partial(_s5_scan_kernel, b),
        grid=(2, nb),
        in_specs=[pl.BlockSpec((rows, w), lambda d, j: (blk(d, j), 0)),
                  pl.BlockSpec((1, w, 2 * S5_LANES), lambda d, j: (d, 0, 0)),
                  pl.BlockSpec((1, 1, S5_LANES), lambda d, j: (d, 0, 0)),
                  pl.BlockSpec((1, 1, S5_LANES), lambda d, j: (d, 0, 0)),
                  pl.BlockSpec((2 * S5_LANES, w), lambda d, j: (0, 0))],
        out_specs=pl.BlockSpec((1, rows, w), lambda d, j: (d, blk(d, j), 0)),
        out_shape=jax.ShapeDtypeStruct((2, t * b, w), F32),
        scratch_shapes=[pltpu.VMEM((rows, 2 * S5_LANES), F32), pltpu.VMEM((b, 2 * S5_LANES), F32)],
        compiler_params=_params("arbitrary", "arbitrary"),
    )(ps.reshape(t * b, w), w_in, a_re, a_im, w_out)


def _attn_kernel(band, sink_ref, q_ref, *refs):
    if band:
        kp_ref, k0_ref, kn_ref, vp_ref, v0_ref, vn_ref, kc_ref, vc_ref, o_ref = refs
    else:
        kc_ref, vc_ref, o_ref = refs
    i = pl.program_id(1)
    nq = q_ref.shape[0]
    q = q_ref[...].astype(BF16)
    kc = kc_ref[...].astype(BF16)
    vc = vc_ref[...].astype(BF16)
    if band:
        kb = jnp.concatenate([kp_ref[...], k0_ref[...], kn_ref[...]], axis=0).astype(BF16)
        vb = jnp.concatenate([vp_ref[...], v0_ref[...], vn_ref[...]], axis=0).astype(BF16)
        nk = kb.shape[0]
        q_pos = i * nq + lax.broadcasted_iota(jnp.int32, (nq, nk), 0)
        k_pos = (i - 1) * nq + lax.broadcasted_iota(jnp.int32, (nq, nk), 1)
        n_lat = pl.num_programs(1) * nq
        mask = (jnp.abs(q_pos - k_pos) <= WINDOW) & (k_pos >= 0) & (k_pos < n_lat)
    qk = lambda a, bb: lax.dot_general(a, bb, (((1,), (1,)), ((), ())), preferred_element_type=F32)
    outs = []
    for h in range(ATT_HEADS):
        kv = h // ATT_GQ
        cols = slice(kv * HEAD_DIM, (kv + 1) * HEAD_DIM)
        qh = q[:, h * HEAD_DIM:(h + 1) * HEAD_DIM]
        sink = sink_ref[h]
        s_c = qk(qh, kc[:, cols])
        m = jnp.maximum(jnp.max(s_c, axis=1, keepdims=True), sink)
        if band:
            s_b = jnp.where(mask, qk(qh, kb[:, cols]), -jnp.inf)
            m = jnp.maximum(m, jnp.max(s_b, axis=1, keepdims=True))
        e_c = jnp.exp(s_c - m)
        den = jnp.sum(e_c, axis=1, keepdims=True) + jnp.exp(sink - m)
        o = jnp.dot(e_c.astype(BF16), vc[:, cols], preferred_element_type=F32)
        if band:
            e_b = jnp.exp(s_b - m)
            den = den + jnp.sum(e_b, axis=1, keepdims=True)
            o = o + jnp.dot(e_b.astype(BF16), vb[:, cols], preferred_element_type=F32)
        outs.append(o / den)
    o_ref[...] = jnp.concatenate(outs, axis=1)


def _attention(q, k, v, sink, n_ctx, band):
    t, b, _ = q.shape
    q2 = q.reshape(t, b * ATT_WIDTH)
    k2 = k.reshape(t, b * KV_WIDTH)
    v2 = v.reshape(t, b * KV_WIDTH)
    cb = n_ctx // ATT_BLOCK
    nq = (t - n_ctx) // ATT_BLOCK if band else cb
    off = cb if band else 0
    qspec = pl.BlockSpec((ATT_BLOCK, ATT_WIDTH), lambda bi, i: (off + i, bi))
    kvspec = lambda f: pl.BlockSpec((ATT_BLOCK, KV_WIDTH), lambda bi, i: (cb + f(i), bi))
    prev = lambda i: jnp.maximum(i - 1, 0)
    own = lambda i: i
    nxt = lambda i: jnp.minimum(i + 1, nq - 1)
    cspec = pl.BlockSpec((n_ctx, KV_WIDTH), lambda bi, i: (0, bi))
    in_specs = [pl.BlockSpec(memory_space=pltpu.SMEM), qspec]
    args = [sink, q2]
    if band:
        in_specs += [kvspec(prev), kvspec(own), kvspec(nxt)] * 2
        args += [k2, k2, k2, v2, v2, v2]
    in_specs += [cspec, cspec]
    args += [k2, v2]
    out = pl.pallas_call(
        functools.partial(_attn_kernel, band),
        grid=(b, nq),
        in_specs=in_specs,
        out_specs=pl.BlockSpec((ATT_BLOCK, ATT_WIDTH), lambda bi, i: (i, bi)),
        out_shape=jax.ShapeDtypeStruct((nq * ATT_BLOCK, b * ATT_WIDTH), F32),
        compiler_params=_params("arbitrary", "arbitrary"),
    )(*args)
    return out.reshape(nq * ATT_BLOCK, b, ATT_WIDTH)


def _gelu_tanh(x):
    return 0.5 * x * (1.0 + jnp.tanh(math.sqrt(2.0 / math.pi) * (x + 0.044715 * (x * x * x))))


def _outproj_kernel(x_ref, gt_ref, yr_ref, bonus_ref, gate_ref, u_ref, ys_ref, ya_ref,
                    hsum_ref, gnw_ref, gnb_ref, dskip_ref, gluw_ref, glub_ref, wo_ref, lng_ref, lnb_ref, o_ref):
    tq, b, d = x_ref.shape
    rows = tq * b
    flat = lambda ref: ref[...].reshape(rows, ref.shape[-1])
    hsum = hsum_ref[...]
    inv_n = 1.0 / HEAD_DIM
    y = flat(yr_ref)
    yc = y - _dot_hi(y, hsum) * inv_n
    var = _dot_hi(yc * yc, hsum) * inv_n
    out_r = (yc * lax.rsqrt(var + GN_EPS) * gnw_ref[...] + gnb_ref[...] + flat(bonus_ref)) * flat(gate_ref)
    z = _gelu_tanh(dskip_ref[...] * flat(u_ref) + ys_ref[0] + ys_ref[1])
    out_s = z * _sigmoid(_dot(z, gluw_ref[...]) + glub_ref[...])
    W = RWKV_WIDTH
    mix = _dot(out_r, wo_ref[0:W, :]) + _dot(out_s, wo_ref[W:W + S5_WIDTH, :]) + _dot(flat(ya_ref), wo_ref[W + S5_WIDTH:, :])
    x = x_ref[...]
    res = DEEPNORM_ALPHA * x + gt_ref[0] * mix.reshape(tq, b, d)
    o_ref[...] = _layer_norm(res, lng_ref[...], lnb_ref[...])


def _output_projection(xt, mod_tab, yr, bonus, gate, u, ys, ya, lp, n_ctx_blocks, first_block):
    t, b, d = xt.shape
    nb = t // TQ - first_block
    seg = lambda i: ((i + first_block) >= n_ctx_blocks).astype(jnp.int32)
    blk = lambda w: pl.BlockSpec((TQ, b, w), lambda i: (i + first_block, 0, 0))
    full = lambda a: pl.BlockSpec(a.shape, lambda i: (0,) * a.ndim)
    consts = (lp['hsum'], lp['gn_w'], lp['gn_b'], lp['s5_d'], lp['glu_w'], lp['glu_b'], lp['w_out'], lp['ln1_g'], lp['ln1_b'])
    return pl.pallas_call(
        _outproj_kernel,
        grid=(nb,),
        in_specs=[blk(d), pl.BlockSpec((1, b, d), lambda i: (seg(i), 0, 2)),
                  blk(RWKV_WIDTH), blk(RWKV_WIDTH), blk(RWKV_WIDTH), blk(S5_WIDTH),
                  pl.BlockSpec((2, TQ * b, S5_WIDTH), lambda i: (0, i + first_block, 0)),
                  pl.BlockSpec((TQ, b, ATT_WIDTH), lambda i: (i, 0, 0))] + [full(a) for a in consts],
        out_specs=pl.BlockSpec((TQ, b, d), lambda i: (i, 0, 0)),
        out_shape=jax.ShapeDtypeStruct((nb * TQ, b, d), F32),
        compiler_params=_params("arbitrary"),
    )(xt, mod_tab, yr, bonus, gate, u, ys, ya, *consts)


def _router_kernel(x_ref, sc_ref, sh_ref, w_ref, b_ref, h_ref, idx_ref, wt_ref):
    tq, b, d = x_ref.shape
    rows = tq * b
    h = (x_ref[...] * (1.0 + sc_ref[0]) + sh_ref[0]).reshape(rows, d)
    for c in range(d // LANES):
        h_ref[pl.ds(c, rows, stride=d // LANES), :] = h[:, c * LANES:(c + 1) * LANES]
    logits = _dot_hi(h, w_ref[...]) + b_ref[...]
    lane = lax.broadcasted_iota(jnp.int32, logits.shape, 1)
    neg = -jnp.inf
    first_max = lambda vals, m: jnp.min(jnp.where(vals == m, lane, LANES), axis=1, keepdims=True)
    gl = jnp.where(lane < N_GROUPS, logits, neg)
    g_max = jnp.max(gl, axis=1, keepdims=True)
    g_idx = first_max(gl, g_max)
    g_prob = 1.0 / jnp.sum(jnp.exp(gl - g_max), axis=1, keepdims=True)
    lo = N_GROUPS + EXPERTS_PER_GROUP * g_idx
    el = jnp.where((lane >= lo) & (lane < lo + EXPERTS_PER_GROUP), logits, neg)
    m1 = jnp.max(el, axis=1, keepdims=True)
    i1 = first_max(el, m1)
    el2 = jnp.where(lane == i1, neg, el)
    m2 = jnp.max(el2, axis=1, keepdims=True)
    i2 = first_max(el2, m2)
    e2 = jnp.exp(m2 - m1)
    p1 = 1.0 / (1.0 + e2)
    p2 = e2 / (1.0 + e2)
    idx_ref[...] = jnp.where(lane == 0, i1 - N_GROUPS, jnp.where(lane == 1, i2 - N_GROUPS, 0))
    wt_ref[...] = jnp.where(lane == 0, g_prob * p1, jnp.where(lane == 1, g_prob * p2, 0.0))


def _router(x1, mod_tab, w_r, b_r, n_ctx_blocks, first_block):
    t, b, d = x1.shape
    nb = t // TQ
    seg = lambda i: ((i + first_block) >= n_ctx_blocks).astype(jnp.int32)
    rows = TQ * b
    n = nb * rows
    return pl.pallas_call(
        _router_kernel,
        grid=(nb,),
        in_specs=[pl.BlockSpec((TQ, b, d), lambda i: (i, 0, 0)),
                  pl.BlockSpec((1, b, d), lambda i: (seg(i), 0, 4)),
                  pl.BlockSpec((1, b, d), lambda i: (seg(i), 0, 3)),
                  pl.BlockSpec(w_r.shape, lambda i: (0, 0)),
                  pl.BlockSpec(b_r.shape, lambda i: (0, 0))],
        out_specs=[pl.BlockSpec((rows * (d // LANES), LANES), lambda i: (i, 0)),
                   pl.BlockSpec((rows, LANES), lambda i: (i, 0)),
                   pl.BlockSpec((rows, LANES), lambda i: (i, 0))],
        out_shape=[jax.ShapeDtypeStruct((n * (d // LANES), LANES), F32),
                   jax.ShapeDtypeStruct((n, LANES), jnp.int32),
                   jax.ShapeDtypeStruct((n, LANES), F32)],
        compiler_params=_params("arbitrary"),
    )(x1, mod_tab, mod_tab, w_r, b_r)


def _dispatch_plan(experts, n_tok, slab_rows):
    a = n_tok * 2
    p = -(-a // MOE_BLOCK) * MOE_BLOCK + N_EXPERTS * MOE_BLOCK
    n_blk = p // MOE_BLOCK
    flat_e = experts.reshape(-1)
    ck = 256
    onehot = (flat_e.reshape(a // ck, ck, 1) == jnp.arange(N_EXPERTS, dtype=jnp.int32)).astype(F32)
    tri = (jnp.arange(ck)[:, None] > jnp.arange(ck)[None, :]).astype(F32)
    within = jnp.einsum('ij,cje->cie', tri, onehot)
    tot = jnp.sum(onehot, axis=1)
    before = jnp.cumsum(tot, axis=0) - tot
    rank = jnp.sum((within + before[:, None, :]) * onehot, axis=-1).astype(jnp.int32).reshape(a)
    counts = jnp.sum(tot, axis=0).astype(jnp.int32)
    padded = (counts + MOE_BLOCK - 1) // MOE_BLOCK * MOE_BLOCK
    pend = jnp.cumsum(padded)
    dest = (pend - padded)[flat_e] + rank
    slab = slab_rows
    a_idx = jnp.arange(a, dtype=jnp.int32)
    src = jnp.zeros((p,), jnp.int32).at[dest].set((a_idx // 2) * slab)
    spare = (n_tok + jnp.arange(p, dtype=jnp.int32) % (2 * MOE_BLOCK)) * slab
    dst = spare.at[dest].set(((a_idx % 2) * (n_tok + 2 * MOE_BLOCK) + a_idx // 2) * slab)
    n_used = pend[-1] // MOE_BLOCK
    blk = jnp.minimum(jnp.arange(n_blk, dtype=jnp.int32), n_used - 1) * MOE_BLOCK
    blk_exp = jnp.minimum(jnp.sum((pend[None, :] <= blk[:, None]).astype(jnp.int32), axis=1), N_EXPERTS - 1)
    return (blk_exp, n_used.reshape(1).astype(jnp.int32),
            src.reshape(n_blk, MOE_BLOCK), dst.reshape(n_blk, MOE_BLOCK), n_blk)


def _expert_kernel(n_tok, blk_exp_ref, n_used_ref, src_ref, dst_ref, h_hbm, wg_ref, wu_ref, wd_ref, y_hbm,
                   xbuf, ybuf, gsem, ssem):
    i = pl.program_id(0)
    n_used = n_used_ref[0]
    slot = i % 2
    slab = xbuf.shape[1] // MOE_BLOCK
    tok_rows = lambda r: pl.ds(pl.multiple_of(r * slab, slab), slab)

    def gather(blk, sl):
        def body(r, c):
            src = pl.multiple_of(src_ref[blk, r], slab)
            pltpu.make_async_copy(h_hbm.at[pl.ds(src, slab), :], xbuf.at[sl, tok_rows(r), :], gsem.at[sl]).start()
            return c
        lax.fori_loop(0, MOE_BLOCK, body, 0, unroll=8)

    def scatter(blk, sl):
        def body(r, c):
            dst = pl.multiple_of(dst_ref[blk, r], slab)
            pltpu.make_async_copy(ybuf.at[sl, tok_rows(r), :], y_hbm.at[pl.ds(dst, slab), :], ssem.at[sl]).start()
            return c
        lax.fori_loop(0, MOE_BLOCK, body, 0, unroll=8)

    wait_gather = lambda sl: pltpu.make_async_copy(xbuf.at[sl], xbuf.at[sl], gsem.at[sl]).wait()
    wait_scatter = lambda sl: pltpu.make_async_copy(ybuf.at[sl], ybuf.at[sl], ssem.at[sl]).wait()

    @pl.when(i == 0)
    def _():
        gather(0, 0)
        ybuf[...] = jnp.zeros_like(ybuf)
        for j in range(2):
            for sl in range(2):
                first = (j * (n_tok + 2 * MOE_BLOCK) + n_tok + sl * MOE_BLOCK) * slab
                spare = pltpu.make_async_copy(ybuf.at[sl], y_hbm.at[pl.ds(first, MOE_BLOCK * slab), :], ssem.at[sl])
                spare.start()
                spare.wait()

    @pl.when(i < n_used)
    def _():
        wait_gather(slot)

        @pl.when(i + 1 < n_used)
        def _():
            gather(i + 1, 1 - slot)

        x = jnp.concatenate([xbuf[slot, pl.ds(c, MOE_BLOCK, stride=slab), :] for c in range(slab)], axis=1).astype(BF16)
        g = jnp.dot(x, wg_ref[0], preferred_element_type=F32)
        u = jnp.dot(x, wu_ref[0], preferred_element_type=F32)
        act = (g * _sigmoid(g)) * u
        y = jnp.dot(act.astype(BF16), wd_ref[0], preferred_element_type=F32)

        @pl.when(i >= 2)
        def _():
            wait_scatter(slot)

        for c in range(slab):
            ybuf[slot, pl.ds(c, MOE_BLOCK, stride=slab), :] = y[:, c * LANES:(c + 1) * LANES]
        scatter(i, slot)

        @pl.when(i == n_used - 1)
        def _():
            wait_scatter(slot)

            @pl.when(i >= 1)
            def _():
                wait_scatter(1 - slot)


def _routed_experts(h, n_tok, plan, wg, wu, wd):
    d = wg.shape[1]
    slab = d // LANES
    blk_exp, n_used, src, dst, n_blk = plan
    wspec = lambda s: pl.BlockSpec((1,) + s, lambda i, be, nu, sr, ds: (be[i], 0, 0))
    y = pl.pallas_call(
        functools.partial(_expert_kernel, n_tok),
        grid_spec=pltpu.PrefetchScalarGridSpec(
            num_scalar_prefetch=4,
            grid=(n_blk,),
            in_specs=[pl.BlockSpec(memory_space=pl.ANY), wspec((d, D_EXPERT)), wspec((d, D_EXPERT)), wspec((D_EXPERT, d))],
            out_specs=pl.BlockSpec(memory_space=pl.ANY),
            scratch_shapes=[pltpu.VMEM((2, MOE_BLOCK * slab, LANES), F32), pltpu.VMEM((2, MOE_BLOCK * slab, LANES), F32),
                            pltpu.SemaphoreType.DMA((2,)), pltpu.SemaphoreType.DMA((2,))]),
        out_shape=jax.ShapeDtypeStruct((2 * (n_tok + 2 * MOE_BLOCK) * slab, LANES), F32),
        compiler_params=_params("arbitrary"),
    )(blk_exp, n_used, src, dst, h, wg, wu, wd)
    return y.reshape(2, (n_tok + 2 * MOE_BLOCK) * slab, LANES)


def _combine_kernel(x_ref, gt_ref, y_ref, wt_ref, lng_ref, lnb_ref, o_ref):
    tq, b, d = x_ref.shape
    rows = tq * b
    slab = d // LANES
    wt = wt_ref[...]
    w0, w1 = wt[:, 0:1], wt[:, 1:2]
    y = jnp.concatenate([y_ref[0, pl.ds(c, rows, stride=slab), :] * w0 + y_ref[1, pl.ds(c, rows, stride=slab), :] * w1
                         for c in range(slab)], axis=1)
    res = DEEPNORM_ALPHA * x_ref[...] + gt_ref[0] * y.reshape(tq, b, d)
    o_ref[...] = _layer_norm(res, lng_ref[...], lnb_ref[...])


def _moe_combine(x1, mod_tab, y, wts, ln_g, ln_b, n_ctx_blocks, first_block):
    t, b, d = x1.shape
    nb = t // TQ
    rows = TQ * b
    seg = lambda i: ((i + first_block) >= n_ctx_blocks).astype(jnp.int32)
    blk = pl.BlockSpec((TQ, b, d), lambda i: (i, 0, 0))
    return pl.pallas_call(
        _combine_kernel,
        grid=(nb,),
        in_specs=[blk, pl.BlockSpec((1, b, d), lambda i: (seg(i), 0, 5)),
                  pl.BlockSpec((2, rows * (d // LANES), LANES), lambda i: (0, i, 0)),
                  pl.BlockSpec((rows, LANES), lambda i: (i, 0)),
                  pl.BlockSpec(ln_g.shape, lambda i: (0, 0)), pl.BlockSpec(ln_b.shape, lambda i: (0, 0))],
        out_specs=blk,
        out_shape=jax.ShapeDtypeStruct((t, b, d), F32),
        compiler_params=_params("arbitrary"),
    )(x1, mod_tab, y, wts, ln_g, ln_b)


def _layer(xt, c_all, lp, rope, n_ctx, last):
    t, b, d = xt.shape
    ncb = n_ctx // TQ
    first = ncb if last else 0
    mod = _modulation(c_all, lp['w_mod'], lp['b_mod'])
    mod_tab = jnp.stack([jnp.broadcast_to(mod[b:b + 1], (b, mod.shape[1])), mod[:b]], axis=0)

    pr, ps, q, k, v = _input_projection(xt, mod_tab, lp['w_in'], rope, ncb)

    r, vv, kk, d0, d1, k0, k1, b0, b1, bonus, gate = _rwkv_features(pr, lp, ncb)
    sl = _to_scan_layout
    yf, yb = _rwkv_scan((sl(r, r), sl(vv, vv), sl(kk, kk), sl(d0, d1), sl(k0, k1), sl(b0, b1)), n_ctx)
    yr = _from_scan_layout(yf, yb, b)

    ys = _s5_scan(ps, lp['s5'], ncb)

    ya = _attention(q, k, v, lp['sink'], n_ctx, True)
    if not last:
        ya = jnp.concatenate([_attention(q, k, v, lp['sink'], n_ctx, False), ya], axis=0)

    x1 = _output_projection(xt, mod_tab, yr, bonus, gate, ps, ys, ya, lp, ncb, first)
    h, idx, wts = _router(x1, mod_tab, lp['w_router'], lp['b_router'], ncb, first)
    n_tok = idx.shape[0]
    plan = _dispatch_plan(idx[:, :2], n_tok, d // LANES)
    y = _routed_experts(h, n_tok, plan, lp['w_gate'], lp['w_up'], lp['w_down'])
    return _moe_combine(x1, mod_tab, y, wts, lp['ln2_g'], lp['ln2_b'], ncb, first)


def kernel(x, c, ctx, c_ctx, w_mod, b_mod, w_in, rwkv_conv, rwkv_w0, rwkv_w2, rwkv_a0, rwkv_a2, rwkv_g2, rwkv_k_k, rwkv_k_a, rwkv_r_k, rwkv_gn_w, rwkv_gn_b, s5_lam_re, s5_lam_im, s5_log_dt, s5_b_re, s5_b_im, s5_c_re, s5_c_im, s5_d, s5_glu_w, s5_glu_b, attn_sink, w_out, ln1_g, ln1_b, ln2_g, ln2_b, router_group_w, router_group_b, router_expert_w, router_expert_b, expert_w_gate, expert_w_up, expert_w_down):
    bsz, seq, d = x.shape
    n_ctx = ctx.shape[1]
    depth = w_mod.shape[0]
    W = RWKV_WIDTH
    xt = jnp.transpose(jnp.concatenate([ctx, x], axis=1), (1, 0, 2))
    c_all = jnp.zeros((bsz + 8, d), F32).at[:bsz].set(c).at[bsz].set(c_ctx)
    rope = _rope_tables(n_ctx, seq)
    hsum = jnp.kron(jnp.eye(RWKV_HEADS, dtype=F32), jnp.ones((HEAD_DIM, HEAD_DIM), F32))
    row = lambda a: a.reshape(1, -1)
    zeros_lo = jnp.zeros((RWKV_LORA // 2, 2 * W), F32)
    for i in range(depth):
        n_r = N_GROUPS + N_EXPERTS
        lp = {
            'w_mod': w_mod[i], 'b_mod': b_mod[i], 'w_in': w_in[i].astype(BF16),
            'conv': rwkv_conv[i], 'w0': rwkv_w0[i], 'a0': rwkv_a0[i],
            'w2': jnp.concatenate([jnp.concatenate([rwkv_w2[i, 0], rwkv_w2[i, 1]], axis=1), zeros_lo], axis=0),
            'a2': jnp.concatenate([zeros_lo, jnp.concatenate([rwkv_a2[i, 0], rwkv_a2[i, 1]], axis=1)], axis=0),
            'g2': rwkv_g2[i].astype(BF16), 'k_k': row(rwkv_k_k[i]), 'k_a': row(rwkv_k_a[i]), 'r_k': row(rwkv_r_k[i]),
            'hsum': hsum, 'gn_w': row(rwkv_gn_w[i]), 'gn_b': row(rwkv_gn_b[i]),
            's5': _s5_weights(s5_lam_re[i], s5_lam_im[i], s5_log_dt[i], s5_b_re[i], s5_b_im[i], s5_c_re[i], s5_c_im[i]),
            's5_d': row(s5_d[i]), 'glu_w': s5_glu_w[i].astype(BF16), 'glu_b': row(s5_glu_b[i]),
            'sink': attn_sink[i], 'w_out': w_out[i].astype(BF16),
            'ln1_g': row(ln1_g[i]), 'ln1_b': row(ln1_b[i]), 'ln2_g': row(ln2_g[i]), 'ln2_b': row(ln2_b[i]),
            'w_router': jnp.zeros((d, LANES), F32).at[:, :N_GROUPS].set(router_group_w[i]).at[:, N_GROUPS:n_r].set(router_expert_w[i]),
            'b_router': jnp.zeros((1, LANES), F32).at[0, :N_GROUPS].set(router_group_b[i]).at[0, N_GROUPS:n_r].set(router_expert_b[i]),
            'w_gate': expert_w_gate[i].astype(BF16), 'w_up': expert_w_up[i].astype(BF16), 'w_down': expert_w_down[i].astype(BF16),
        }
        xt = _layer(xt, c_all, lp, rope, n_ctx, i == depth - 1)
    return jnp.transpose(xt, (1, 0, 2))
```

```python
import functools
import math

import jax
import jax.numpy as jnp
from jax import lax
from jax.experimental import pallas as pl
from jax.experimental.pallas import tpu as pltpu

F32 = jnp.float32
BF16 = jnp.bfloat16

HEAD_DIM = 64
RWKV_WIDTH = 256
RWKV_HEADS = 4
RWKV_LORA = 128
S5_WIDTH = 256
S5_GROUPS = 16
S5_CH = 16
S5_STATE = 64
S5_LANES = S5_GROUPS * S5_STATE
ATT_WIDTH = 512
ATT_HEADS = 8
ATT_KV_HEADS = 2
ATT_GQ = 4
KV_WIDTH = ATT_KV_HEADS * HEAD_DIM
WINDOW = 128
ATT_BLOCK = 128
GRID_W = 64
ROPE_BASE = 10000.0
N_GROUPS = 4
EXPERTS_PER_GROUP = 8
N_EXPERTS = 32
D_EXPERT = 512
MOE_BLOCK = 128
N_MOD = 6
DEPTH = 2
DEEPNORM_ALPHA = (2.0 * DEPTH) ** 0.25
LN_EPS = 1e-5
GN_EPS = 64e-5
LANES = 128
TQ = 32
SCAN_UNROLL = 32
VMEM_LIMIT = 48 * 1024 * 1024


def _params(*sem):
    return pltpu.CompilerParams(dimension_semantics=sem, vmem_limit_bytes=VMEM_LIMIT)


def _dot(a, b):
    return jnp.dot(a.astype(BF16), b.astype(BF16), preferred_element_type=F32)


def _dot_hi(a, b):
    return jnp.dot(a, b, precision=lax.Precision.HIGHEST, preferred_element_type=F32)


def _sigmoid(x):
    return 1.0 / (1.0 + jnp.exp(-x))


def _layer_norm(x, g, b):
    mu = jnp.mean(x, axis=-1, keepdims=True)
    xc = x - mu
    var = jnp.mean(xc * xc, axis=-1, keepdims=True)
    return xc * lax.rsqrt(var + LN_EPS) * g + b


def _mod_kernel(c_ref, w_ref, b_ref, o_ref):
    c = c_ref[...]
    o_ref[...] = _dot_hi(c * _sigmoid(c), w_ref[...]) + b_ref[...]


def _modulation(c_all, w_mod, b_mod):
    rows, d = c_all.shape
    n = w_mod.shape[1]
    return pl.pallas_call(
        _mod_kernel,
        grid=(n // d,),
        in_specs=[pl.BlockSpec((rows, d), lambda j: (0, 0)),
                  pl.BlockSpec((d, d), lambda j: (0, j)),
                  pl.BlockSpec((1, d), lambda j: (0, j))],
        out_specs=pl.BlockSpec((rows, d), lambda j: (0, j)),
        out_shape=jax.ShapeDtypeStruct((rows, n), F32),
        compiler_params=_params("arbitrary"),
    )(c_all, w_mod, b_mod.reshape(1, n))


def _swap_halves(t):
    n = t.shape[-1]
    lane = lax.broadcasted_iota(jnp.int32, t.shape, t.ndim - 1)
    first = (lane % HEAD_DIM) < (HEAD_DIM // 2)
    return jnp.where(first, pltpu.roll(t, n - HEAD_DIM // 2, t.ndim - 1), pltpu.roll(t, HEAD_DIM // 2, t.ndim - 1))


def _inproj_kernel(x_ref, sc_ref, sh_ref, w_ref, cq_ref, sq_ref, ck_ref, sk_ref,
                   pr_ref, ps_ref, q_ref, k_ref, v_ref):
    tq, b, d = x_ref.shape
    h = x_ref[...] * (1.0 + sc_ref[0]) + sh_ref[0]
    p = _dot(h.reshape(tq * b, d), w_ref[...])
    o_r = RWKV_WIDTH * 3 + RWKV_LORA + 128
    o_s = o_r + S5_WIDTH
    o_q = o_s + ATT_WIDTH
    o_k = o_q + KV_WIDTH
    pr_ref[...] = p[:, :o_r].reshape(tq, b, o_r)
    ps_ref[...] = p[:, o_r:o_s].reshape(tq, b, S5_WIDTH)
    q = p[:, o_s:o_q]
    k = p[:, o_q:o_k]
    bc = lambda r, w: jnp.broadcast_to(r[...], (tq, b, w)).reshape(tq * b, w)
    q = q * bc(cq_ref, ATT_WIDTH) + _swap_halves(q) * bc(sq_ref, ATT_WIDTH)
    k = k * bc(ck_ref, KV_WIDTH) + _swap_halves(k) * bc(sk_ref, KV_WIDTH)
    q_ref[...] = q.reshape(tq, b, ATT_WIDTH)
    k_ref[...] = k.reshape(tq, b, KV_WIDTH)
    v_ref[...] = p[:, o_k:].reshape(tq, b, KV_WIDTH)


def _input_projection(xt, mod_tab, w_in_bf, rope, n_ctx_blocks):
    t, b, d = xt.shape
    d_in = w_in_bf.shape[1]
    seg = lambda i: (i >= n_ctx_blocks).astype(jnp.int32)
    cq, sq, ck, sk = rope
    widths = (RWKV_WIDTH * 3 + RWKV_LORA + 128, S5_WIDTH, ATT_WIDTH, KV_WIDTH, KV_WIDTH)
    tab = lambda w: pl.BlockSpec((TQ, 1, w), lambda i: (i, 0, 0))
    return pl.pallas_call(
        _inproj_kernel,
        grid=(t // TQ,),
        in_specs=[pl.BlockSpec((TQ, b, d), lambda i: (i, 0, 0)),
                  pl.BlockSpec((1, b, d), lambda i: (seg(i), 0, 1)),
                  pl.BlockSpec((1, b, d), lambda i: (seg(i), 0, 0)),
                  pl.BlockSpec((d, d_in), lambda i: (0, 0)),
                  tab(ATT_WIDTH), tab(ATT_WIDTH), tab(KV_WIDTH), tab(KV_WIDTH)],
        out_specs=[pl.BlockSpec((TQ, b, w), lambda i: (i, 0, 0)) for w in widths],
        out_shape=[jax.ShapeDtypeStruct((t, b, w), F32) for w in widths],
        compiler_params=_params("arbitrary"),
    )(xt, mod_tab, mod_tab, w_in_bf, cq, sq, ck, sk)


def _rope_tables(n_ctx, n_lat):
    rows = n_lat // GRID_W
    row_id, col_id = jnp.meshgrid(jnp.arange(rows, dtype=F32), jnp.arange(GRID_W, dtype=F32), indexing='ij')
    n_freq = HEAD_DIM // 4
    inv_freq = ROPE_BASE ** (-jnp.arange(n_freq, dtype=F32) / n_freq)
    ang = jnp.concatenate([row_id.reshape(-1, 1) * inv_freq, col_id.reshape(-1, 1) * inv_freq], axis=-1)
    cos, sin = jnp.cos(ang), jnp.sin(ang)
    cos_h = jnp.concatenate([jnp.ones((n_ctx, HEAD_DIM), F32), jnp.concatenate([cos, cos], axis=-1)], axis=0)
    sin_h = jnp.concatenate([jnp.zeros((n_ctx, HEAD_DIM), F32), jnp.concatenate([-sin, sin], axis=-1)], axis=0)
    scale = HEAD_DIM ** -0.5
    expand = lambda tb, n, s: (jnp.tile(tb, (1, n)) * s)[:, None, :]
    return (expand(cos_h, ATT_HEADS, scale), expand(sin_h, ATT_HEADS, scale),
            expand(cos_h, ATT_KV_HEADS, 1.0), expand(sin_h, ATT_KV_HEADS, 1.0))


def _rwkv_feat_kernel(n_ctx_blocks, cur_ref, prev_ref, next_ref, conv_ref, w0_ref, a0_ref, w2_ref, a2_ref, g2_ref,
                      kk_w_ref, ka_ref, rk_ref, hsum_ref,
                      r_ref, v_ref, kk_ref, d0_ref, d1_ref, k0_ref, k1_ref, b0_ref, b1_ref, bonus_ref, gate_ref):
    i = pl.program_id(0)
    tq, b, w = cur_ref.shape
    rows = tq * b
    cur = cur_ref[...]
    keep_prev = jnp.where((i == 0) | (i == n_ctx_blocks), 0.0, 1.0)
    keep_next = jnp.where((i == n_ctx_blocks - 1) | (i == pl.num_programs(0) - 1), 0.0, 1.0)
    before = jnp.concatenate([prev_ref[...] * keep_prev, cur[:-1]], axis=0)
    after = jnp.concatenate([cur[1:], next_ref[...] * keep_next], axis=0)
    pc = (before * conv_ref[0:1, :] + cur * conv_ref[1:2, :] + after * conv_ref[2:3, :]).reshape(rows, w)
    W = RWKV_WIDTH
    r, k, v = pc[:, :W], pc[:, W:2 * W], pc[:, 2 * W:3 * W]
    lo = pc[:, 3 * W:3 * W + RWKV_LORA]
    g_lo = pc[:, 3 * W + RWKV_LORA:]
    hsum = hsum_ref[...]
    kk = k * kk_w_ref[...]
    kk = kk * lax.rsqrt(_dot_hi(kk * kk, hsum) + 1e-12)
    wl = _dot_hi(jnp.tanh(lo), w2_ref[...])
    al = _dot_hi(lo, a2_ref[...])
    k_sum = None
    for d, (d_ref, kd_ref, b_ref) in enumerate(((d0_ref, k0_ref, b0_ref), (d1_ref, k1_ref, b1_ref))):
        z = -(w0_ref[d:d + 1, :] + wl[:, d * W:(d + 1) * W])
        softplus = jnp.maximum(z, 0.0) + jnp.log(1.0 + jnp.exp(-jnp.abs(z)))
        decay = jnp.exp(-jnp.exp(-softplus - 0.5))
        a = _sigmoid(a0_ref[d:d + 1, :] + al[:, d * W:(d + 1) * W])
        k_d = k * (1.0 + (a - 1.0) * ka_ref[...])
        d_ref[...] = decay.reshape(tq, b, W)
        kd_ref[...] = k_d.reshape(tq, b, W)
        b_ref[...] = (kk * a).reshape(tq, b, W)
        k_sum = k_d if k_sum is None else k_sum + k_d
    r_ref[...] = r.reshape(tq, b, W)
    v_ref[...] = v.reshape(tq, b, W)
    kk_ref[...] = kk.reshape(tq, b, W)
    bonus_ref[...] = (_dot_hi(r * k_sum * rk_ref[...], hsum) * v).reshape(tq, b, W)
    gate_ref[...] = _dot(_sigmoid(g_lo), g2_ref[...]).reshape(tq, b, W)


def _rwkv_features(pr, lp, n_ctx_blocks):
    t, b, w = pr.shape
    nb = t // TQ
    W = RWKV_WIDTH
    full = lambda a: pl.BlockSpec(a.shape, lambda i: (0,) * a.ndim)
    consts = (lp['conv'], lp['w0'], lp['a0'], lp['w2'], lp['a2'], lp['g2'], lp['k_k'], lp['k_a'], lp['r_k'], lp['hsum'])
    return pl.pallas_call(
        functools.partial(_rwkv_feat_kernel, n_ctx_blocks),
        grid=(nb,),
        in_specs=[pl.BlockSpec((TQ, b, w), lambda i: (i, 0, 0)),
                  pl.BlockSpec((1, b, w), lambda i: (jnp.maximum(i * TQ - 1, 0), 0, 0)),
                  pl.BlockSpec((1, b, w), lambda i: (jnp.minimum((i + 1) * TQ, t - 1), 0, 0))]
                 + [full(a) for a in consts],
        out_specs=[pl.BlockSpec((TQ, b, W), lambda i: (i, 0, 0))] * 11,
        out_shape=[jax.ShapeDtypeStruct((t, b, W), F32)] * 11,
        compiler_params=_params("arbitrary"),
    )(pr, pr, pr, *consts)


def _rwkv_scan_kernel(rf_ref, vf_ref, kkf_ref, wf_ref, kf_ref, bf_ref, rb_ref, vb_ref, kkb_ref, wb_ref, kb_ref, bb_ref,
                      yf_ref, yb_ref, s_ref, x_ref):
    @pl.when(pl.program_id(0) == 0)
    def _():
        s_ref[...] = jnp.zeros_like(s_ref)

    n = HEAD_DIM
    tc, _, lanes = rf_ref.shape
    R, V, KK, W, K, B, WR = range(7)
    pairs = ((rf_ref, rb_ref), (vf_ref, vb_ref), (kkf_ref, kkb_ref), (wf_ref, wb_ref), (kf_ref, kb_ref), (bf_ref, bb_ref))
    is_fwd = lax.broadcasted_iota(jnp.int32, (n, lanes), 1) < lanes // 2

    def step(t, carry):
        tb = tc - 1 - t
        for q, (f_ref, b_ref) in enumerate(pairs):
            x_ref[q] = jnp.where(is_fwd, f_ref[t], b_ref[tb])
        x_ref[WR] = x_ref[W] * x_ref[R]

        def reduce_pass(jg, acc):
            s_kk, z = acc
            for jj in range(SCAN_UNROLL):
                j = jg * SCAN_UNROLL + jj
                s = s_ref[j]
                s_kk = s_kk + s * x_ref[KK, pl.ds(j, 1), :]
                z = z + s * x_ref[WR, pl.ds(j, 1), :]
            return s_kk, z

        zero = jnp.zeros((n, lanes), F32)
        s_kk, z = lax.fori_loop(0, n // SCAN_UNROLL, reduce_pass, (zero, zero))
        v = x_ref[V]

        def update_pass(jg, c):
            for jj in range(SCAN_UNROLL):
                j = jg * SCAN_UNROLL + jj
                s_ref[j] = (s_ref[j] * x_ref[W, pl.ds(j, 1), :] - s_kk * x_ref[B, pl.ds(j, 1), :]
                            + v * x_ref[K, pl.ds(j, 1), :])
            return c

        lax.fori_loop(0, n // SCAN_UNROLL, update_pass, 0)
        r = x_ref[R]
        br = jnp.sum(x_ref[B] * r, axis=0, keepdims=True)
        kr = jnp.sum(x_ref[K] * r, axis=0, keepdims=True)
        y = z - s_kk * br + v * kr
        yf_ref[t] = y
        yb_ref[tb] = y
        return carry

    lax.fori_loop(0, tc, step, 0)


def _rwkv_scan(streams, n_ctx, tc=8):
    t, _, lanes = streams[0].shape
    nb, ncb = t // tc, n_ctx // tc
    mirror = lambda i: jnp.where(i < ncb, ncb - 1 - i, ncb + nb - 1 - i)
    fspec = pl.BlockSpec((tc, HEAD_DIM, lanes), lambda i: (i, 0, 0))
    bspec = pl.BlockSpec((tc, HEAD_DIM, lanes), lambda i: (mirror(i), 0, 0))
    out = jax.ShapeDtypeStruct((t, HEAD_DIM, lanes), F32)
    return pl.pallas_call(
        _rwkv_scan_kernel,
        grid=(nb,),
        in_specs=[fspec] * 6 + [bspec] * 6,
        out_specs=[fspec, bspec],
        out_shape=[out, out],
        scratch_shapes=[pltpu.VMEM((HEAD_DIM, HEAD_DIM, lanes), F32), pltpu.VMEM((7, HEAD_DIM, lanes), F32)],
        compiler_params=_params("arbitrary"),
    )(*streams, *streams)


def _to_scan_layout(fwd, bwd):
    t, b, _ = fwd.shape
    tr = lambda a: jnp.transpose(a.reshape(t, b, RWKV_HEADS, HEAD_DIM), (0, 3, 1, 2)).reshape(t, HEAD_DIM, b * RWKV_HEADS)
    return jnp.concatenate([tr(fwd), tr(bwd)], axis=-1)


def _from_scan_layout(yf, yb, b):
    t, _, lanes = yf.shape
    y = yf[:, :, :lanes // 2] + yb[:, :, lanes // 2:]
    return jnp.transpose(y.reshape(t, HEAD_DIM, b, RWKV_HEADS), (0, 2, 3, 1)).reshape(t, b, RWKV_WIDTH)


def _s5_disc_kernel(lre_ref, lim_ref, ldt_ref, bre_ref, bim_ref, are_ref, aim_ref, bbre_ref, bbim_ref):
    lre, lim = lre_ref[...], lim_ref[...]
    dt = jnp.exp(ldt_ref[...])
    mag = jnp.exp(lre * dt)
    ab_re = mag * jnp.cos(lim * dt)
    ab_im = mag * jnp.sin(lim * dt)
    nr, ni = ab_re - 1.0, ab_im
    den = lre * lre + lim * lim
    coef_re = (nr * lre + ni * lim) / den
    coef_im = (ni * lre - nr * lim) / den
    are_ref[...] = ab_re
    aim_ref[...] = ab_im
    for c in range(S5_CH):
        bre, bim = bre_ref[c], bim_ref[c]
        bbre_ref[c] = coef_re * bre - coef_im * bim
        bbim_ref[c] = coef_re * bim + coef_im * bre


def _s5_weights(lam_re, lam_im, log_dt, b_re, b_im, c_re, c_im):
    G, P, CH = S5_GROUPS, S5_STATE, S5_CH
    rows = 2 * G
    bt = lambda a: jnp.broadcast_to(jnp.transpose(a, (2, 0, 1))[:, None], (CH, 2, G, P)).reshape(CH, rows, P)
    shp = jax.ShapeDtypeStruct((rows, P), F32)
    shp_b = jax.ShapeDtypeStruct((CH, rows, P), F32)
    a_re, a_im, bb_re, bb_im = pl.pallas_call(
        _s5_disc_kernel, out_shape=[shp, shp, shp_b, shp_b],
    )(lam_re.reshape(rows, P), lam_im.reshape(rows, P), log_dt.reshape(rows, 1), bt(b_re), bt(b_im))
    eye = jnp.eye(G, dtype=F32)

    def in_map(bb):
        bb = jnp.transpose(bb.reshape(CH, 2, G, P), (1, 2, 0, 3))
        return jnp.einsum('dgcp,gh->dgchp', bb, eye).reshape(2, G * CH, G * P)

    w_in = jnp.concatenate([in_map(bb_re), in_map(bb_im)], axis=-1)
    out_map = lambda c: jnp.einsum('gcp,gh->gphc', c, eye).reshape(G * P, G * CH)
    w_out = jnp.concatenate([out_map(c_re), -out_map(c_im)], axis=0)
    return w_in.astype(BF16), a_re.reshape(2, 1, G * P), a_im.reshape(2, 1, G * P), w_out.astype(BF16)


def _s5_scan_kernel(b, u_ref, win_ref, are_ref, aim_ref, wout_ref, y_ref, h_ref, st_ref):
    d = pl.program_id(0)
    rows = u_ref.shape[0]
    tc = rows // b
    n = S5_LANES

    @pl.when(pl.program_id(1) == 0)
    def _():
        st_ref[...] = jnp.zeros_like(st_ref)

    h_ref[...] = _dot(u_ref[...], win_ref[0])
    chunk = 512
    for c in range(n // chunk):
        re = slice(c * chunk, (c + 1) * chunk)
        im = slice(n + c * chunk, n + (c + 1) * chunk)
        a_re = jnp.broadcast_to(are_ref[0, :, re], (b, chunk))
        a_im = jnp.broadcast_to(aim_ref[0, :, re], (b, chunk))

        def step(t, carry):
            h_re, h_im = carry
            t_eff = jnp.where(d == 0, t, tc - 1 - t)
            row = pl.ds(pl.multiple_of(t_eff * b, b), b)
            n_re = a_re * h_re - a_im * h_im + h_ref[row, re]
            n_im = a_re * h_im + a_im * h_re + h_ref[row, im]
            h_ref[row, re] = n_re
            h_ref[row, im] = n_im
            return n_re, n_im

        h_re, h_im = lax.fori_loop(0, tc, step, (st_ref[:, re], st_ref[:, im]))
        st_ref[:, re] = h_re
        st_ref[:, im] = h_im
    y_ref[0] = _dot(h_ref[...], wout_ref[...])


def _s5_scan(ps, weights, n_ctx_blocks, tc=TQ):
    t, b, w = ps.shape
    w_in, a_re, a_im, w_out = weights
    nb = t // tc
    n_lat_blocks = nb - n_ctx_blocks
    rows = tc * b

    def blk(d, j):
        bwd = jnp.where(j < n_ctx_blocks, n_ctx_blocks - 1 - j, n_ctx_blocks + nb - 1 - j)
        return jnp.where(d == 0, j, bwd)

    del n_lat_blocks
    return pl.pallas_call(
        functools.partial(_s5_scan_kernel, b),
        grid=(2, nb),
        in_specs=[pl.BlockSpec((rows, w), lambda d, j: (blk(d, j), 0)),
                  pl.BlockSpec((1, w, 2 * S5_LANES), lambda d, j: (d, 0, 0)),
                  pl.BlockSpec((1, 1, S5_LANES), lambda d, j: (d, 0, 0)),
                  pl.BlockSpec((1, 1, S5_LANES), lambda d, j: (d, 0, 0)),
                  pl.BlockSpec((2 * S5_LANES, w), lambda d, j: (0, 0))],
        out_specs=pl.BlockSpec((1, rows, w), lambda d, j: (d, blk(d, j), 0)),
        out_shape=jax.ShapeDtypeStruct((2, t * b, w), F32),
        scratch_shapes=[pltpu.VMEM((rows, 2 * S5_LANES), F32), pltpu.VMEM((b, 2 * S5_LANES), F32)],
        compiler_params=_params("arbitrary", "arbitrary"),
    )(ps.reshape(t * b, w), w_in, a_re, a_im, w_out)


def _attn_kernel(band, sink_ref, q_ref, *refs):
    if band:
        kp_ref, k0_ref, kn_ref, vp_ref, v0_ref, vn_ref, kc_ref, vc_ref, o_ref = refs
    else:
        kc_ref, vc_ref, o_ref = refs
    i = pl.program_id(1)
    nq = q_ref.shape[0]
    q = q_ref[...].astype(BF16)
    kc = kc_ref[...].astype(BF16)
    vc = vc_ref[...].astype(BF16)
    if band:
        kb = jnp.concatenate([kp_ref[...], k0_ref[...], kn_ref[...]], axis=0).astype(BF16)
        vb = jnp.concatenate([vp_ref[...], v0_ref[...], vn_ref[...]], axis=0).astype(BF16)
        nk = kb.shape[0]
        q_pos = i * nq + lax.broadcasted_iota(jnp.int32, (nq, nk), 0)
        k_pos = (i - 1) * nq + lax.broadcasted_iota(jnp.int32, (nq, nk), 1)
        n_lat = pl.num_programs(1) * nq
        mask = (jnp.abs(q_pos - k_pos) <= WINDOW) & (k_pos >= 0) & (k_pos < n_lat)
    qk = lambda a, bb: lax.dot_general(a, bb, (((1,), (1,)), ((), ())), preferred_element_type=F32)
    outs = []
    for h in range(ATT_HEADS):
        kv = h // ATT_GQ
        cols = slice(kv * HEAD_DIM, (kv + 1) * HEAD_DIM)
        qh = q[:, h * HEAD_DIM:(h + 1) * HEAD_DIM]
        sink = sink_ref[h]
        s_c = qk(qh, kc[:, cols])
        m = jnp.maximum(jnp.max(s_c, axis=1, keepdims=True), sink)
        if band:
            s_b = jnp.where(mask, qk(qh, kb[:, cols]), -jnp.inf)
            m = jnp.maximum(m, jnp.max(s_b, axis=1, keepdims=True))
        e_c = jnp.exp(s_c - m)
        den = jnp.sum(e_c, axis=1, keepdims=True) + jnp.exp(sink - m)
        o = jnp.dot(e_c.astype(BF16), vc[:, cols], preferred_element_type=F32)
        if band:
            e_b = jnp.exp(s_b - m)
            den = den + jnp.sum(e_b, axis=1, keepdims=True)
            o = o + jnp.dot(e_b.astype(BF16), vb[:, cols], preferred_element_type=F32)
        outs.append(o / den)
    o_ref[...] = jnp.concatenate(outs, axis=1)


def _attention(q, k, v, sink, n_ctx, band):
    t, b, _ = q.shape
    q2 = q.reshape(t, b * ATT_WIDTH)
    k2 = k.reshape(t, b * KV_WIDTH)
    v2 = v.reshape(t, b * KV_WIDTH)
    cb = n_ctx // ATT_BLOCK
    nq = (t - n_ctx) // ATT_BLOCK if band else cb
    off = cb if band else 0
    qspec = pl.BlockSpec((ATT_BLOCK, ATT_WIDTH), lambda bi, i: (off + i, bi))
    kvspec = lambda f: pl.BlockSpec((ATT_BLOCK, KV_WIDTH), lambda bi, i: (cb + f(i), bi))
    prev = lambda i: jnp.maximum(i - 1, 0)
    own = lambda i: i
    nxt = lambda i: jnp.minimum(i + 1, nq - 1)
    cspec = pl.BlockSpec((n_ctx, KV_WIDTH), lambda bi, i: (0, bi))
    in_specs = [pl.BlockSpec(memory_space=pltpu.SMEM), qspec]
    args = [sink, q2]
    if band:
        in_specs += [kvspec(prev), kvspec(own), kvspec(nxt)] * 2
        args += [k2, k2, k2, v2, v2, v2]
    in_specs += [cspec, cspec]
    args += [k2, v2]
    out = pl.pallas_call(
        functools.partial(_attn_kernel, band),
        grid=(b, nq),
        in_specs=in_specs,
        out_specs=pl.BlockSpec((ATT_BLOCK, ATT_WIDTH), lambda bi, i: (i, bi)),
        out_shape=jax.ShapeDtypeStruct((nq * ATT_BLOCK, b * ATT_WIDTH), F32),
        compiler_params=_params("arbitrary", "arbitrary"),
    )(*args)
    return out.reshape(nq * ATT_BLOCK, b, ATT_WIDTH)


def _gelu_tanh(x):
    return 0.5 * x * (1.0 + jnp.tanh(math.sqrt(2.0 / math.pi) * (x + 0.044715 * (x * x * x))))


def _outproj_kernel(x_ref, gt_ref, yr_ref, bonus_ref, gate_ref, u_ref, ys_ref, ya_ref,
                    hsum_ref, gnw_ref, gnb_ref, dskip_ref, gluw_ref, glub_ref, wo_ref, lng_ref, lnb_ref, o_ref):
    tq, b, d = x_ref.shape
    rows = tq * b
    flat = lambda ref: ref[...].reshape(rows, ref.shape[-1])
    hsum = hsum_ref[...]
    inv_n = 1.0 / HEAD_DIM
    y = flat(yr_ref)
    yc = y - _dot_hi(y, hsum) * inv_n
    var = _dot_hi(yc * yc, hsum) * inv_n
    out_r = (yc * lax.rsqrt(var + GN_EPS) * gnw_ref[...] + gnb_ref[...] + flat(bonus_ref)) * flat(gate_ref)
    z = _gelu_tanh(dskip_ref[...] * flat(u_ref) + ys_ref[0] + ys_ref[1])
    out_s = z * _sigmoid(_dot(z, gluw_ref[...]) + glub_ref[...])
    W = RWKV_WIDTH
    mix = _dot(out_r, wo_ref[0:W, :]) + _dot(out_s, wo_ref[W:W + S5_WIDTH, :]) + _dot(flat(ya_ref), wo_ref[W + S5_WIDTH:, :])
    x = x_ref[...]
    res = DEEPNORM_ALPHA * x + gt_ref[0] * mix.reshape(tq, b, d)
    o_ref[...] = _layer_norm(res, lng_ref[...], lnb_ref[...])


def _output_projection(xt, mod_tab, yr, bonus, gate, u, ys, ya, lp, n_ctx_blocks, first_block):
    t, b, d = xt.shape
    nb = t // TQ - first_block
    seg = lambda i: ((i + first_block) >= n_ctx_blocks).astype(jnp.int32)
    blk = lambda w: pl.BlockSpec((TQ, b, w), lambda i: (i + first_block, 0, 0))
    full = lambda a: pl.BlockSpec(a.shape, lambda i: (0,) * a.ndim)
    consts = (lp['hsum'], lp['gn_w'], lp['gn_b'], lp['s5_d'], lp['glu_w'], lp['glu_b'], lp['w_out'], lp['ln1_g'], lp['ln1_b'])
    return pl.pallas_call(
        _outproj_kernel,
        grid=(nb,),
        in_specs=[blk(d), pl.BlockSpec((1, b, d), lambda i: (seg(i), 0, 2)),
                  blk(RWKV_WIDTH), blk(RWKV_WIDTH), blk(RWKV_WIDTH), blk(S5_WIDTH),
                  pl.BlockSpec((2, TQ * b, S5_WIDTH), lambda i: (0, i + first_block, 0)),
                  pl.BlockSpec((TQ, b, ATT_WIDTH), lambda i: (i, 0, 0))] + [full(a) for a in consts],
        out_specs=pl.BlockSpec((TQ, b, d), lambda i: (i, 0, 0)),
        out_shape=jax.ShapeDtypeStruct((nb * TQ, b, d), F32),
        compiler_params=_params("arbitrary"),
    )(xt, mod_tab, yr, bonus, gate, u, ys, ya, *consts)


def _router_kernel(x_ref, sc_ref, sh_ref, w_ref, b_ref, h_ref, idx_ref, wt_ref):
    tq, b, d = x_ref.shape
    rows = tq * b
    h = (x_ref[...] * (1.0 + sc_ref[0]) + sh_ref[0]).reshape(rows, d)
    for c in range(d // LANES):
        h_ref[pl.ds(c, rows, stride=d // LANES), :] = h[:, c * LANES:(c + 1) * LANES]
    logits = _dot_hi(h, w_ref[...]) + b_ref[...]
    lane = lax.broadcasted_iota(jnp.int32, logits.shape, 1)
    neg = -jnp.inf
    first_max = lambda vals, m: jnp.min(jnp.where(vals == m, lane, LANES), axis=1, keepdims=True)
    gl = jnp.where(lane < N_GROUPS, logits, neg)
    g_max = jnp.max(gl, axis=1, keepdims=True)
    g_idx = first_max(gl, g_max)
    g_prob = 1.0 / jnp.sum(jnp.exp(gl - g_max), axis=1, keepdims=True)
    lo = N_GROUPS + EXPERTS_PER_GROUP * g_idx
    el = jnp.where((lane >= lo) & (lane < lo + EXPERTS_PER_GROUP), logits, neg)
    m1 = jnp.max(el, axis=1, keepdims=True)
    i1 = first_max(el, m1)
    el2 = jnp.where(lane == i1, neg, el)
    m2 = jnp.max(el2, axis=1, keepdims=True)
    i2 = first_max(el2, m2)
    e2 = jnp.exp(m2 - m1)
    p1 = 1.0 / (1.0 + e2)
    p2 = e2 / (1.0 + e2)
    idx_ref[...] = jnp.where(lane == 0, i1 - N_GROUPS, jnp.where(lane == 1, i2 - N_GROUPS, 0))
    wt_ref[...] = jnp.where(lane == 0, g_prob * p1, jnp.where(lane == 1, g_prob * p2, 0.0))


def _router(x1, mod_tab, w_r, b_r, n_ctx_blocks, first_block):
    t, b, d = x1.shape
    nb = t // TQ
    seg = lambda i: ((i + first_block) >= n_ctx_blocks).astype(jnp.int32)
    rows = TQ * b
    n = nb * rows
    return pl.pallas_call(
        _router_kernel,
        grid=(nb,),
        in_specs=[pl.BlockSpec((TQ, b, d), lambda i: (i, 0, 0)),
                  pl.BlockSpec((1, b, d), lambda i: (seg(i), 0, 4)),
                  pl.BlockSpec((1, b, d), lambda i: (seg(i), 0, 3)),
                  pl.BlockSpec(w_r.shape, lambda i: (0, 0)),
                  pl.BlockSpec(b_r.shape, lambda i: (0, 0))],
        out_specs=[pl.BlockSpec((rows * (d // LANES), LANES), lambda i: (i, 0)),
                   pl.BlockSpec((rows, LANES), lambda i: (i, 0)),
                   pl.BlockSpec((rows, LANES), lambda i: (i, 0))],
        out_shape=[jax.ShapeDtypeStruct((n * (d // LANES), LANES), F32),
                   jax.ShapeDtypeStruct((n, LANES), jnp.int32),
                   jax.ShapeDtypeStruct((n, LANES), F32)],
        compiler_params=_params("arbitrary"),
    )(x1, mod_tab, mod_tab, w_r, b_r)


def _dispatch_plan(experts, n_tok, slab_rows):
    a = n_tok * 2
    p = -(-a // MOE_BLOCK) * MOE_BLOCK + N_EXPERTS * MOE_BLOCK
    n_blk = p // MOE_BLOCK
    flat_e = experts.reshape(-1)
    ck = 256
    onehot = (flat_e.reshape(a // ck, ck, 1) == jnp.arange(N_EXPERTS, dtype=jnp.int32)).astype(F32)
    tri = (jnp.arange(ck)[:, None] > jnp.arange(ck)[None, :]).astype(F32)
    within = jnp.einsum('ij,cje->cie', tri, onehot)
    tot = jnp.sum(onehot, axis=1)
    before = jnp.cumsum(tot, axis=0) - tot
    rank = jnp.sum((within + before[:, None, :]) * onehot, axis=-1).astype(jnp.int32).reshape(a)
    counts = jnp.sum(tot, axis=0).astype(jnp.int32)
    padded = (counts + MOE_BLOCK - 1) // MOE_BLOCK * MOE_BLOCK
    pend = jnp.cumsum(padded)
    dest = (pend - padded)[flat_e] + rank
    slab = slab_rows
    inv = jnp.full((p,), -1, jnp.int32).at[dest].set(jnp.arange(a, dtype=jnp.int32))
    valid = inv >= 0
    src = jnp.where(valid, (inv // 2) * slab, 0)
    spare = (n_tok + jnp.arange(p, dtype=jnp.int32) % (2 * MOE_BLOCK)) * slab
    dst = jnp.where(valid, ((inv % 2) * (n_tok + 2 * MOE_BLOCK) + inv // 2) * slab, spare)
    n_used = pend[-1] // MOE_BLOCK
    blk = jnp.minimum(jnp.arange(n_blk, dtype=jnp.int32), n_used - 1) * MOE_BLOCK
    blk_exp = jnp.minimum(jnp.sum((pend[None, :] <= blk[:, None]).astype(jnp.int32), axis=1), N_EXPERTS - 1)
    return (blk_exp, n_used.reshape(1).astype(jnp.int32),
            src.reshape(n_blk, MOE_BLOCK), dst.reshape(n_blk, MOE_BLOCK), n_blk)


def _expert_kernel(n_tok, blk_exp_ref, n_used_ref, src_ref, dst_ref, h_hbm, wg_ref, wu_ref, wd_ref, y_hbm,
                   xbuf, ybuf, gsem, ssem):
    i = pl.program_id(0)
    n_used = n_used_ref[0]
    slot = i % 2
    slab = xbuf.shape[1] // MOE_BLOCK
    tok_rows = lambda r: pl.ds(pl.multiple_of(r * slab, slab), slab)

    def gather(blk, sl):
        for r in range(MOE_BLOCK):
            src = pl.multiple_of(src_ref[blk, r], slab)
            pltpu.make_async_copy(h_hbm.at[pl.ds(src, slab), :], xbuf.at[sl, tok_rows(r), :], gsem.at[sl]).start()

    def scatter(blk, sl):
        for r in range(MOE_BLOCK):
            dst = pl.multiple_of(dst_ref[blk, r], slab)
            pltpu.make_async_copy(ybuf.at[sl, tok_rows(r), :], y_hbm.at[pl.ds(dst, slab), :], ssem.at[sl]).start()

    wait_gather = lambda sl: pltpu.make_async_copy(xbuf.at[sl], xbuf.at[sl], gsem.at[sl]).wait()
    wait_scatter = lambda sl: pltpu.make_async_copy(ybuf.at[sl], ybuf.at[sl], ssem.at[sl]).wait()

    @pl.when(i == 0)
    def _():
        gather(0, 0)
        ybuf[...] = jnp.zeros_like(ybuf)
        for sl in range(2):
            spare = lambda j: pltpu.make_async_copy(
                ybuf.at[sl], y_hbm.at[pl.ds((j * (n_tok + 2 * MOE_BLOCK) + n_tok + sl * MOE_BLOCK) * slab, MOE_BLOCK * slab), :],
                ssem.at[sl])
            spare(1).start()
            spare(1).wait()
            spare(0).start()

    @pl.when(i < n_used)
    def _():
        wait_gather(slot)
        gather(jnp.minimum(i + 1, n_used - 1), 1 - slot)
        x = jnp.concatenate([xbuf[slot, pl.ds(c, MOE_BLOCK, stride=slab), :] for c in range(slab)], axis=1).astype(BF16)
        g = jnp.dot(x, wg_ref[0], preferred_element_type=F32)
        u = jnp.dot(x, wu_ref[0], preferred_element_type=F32)
        act = (g * _sigmoid(g)) * u
        y = jnp.dot(act.astype(BF16), wd_ref[0], preferred_element_type=F32)
        wait_scatter(slot)
        for c in range(slab):
            ybuf[slot, pl.ds(c, MOE_BLOCK, stride=slab), :] = y[:, c * LANES:(c + 1) * LANES]
        scatter(i, slot)

        @pl.when(i == n_used - 1)
        def _():
            wait_gather(1 - slot)
            wait_scatter(slot)
            wait_scatter(1 - slot)


def _routed_experts(h, n_tok, plan, wg, wu, wd):
    d = wg.shape[1]
    slab = d // LANES
    blk_exp, n_used, src, dst, n_blk = plan
    wspec = lambda s: pl.BlockSpec((1,) + s, lambda i, be, nu, sr, ds: (be[i], 0, 0))
    y = pl.pallas_call(
        functools.partial(_expert_kernel, n_tok),
        grid_spec=pltpu.PrefetchScalarGridSpec(
            num_scalar_prefetch=4,
            grid=(n_blk,),
            in_specs=[pl.BlockSpec(memory_space=pl.ANY), wspec((d, D_EXPERT)), wspec((d, D_EXPERT)), wspec((D_EXPERT, d))],
            out_specs=pl.BlockSpec(memory_space=pl.ANY),
            scratch_shapes=[pltpu.VMEM((2, MOE_BLOCK * slab, LANES), F32), pltpu.VMEM((2, MOE_BLOCK * slab, LANES), F32),
                            pltpu.SemaphoreType.DMA((2,)), pltpu.SemaphoreType.DMA((2,))]),
        out_shape=jax.ShapeDtypeStruct((2 * (n_tok + 2 * MOE_BLOCK) * slab, LANES), F32),
        compiler_params=_params("arbitrary"),
    )(blk_exp, n_used, src, dst, h, wg, wu, wd)
    return y.reshape(2, (n_tok + 2 * MOE_BLOCK) * slab, LANES)


def _combine_kernel(x_ref, gt_ref, y_ref, wt_ref, lng_ref, lnb_ref, o_ref):
    tq, b, d = x_ref.shape
    rows = tq * b
    slab = d // LANES
    wt = wt_ref[...]
    w0, w1 = wt[:, 0:1], wt[:, 1:2]
    y = jnp.concatenate([y_ref[0, pl.ds(c, rows, stride=slab), :] * w0 + y_ref[1, pl.ds(c, rows, stride=slab), :] * w1
                         for c in range(slab)], axis=1)
    res = DEEPNORM_ALPHA * x_ref[...] + gt_ref[0] * y.reshape(tq, b, d)
    o_ref[...] = _layer_norm(res, lng_ref[...], lnb_ref[...])


def _moe_combine(x1, mod_tab, y, wts, ln_g, ln_b, n_ctx_blocks, first_block):
    t, b, d = x1.shape
    nb = t // TQ
    rows = TQ * b
    seg = lambda i: ((i + first_block) >= n_ctx_blocks).astype(jnp.int32)
    blk = pl.BlockSpec((TQ, b, d), lambda i: (i, 0, 0))
    return pl.pallas_call(
        _combine_kernel,
        grid=(nb,),
        in_specs=[blk, pl.BlockSpec((1, b, d), lambda i: (seg(i), 0, 5)),
                  pl.BlockSpec((2, rows * (d // LANES), LANES), lambda i: (0, i, 0)),
                  pl.BlockSpec((rows, LANES), lambda i: (i, 0)),
                  pl.BlockSpec(ln_g.shape, lambda i: (0, 0)), pl.BlockSpec(ln_b.shape, lambda i: (0, 0))],
        out_specs=blk,
        out_shape=jax.ShapeDtypeStruct((t, b, d), F32),
        compiler_params=_params("arbitrary"),
    )(x1, mod_tab, y, wts, ln_g, ln_b)


def _layer(xt, c_all, lp, rope, n_ctx, last):
    t, b, d = xt.shape
    ncb = n_ctx // TQ
    first = ncb if last else 0
    mod = _modulation(c_all, lp['w_mod'], lp['b_mod'])
    mod_tab = jnp.stack([jnp.broadcast_to(mod[b:b + 1], (b, mod.shape[1])), mod[:b]], axis=0)

    pr, ps, q, k, v = _input_projection(xt, mod_tab, lp['w_in'], rope, ncb)

    r, vv, kk, d0, d1, k0, k1, b0, b1, bonus, gate = _rwkv_features(pr, lp, ncb)
    sl = _to_scan_layout
    yf, yb = _rwkv_scan((sl(r, r), sl(vv, vv), sl(kk, kk), sl(d0, d1), sl(k0, k1), sl(b0, b1)), n_ctx)
    yr = _from_scan_layout(yf, yb, b)

    ys = _s5_scan(ps, lp['s5'], ncb)

    ya = _attention(q, k, v, lp['sink'], n_ctx, True)
    if not last:
        ya = jnp.concatenate([_attention(q, k, v, lp['sink'], n_ctx, False), ya], axis=0)

    x1 = _output_projection(xt, mod_tab, yr, bonus, gate, ps, ys, ya, lp, ncb, first)
    h, idx, wts = _router(x1, mod_tab, lp['w_router'], lp['b_router'], ncb, first)
    n_tok = idx.shape[0]
    plan = _dispatch_plan(idx[:, :2], n_tok, d // LANES)
    y = _routed_experts(h, n_tok, plan, lp['w_gate'], lp['w_up'], lp['w_down'])
    return _moe_combine(x1, mod_tab, y, wts, lp['ln2_g'], lp['ln2_b'], ncb, first)


def kernel(x, c, ctx, c_ctx, w_mod, b_mod, w_in, rwkv_conv, rwkv_w0, rwkv_w2, rwkv_a0, rwkv_a2, rwkv_g2, rwkv_k_k, rwkv_k_a, rwkv_r_k, rwkv_gn_w, rwkv_gn_b, s5_lam_re, s5_lam_im, s5_log_dt, s5_b_re, s5_b_im, s5_c_re, s5_c_im, s5_d, s5_glu_w, s5_glu_b, attn_sink, w_out, ln1_g, ln1_b, ln2_g, ln2_b, router_group_w, router_group_b, router_expert_w, router_expert_b, expert_w_gate, expert_w_up, expert_w_down):
    bsz, seq, d = x.shape
    n_ctx = ctx.shape[1]
    depth = w_mod.shape[0]
    W = RWKV_WIDTH
    xt = jnp.transpose(jnp.concatenate([ctx, x], axis=1), (1, 0, 2))
    c_all = jnp.zeros((bsz + 8, d), F32).at[:bsz].set(c).at[bsz].set(c_ctx)
    rope = _rope_tables(n_ctx, seq)
    hsum = jnp.kron(jnp.eye(RWKV_HEADS, dtype=F32), jnp.ones((HEAD_DIM, HEAD_DIM), F32))
    row = lambda a: a.reshape(1, -1)
    zeros_lo = jnp.zeros((RWKV_LORA // 2, 2 * W), F32)
    for i in range(depth):
        n_r = N_GROUPS + N_EXPERTS
        lp = {
            'w_mod': w_mod[i], 'b_mod': b_mod[i], 'w_in': w_in[i].astype(BF16),
            'conv': rwkv_conv[i], 'w0': rwkv_w0[i], 'a0': rwkv_a0[i],
            'w2': jnp.concatenate([jnp.concatenate([rwkv_w2[i, 0], rwkv_w2[i, 1]], axis=1), zeros_lo], axis=0),
            'a2': jnp.concatenate([zeros_lo, jnp.concatenate([rwkv_a2[i, 0], rwkv_a2[i, 1]], axis=1)], axis=0),
            'g2': rwkv_g2[i].astype(BF16), 'k_k': row(rwkv_k_k[i]), 'k_a': row(rwkv_k_a[i]), 'r_k': row(rwkv_r_k[i]),
            'hsum': hsum, 'gn_w': row(rwkv_gn_w[i]), 'gn_b': row(rwkv_gn_b[i]),
            's5': _s5_weights(s5_lam_re[i], s5_lam_im[i], s5_log_dt[i], s5_b_re[i], s5_b_im[i], s5_c_re[i], s5_c_im[i]),
            's5_d': row(s5_d[i]), 'glu_w': s5_glu_w[i].astype(BF16), 'glu_b': row(s5_glu_b[i]),
            'sink': attn_sink[i], 'w_out': w_out[i].astype(BF16),
            'ln1_g': row(ln1_g[i]), 'ln1_b': row(ln1_b[i]), 'ln2_g': row(ln2_g[i]), 'ln2_b': row(ln2_b[i]),
            'w_router': jnp.zeros((d, LANES), F32).at[:, :N_GROUPS].set(router_group_w[i]).at[:, N_GROUPS:n_r].set(router_expert_w[i]),
            'b_router': jnp.zeros((1, LANES), F32).at[0, :N_GROUPS].set(router_group_b[i]).at[0, N_GROUPS:n_r].set(router_expert_b[i]),
            'w_gate': expert_w_gate[i].astype(BF16), 'w_up': expert_w_up[i].astype(BF16), 'w_down': expert_w_down[i].astype(BF16),
        }
        xt = _layer(xt, c_all, lp, rope, n_ctx, i == depth - 1)
    return jnp.transpose(xt, (1, 0, 2))
```

```python
import functools
import math

import jax
import jax.numpy as jnp
from jax import lax
from jax.experimental import pallas as pl
from jax.experimental.pallas import tpu as pltpu

F32 = jnp.float32
BF16 = jnp.bfloat16

HEAD_DIM = 64
RWKV_WIDTH = 256
RWKV_HEADS = 4
RWKV_LORA = 128
S5_WIDTH = 256
S5_GROUPS = 16
S5_CH = 16
S5_STATE = 64
S5_LANES = S5_GROUPS * S5_STATE
ATT_WIDTH = 512
ATT_HEADS = 8
ATT_KV_HEADS = 2
ATT_GQ = 4
KV_WIDTH = ATT_KV_HEADS * HEAD_DIM
WINDOW = 128
ATT_BLOCK = 128
GRID_W = 64
ROPE_BASE = 10000.0
N_GROUPS = 4
EXPERTS_PER_GROUP = 8
N_EXPERTS = 32
D_EXPERT = 512
MOE_BLOCK = 128
N_MOD = 6
DEPTH = 2
DEEPNORM_ALPHA = (2.0 * DEPTH) ** 0.25
LN_EPS = 1e-5
GN_EPS = 64e-5
LANES = 128
TQ = 32
SCAN_UNROLL = 32
SCAN_CHUNK = 16
VMEM_LIMIT = 48 * 1024 * 1024


def _params(*sem):
    return pltpu.CompilerParams(dimension_semantics=sem, vmem_limit_bytes=VMEM_LIMIT)


def _dot(a, b):
    return jnp.dot(a.astype(BF16), b.astype(BF16), preferred_element_type=F32)


def _dot_hi(a, b):
    return jnp.dot(a, b, precision=lax.Precision.HIGHEST, preferred_element_type=F32)


def _sigmoid(x):
    return 1.0 / (1.0 + jnp.exp(-x))


def _layer_norm(x, g, b):
    mu = jnp.mean(x, axis=-1, keepdims=True)
    xc = x - mu
    var = jnp.mean(xc * xc, axis=-1, keepdims=True)
    return xc * lax.rsqrt(var + LN_EPS) * g + b


def _mod_kernel(c_ref, w_ref, b_ref, o_ref):
    c = c_ref[...]
    o_ref[...] = _dot_hi(c * _sigmoid(c), w_ref[...]) + b_ref[...]


def _modulation(c_all, w_mod, b_mod):
    rows, d = c_all.shape
    n = w_mod.shape[1]
    return pl.pallas_call(
        _mod_kernel,
        grid=(n // d,),
        in_specs=[pl.BlockSpec((rows, d), lambda j: (0, 0)),
                  pl.BlockSpec((d, d), lambda j: (0, j)),
                  pl.BlockSpec((1, d), lambda j: (0, j))],
        out_specs=pl.BlockSpec((rows, d), lambda j: (0, j)),
        out_shape=jax.ShapeDtypeStruct((rows, n), F32),
        compiler_params=_params("arbitrary"),
    )(c_all, w_mod, b_mod.reshape(1, n))


def _swap_halves(t):
    n = t.shape[-1]
    lane = lax.broadcasted_iota(jnp.int32, t.shape, t.ndim - 1)
    first = (lane % HEAD_DIM) < (HEAD_DIM // 2)
    return jnp.where(first, pltpu.roll(t, n - HEAD_DIM // 2, t.ndim - 1), pltpu.roll(t, HEAD_DIM // 2, t.ndim - 1))


def _inproj_kernel(x_ref, sc_ref, sh_ref, w_ref, cq_ref, sq_ref, ck_ref, sk_ref,
                   pr_ref, ps_ref, q_ref, k_ref, v_ref):
    tq, b, d = x_ref.shape
    h = x_ref[...] * (1.0 + sc_ref[0]) + sh_ref[0]
    p = _dot(h.reshape(tq * b, d), w_ref[...])
    o_r = RWKV_WIDTH * 3 + RWKV_LORA + 128
    o_s = o_r + S5_WIDTH
    o_q = o_s + ATT_WIDTH
    o_k = o_q + KV_WIDTH
    pr_ref[...] = p[:, :o_r].reshape(tq, b, o_r)
    ps_ref[...] = p[:, o_r:o_s].reshape(tq, b, S5_WIDTH)
    q = p[:, o_s:o_q]
    k = p[:, o_q:o_k]
    bc = lambda r, w: jnp.broadcast_to(r[...], (tq, b, w)).reshape(tq * b, w)
    q = q * bc(cq_ref, ATT_WIDTH) + _swap_halves(q) * bc(sq_ref, ATT_WIDTH)
    k = k * bc(ck_ref, KV_WIDTH) + _swap_halves(k) * bc(sk_ref, KV_WIDTH)
    q_ref[...] = q.reshape(tq, b, ATT_WIDTH)
    k_ref[...] = k.reshape(tq, b, KV_WIDTH)
    v_ref[...] = p[:, o_k:].reshape(tq, b, KV_WIDTH)


def _input_projection(xt, mod_tab, w_in_bf, rope, n_ctx_blocks):
    t, b, d = xt.shape
    d_in = w_in_bf.shape[1]
    seg = lambda i: (i >= n_ctx_blocks).astype(jnp.int32)
    cq, sq, ck, sk = rope
    widths = (RWKV_WIDTH * 3 + RWKV_LORA + 128, S5_WIDTH, ATT_WIDTH, KV_WIDTH, KV_WIDTH)
    tab = lambda w: pl.BlockSpec((TQ, 1, w), lambda i: (i, 0, 0))
    return pl.pallas_call(
        _inproj_kernel,
        grid=(t // TQ,),
        in_specs=[pl.BlockSpec((TQ, b, d), lambda i: (i, 0, 0)),
                  pl.BlockSpec((1, b, d), lambda i: (seg(i), 0, 1)),
                  pl.BlockSpec((1, b, d), lambda i: (seg(i), 0, 0)),
                  pl.BlockSpec((d, d_in), lambda i: (0, 0)),
                  tab(ATT_WIDTH), tab(ATT_WIDTH), tab(KV_WIDTH), tab(KV_WIDTH)],
        out_specs=[pl.BlockSpec((TQ, b, w), lambda i: (i, 0, 0)) for w in widths],
        out_shape=[jax.ShapeDtypeStruct((t, b, w), F32) for w in widths],
        compiler_params=_params("arbitrary"),
    )(xt, mod_tab, mod_tab, w_in_bf, cq, sq, ck, sk)


def _rope_tables(n_ctx, n_lat):
    rows = n_lat // GRID_W
    row_id, col_id = jnp.meshgrid(jnp.arange(rows, dtype=F32), jnp.arange(GRID_W, dtype=F32), indexing='ij')
    n_freq = HEAD_DIM // 4
    inv_freq = ROPE_BASE ** (-jnp.arange(n_freq, dtype=F32) / n_freq)
    ang = jnp.concatenate([row_id.reshape(-1, 1) * inv_freq, col_id.reshape(-1, 1) * inv_freq], axis=-1)
    cos, sin = jnp.cos(ang), jnp.sin(ang)
    cos_h = jnp.concatenate([jnp.ones((n_ctx, HEAD_DIM), F32), jnp.concatenate([cos, cos], axis=-1)], axis=0)
    sin_h = jnp.concatenate([jnp.zeros((n_ctx, HEAD_DIM), F32), jnp.concatenate([-sin, sin], axis=-1)], axis=0)
    scale = HEAD_DIM ** -0.5
    expand = lambda tb, n, s: (jnp.tile(tb, (1, n)) * s)[:, None, :]
    return (expand(cos_h, ATT_HEADS, scale), expand(sin_h, ATT_HEADS, scale),
            expand(cos_h, ATT_KV_HEADS, 1.0), expand(sin_h, ATT_KV_HEADS, 1.0))


def _rwkv_feat_kernel(n_ctx_blocks, cur_ref, prev_ref, next_ref, conv_ref, w0_ref, a0_ref, w2_ref, a2_ref, g2_ref,
                      kk_w_ref, ka_ref, rk_ref, hsum_ref,
                      r_ref, v_ref, kk_ref, d0_ref, d1_ref, k0_ref, k1_ref, b0_ref, b1_ref, bonus_ref, gate_ref):
    i = pl.program_id(0)
    tq, b, w = cur_ref.shape
    rows = tq * b
    cur = cur_ref[...]
    keep_prev = jnp.where((i == 0) | (i == n_ctx_blocks), 0.0, 1.0)
    keep_next = jnp.where((i == n_ctx_blocks - 1) | (i == pl.num_programs(0) - 1), 0.0, 1.0)
    before = jnp.concatenate([prev_ref[...] * keep_prev, cur[:-1]], axis=0)
    after = jnp.concatenate([cur[1:], next_ref[...] * keep_next], axis=0)
    pc = (before * conv_ref[0:1, :] + cur * conv_ref[1:2, :] + after * conv_ref[2:3, :]).reshape(rows, w)
    W = RWKV_WIDTH
    r, k, v = pc[:, :W], pc[:, W:2 * W], pc[:, 2 * W:3 * W]
    lo = pc[:, 3 * W:3 * W + RWKV_LORA]
    g_lo = pc[:, 3 * W + RWKV_LORA:]
    hsum = hsum_ref[...]
    kk = k * kk_w_ref[...]
    kk = kk * lax.rsqrt(_dot_hi(kk * kk, hsum) + 1e-12)
    wl = _dot_hi(jnp.tanh(lo), w2_ref[...])
    al = _dot_hi(lo, a2_ref[...])
    k_sum = None
    for d, (d_ref, kd_ref, b_ref) in enumerate(((d0_ref, k0_ref, b0_ref), (d1_ref, k1_ref, b1_ref))):
        z = -(w0_ref[d:d + 1, :] + wl[:, d * W:(d + 1) * W])
        softplus = jnp.maximum(z, 0.0) + jnp.log(1.0 + jnp.exp(-jnp.abs(z)))
        decay = jnp.exp(-jnp.exp(-softplus - 0.5))
        a = _sigmoid(a0_ref[d:d + 1, :] + al[:, d * W:(d + 1) * W])
        k_d = k * (1.0 + (a - 1.0) * ka_ref[...])
        d_ref[...] = decay.reshape(tq, b, W)
        kd_ref[...] = k_d.reshape(tq, b, W)
        b_ref[...] = (kk * a).reshape(tq, b, W)
        k_sum = k_d if k_sum is None else k_sum + k_d
    r_ref[...] = r.reshape(tq, b, W)
    v_ref[...] = v.reshape(tq, b, W)
    kk_ref[...] = kk.reshape(tq, b, W)
    bonus_ref[...] = (_dot_hi(r * k_sum * rk_ref[...], hsum) * v).reshape(tq, b, W)
    gate_ref[...] = _dot(_sigmoid(g_lo), g2_ref[...]).reshape(tq, b, W)


def _rwkv_features(pr, lp, n_ctx_blocks):
    t, b, w = pr.shape
    nb = t // TQ
    W = RWKV_WIDTH
    full = lambda a: pl.BlockSpec(a.shape, lambda i: (0,) * a.ndim)
    consts = (lp['conv'], lp['w0'], lp['a0'], lp['w2'], lp['a2'], lp['g2'], lp['k_k'], lp['k_a'], lp['r_k'], lp['hsum'])
    return pl.pallas_call(
        functools.partial(_rwkv_feat_kernel, n_ctx_blocks),
        grid=(nb,),
        in_specs=[pl.BlockSpec((TQ, b, w), lambda i: (i, 0, 0)),
                  pl.BlockSpec((1, b, w), lambda i: (jnp.maximum(i * TQ - 1, 0), 0, 0)),
                  pl.BlockSpec((1, b, w), lambda i: (jnp.minimum((i + 1) * TQ, t - 1), 0, 0))]
                 + [full(a) for a in consts],
        out_specs=[pl.BlockSpec((TQ, b, W), lambda i: (i, 0, 0))] * 11,
        out_shape=[jax.ShapeDtypeStruct((t, b, W), F32)] * 11,
        compiler_params=_params("arbitrary"),
    )(pr, pr, pr, *consts)


def _rwkv_scan_kernel(rf_ref, vf_ref, kkf_ref, wf_ref, kf_ref, bf_ref, rb_ref, vb_ref, kkb_ref, wb_ref, kb_ref, bb_ref,
                      yf_ref, yb_ref, s_ref, x_ref):
    @pl.when(pl.program_id(0) == 0)
    def _():
        s_ref[...] = jnp.zeros_like(s_ref)

    n = HEAD_DIM
    tc, _, lanes = rf_ref.shape
    R, V, KK, W, K, B, WR = range(7)
    pairs = ((rf_ref, rb_ref), (vf_ref, vb_ref), (kkf_ref, kkb_ref), (wf_ref, wb_ref), (kf_ref, kb_ref), (bf_ref, bb_ref))
    is_fwd = lax.broadcasted_iota(jnp.int32, (n, lanes), 1) < lanes // 2

    def step(t, d_prev):
        tb = tc - 1 - t
        for q, (f_ref, b_ref) in enumerate(pairs):
            x_ref[q] = jnp.where(is_fwd, f_ref[t], b_ref[tb])
        r = x_ref[R]
        b_row, k_row = x_ref[B], x_ref[K]
        br = jnp.sum(b_row * r, axis=0, keepdims=True)
        kr = jnp.sum(k_row * r, axis=0, keepdims=True)
        d_new = d_prev * x_ref[W]
        inv_d = 1.0 / d_new
        x_ref[KK] = x_ref[KK] * d_prev
        x_ref[WR] = d_new * r
        x_ref[B] = b_row * inv_d
        x_ref[K] = k_row * inv_d

        def reduce_pass(jg, acc):
            s_kk, z = acc
            for jj in range(SCAN_UNROLL):
                j = jg * SCAN_UNROLL + jj
                s = s_ref[j]
                s_kk = s_kk + s * x_ref[KK, pl.ds(j, 1), :]
                z = z + s * x_ref[WR, pl.ds(j, 1), :]
            return s_kk, z

        zero = jnp.zeros((n, lanes), F32)
        s_kk, z = lax.fori_loop(0, n // SCAN_UNROLL, reduce_pass, (zero, zero))
        v = x_ref[V]

        def update_pass(jg, c):
            for jj in range(SCAN_UNROLL):
                j = jg * SCAN_UNROLL + jj
                s_ref[j] = s_ref[j] - s_kk * x_ref[B, pl.ds(j, 1), :] + v * x_ref[K, pl.ds(j, 1), :]
            return c

        lax.fori_loop(0, n // SCAN_UNROLL, update_pass, 0)
        y = z - s_kk * br + v * kr
        yf_ref[t] = y
        yb_ref[tb] = y
        return d_new

    x_ref[W] = lax.fori_loop(0, tc, step, jnp.ones((n, lanes), F32))

    def rescale_pass(jg, c):
        for jj in range(SCAN_UNROLL):
            j = jg * SCAN_UNROLL + jj
            s_ref[j] = s_ref[j] * x_ref[W, pl.ds(j, 1), :]
        return c

    lax.fori_loop(0, n // SCAN_UNROLL, rescale_pass, 0)


def _rwkv_scan(streams, n_ctx, tc=SCAN_CHUNK):
    t, _, lanes = streams[0].shape
    nb, ncb = t // tc, n_ctx // tc
    mirror = lambda i: jnp.where(i < ncb, ncb - 1 - i, ncb + nb - 1 - i)
    fspec = pl.BlockSpec((tc, HEAD_DIM, lanes), lambda i: (i, 0, 0))
    bspec = pl.BlockSpec((tc, HEAD_DIM, lanes), lambda i: (mirror(i), 0, 0))
    out = jax.ShapeDtypeStruct((t, HEAD_DIM, lanes), F32)
    return pl.pallas_call(
        _rwkv_scan_kernel,
        grid=(nb,),
        in_specs=[fspec] * 6 + [bspec] * 6,
        out_specs=[fspec, bspec],
        out_shape=[out, out],
        scratch_shapes=[pltpu.VMEM((HEAD_DIM, HEAD_DIM, lanes), F32), pltpu.VMEM((7, HEAD_DIM, lanes), F32)],
        compiler_params=_params("arbitrary"),
    )(*streams, *streams)


def _to_scan_layout(fwd, bwd):
    t, b, _ = fwd.shape
    tr = lambda a: jnp.transpose(a.reshape(t, b, RWKV_HEADS, HEAD_DIM), (0, 3, 1, 2)).reshape(t, HEAD_DIM, b * RWKV_HEADS)
    return jnp.concatenate([tr(fwd), tr(bwd)], axis=-1)


def _from_scan_layout(yf, yb, b):
    t, _, lanes = yf.shape
    y = yf[:, :, :lanes // 2] + yb[:, :, lanes // 2:]
    return jnp.transpose(y.reshape(t, HEAD_DIM, b, RWKV_HEADS), (0, 2, 3, 1)).reshape(t, b, RWKV_WIDTH)


def _s5_disc_kernel(lre_ref, lim_ref, ldt_ref, bre_ref, bim_ref, are_ref, aim_ref, bbre_ref, bbim_ref):
    lre, lim = lre_ref[...], lim_ref[...]
    dt = jnp.exp(ldt_ref[...])
    mag = jnp.exp(lre * dt)
    ab_re = mag * jnp.cos(lim * dt)
    ab_im = mag * jnp.sin(lim * dt)
    nr, ni = ab_re - 1.0, ab_im
    den = lre * lre + lim * lim
    coef_re = (nr * lre + ni * lim) / den
    coef_im = (ni * lre - nr * lim) / den
    are_ref[...] = ab_re
    aim_ref[...] = ab_im
    for c in range(S5_CH):
        bre, bim = bre_ref[c], bim_ref[c]
        bbre_ref[c] = coef_re * bre - coef_im * bim
        bbim_ref[c] = coef_re * bim + coef_im * bre


def _s5_weights(lam_re, lam_im, log_dt, b_re, b_im, c_re, c_im):
    G, P, CH = S5_GROUPS, S5_STATE, S5_CH
    rows = 2 * G
    bt = lambda a: jnp.broadcast_to(jnp.transpose(a, (2, 0, 1))[:, None], (CH, 2, G, P)).reshape(CH, rows, P)
    shp = jax.ShapeDtypeStruct((rows, P), F32)
    shp_b = jax.ShapeDtypeStruct((CH, rows, P), F32)
    a_re, a_im, bb_re, bb_im = pl.pallas_call(
        _s5_disc_kernel, out_shape=[shp, shp, shp_b, shp_b],
    )(lam_re.reshape(rows, P), lam_im.reshape(rows, P), log_dt.reshape(rows, 1), bt(b_re), bt(b_im))
    eye = jnp.eye(G, dtype=F32)

    def in_map(bb):
        bb = jnp.transpose(bb.reshape(CH, 2, G, P), (1, 2, 0, 3))
        return jnp.einsum('dgcp,gh->dgchp', bb, eye).reshape(2, G * CH, G * P)

    w_in = jnp.concatenate([in_map(bb_re), in_map(bb_im)], axis=-1)
    out_map = lambda c: jnp.einsum('gcp,gh->gphc', c, eye).reshape(G * P, G * CH)
    w_out = jnp.concatenate([out_map(c_re), -out_map(c_im)], axis=0)
    return w_in.astype(BF16), a_re.reshape(2, 1, G * P), a_im.reshape(2, 1, G * P), w_out.astype(BF16)


def _s5_scan_kernel(b, u_ref, win_ref, are_ref, aim_ref, wout_ref, y_ref, h_ref, st_ref):
    d = pl.program_id(0)
    rows = u_ref.shape[0]
    tc = rows // b
    n = S5_LANES

    @pl.when(pl.program_id(1) == 0)
    def _():
        st_ref[...] = jnp.zeros_like(st_ref)

    h_ref[...] = _dot(u_ref[...], win_ref[0])
    chunk = 512
    for c in range(n // chunk):
        re = slice(c * chunk, (c + 1) * chunk)
        im = slice(n + c * chunk, n + (c + 1) * chunk)
        a_re = jnp.broadcast_to(are_ref[0, :, re], (b, chunk))
        a_im = jnp.broadcast_to(aim_ref[0, :, re], (b, chunk))

        def step(t, carry):
            h_re, h_im = carry
            t_eff = jnp.where(d == 0, t, tc - 1 - t)
            row = pl.ds(pl.multiple_of(t_eff * b, b), b)
            n_re = a_re * h_re - a_im * h_im + h_ref[row, re]
            n_im = a_re * h_im + a_im * h_re + h_ref[row, im]
            h_ref[row, re] = n_re
            h_ref[row, im] = n_im
            return n_re, n_im

        h_re, h_im = lax.fori_loop(0, tc, step, (st_ref[:, re], st_ref[:, im]))
        st_ref[:, re] = h_re
        st_ref[:, im] = h_im
    y_ref[0] = _dot(h_ref[...], wout_ref[...])


def _s5_scan(ps, weights, n_ctx_blocks, tc=TQ):
    t, b, w = ps.shape
    w_in, a_re, a_im, w_out = weights
    nb = t // tc
    n_lat_blocks = nb - n_ctx_blocks
    rows = tc * b

    def blk(d, j):
        bwd = jnp.where(j < n_ctx_blocks, n_ctx_blocks - 1 - j, n_ctx_blocks + nb - 1 - j)
        return jnp.where(d == 0, j, bwd)

    del n_lat_blocks
    return pl.pallas_call(
        functools.partial(_s5_scan_kernel, b),
        grid=(2, nb),
        in_specs=[pl.BlockSpec((rows, w), lambda d, j: (blk(d, j), 0)),
                  pl.BlockSpec((1, w, 2 * S5_LANES), lambda d, j: (d, 0, 0)),
                  pl.BlockSpec((1, 1, S5_LANES), lambda d, j: (d, 0, 0)),
                  pl.BlockSpec((1, 1, S5_LANES), lambda d, j: (d, 0, 0)),
                  pl.BlockSpec((2 * S5_LANES, w), lambda d, j: (0, 0))],
        out_specs=pl.BlockSpec((1, rows, w), lambda d, j: (d, blk(d, j), 0)),
        out_shape=jax.ShapeDtypeStruct((2, t * b, w), F32),
        scratch_shapes=[pltpu.VMEM((rows, 2 * S5_LANES), F32), pltpu.VMEM((b, 2 * S5_LANES), F32)],
        compiler_params=_params("arbitrary", "arbitrary"),
    )(ps.reshape(t * b, w), w_in, a_re, a_im, w_out)


def _attn_kernel(band, sink_ref, q_ref, *refs):
    if band:
        kp_ref, k0_ref, kn_ref, vp_ref, v0_ref, vn_ref, kc_ref, vc_ref, o_ref = refs
    else:
        kc_ref, vc_ref, o_ref = refs
    i = pl.program_id(1)
    nq = q_ref.shape[0]
    q = q_ref[...].astype(BF16)
    kc = kc_ref[...].astype(BF16)
    vc = vc_ref[...].astype(BF16)
    if band:
        kb = jnp.concatenate([kp_ref[...], k0_ref[...], kn_ref[...]], axis=0).astype(BF16)
        vb = jnp.concatenate([vp_ref[...], v0_ref[...], vn_ref[...]], axis=0).astype(BF16)
        nk = kb.shape[0]
        q_pos = i * nq + lax.broadcasted_iota(jnp.int32, (nq, nk), 0)
        k_pos = (i - 1) * nq + lax.broadcasted_iota(jnp.int32, (nq, nk), 1)
        n_lat = pl.num_programs(1) * nq
        mask = (jnp.abs(q_pos - k_pos) <= WINDOW) & (k_pos >= 0) & (k_pos < n_lat)
    qk = lambda a, bb: lax.dot_general(a, bb, (((1,), (1,)), ((), ())), preferred_element_type=F32)
    outs = []
    for h in range(ATT_HEADS):
        kv = h // ATT_GQ
        cols = slice(kv * HEAD_DIM, (kv + 1) * HEAD_DIM)
        qh = q[:, h * HEAD_DIM:(h + 1) * HEAD_DIM]
        sink = sink_ref[h]
        s_c = qk(qh, kc[:, cols])
        m = jnp.maximum(jnp.max(s_c, axis=1, keepdims=True), sink)
        if band:
            s_b = jnp.where(mask, qk(qh, kb[:, cols]), -jnp.inf)
            m = jnp.maximum(m, jnp.max(s_b, axis=1, keepdims=True))
        e_c = jnp.exp(s_c - m)
        den = jnp.sum(e_c, axis=1, keepdims=True) + jnp.exp(sink - m)
        o = jnp.dot(e_c.astype(BF16), vc[:, cols], preferred_element_type=F32)
        if band:
            e_b = jnp.exp(s_b - m)
            den = den + jnp.sum(e_b, axis=1, keepdims=True)
            o = o + jnp.dot(e_b.astype(BF16), vb[:, cols], preferred_element_type=F32)
        outs.append(o / den)
    o_ref[...] = jnp.concatenate(outs, axis=1)


def _attention(q, k, v, sink, n_ctx, band):
    t, b, _ = q.shape
    q2 = q.reshape(t, b * ATT_WIDTH)
    k2 = k.reshape(t, b * KV_WIDTH)
    v2 = v.reshape(t, b * KV_WIDTH)
    cb = n_ctx // ATT_BLOCK
    nq = (t - n_ctx) // ATT_BLOCK if band else cb
    off = cb if band else 0
    qspec = pl.BlockSpec((ATT_BLOCK, ATT_WIDTH), lambda bi, i: (off + i, bi))
    kvspec = lambda f: pl.BlockSpec((ATT_BLOCK, KV_WIDTH), lambda bi, i: (cb + f(i), bi))
    prev = lambda i: jnp.maximum(i - 1, 0)
    own = lambda i: i
    nxt = lambda i: jnp.minimum(i + 1, nq - 1)
    cspec = pl.BlockSpec((n_ctx, KV_WIDTH), lambda bi, i: (0, bi))
    in_specs = [pl.BlockSpec(memory_space=pltpu.SMEM), qspec]
    args = [sink, q2]
    if band:
        in_specs += [kvspec(prev), kvspec(own), kvspec(nxt)] * 2
        args += [k2, k2, k2, v2, v2, v2]
    in_specs += [cspec, cspec]
    args += [k2, v2]
    out = pl.pallas_call(
        functools.partial(_attn_kernel, band),
        grid=(b, nq),
        in_specs=in_specs,
        out_specs=pl.BlockSpec((ATT_BLOCK, ATT_WIDTH), lambda bi, i: (i, bi)),
        out_shape=jax.ShapeDtypeStruct((nq * ATT_BLOCK, b * ATT_WIDTH), F32),
        compiler_params=_params("arbitrary", "arbitrary"),
    )(*args)
    return out.reshape(nq * ATT_BLOCK, b, ATT_WIDTH)


def _gelu_tanh(x):
    return 0.5 * x * (1.0 + jnp.tanh(math.sqrt(2.0 / math.pi) * (x + 0.044715 * (x * x * x))))


def _outproj_kernel(x_ref, gt_ref, yr_ref, bonus_ref, gate_ref, u_ref, ys_ref, ya_ref,
                    hsum_ref, gnw_ref, gnb_ref, dskip_ref, gluw_ref, glub_ref, wo_ref, lng_ref, lnb_ref, o_ref):
    tq, b, d = x_ref.shape
    rows = tq * b
    flat = lambda ref: ref[...].reshape(rows, ref.shape[-1])
    hsum = hsum_ref[...]
    inv_n = 1.0 / HEAD_DIM
    y = flat(yr_ref)
    yc = y - _dot_hi(y, hsum) * inv_n
    var = _dot_hi(yc * yc, hsum) * inv_n
    out_r = (yc * lax.rsqrt(var + GN_EPS) * gnw_ref[...] + gnb_ref[...] + flat(bonus_ref)) * flat(gate_ref)
    z = _gelu_tanh(dskip_ref[...] * flat(u_ref) + ys_ref[0] + ys_ref[1])
    out_s = z * _sigmoid(_dot(z, gluw_ref[...]) + glub_ref[...])
    W = RWKV_WIDTH
    mix = _dot(out_r, wo_ref[0:W, :]) + _dot(out_s, wo_ref[W:W + S5_WIDTH, :]) + _dot(flat(ya_ref), wo_ref[W + S5_WIDTH:, :])
    x = x_ref[...]
    res = DEEPNORM_ALPHA * x + gt_ref[0] * mix.reshape(tq, b, d)
    o_ref[...] = _layer_norm(res, lng_ref[...], lnb_ref[...])


def _output_projection(xt, mod_tab, yr, bonus, gate, u, ys, ya, lp, n_ctx_blocks, first_block):
    t, b, d = xt.shape
    nb = t // TQ - first_block
    seg = lambda i: ((i + first_block) >= n_ctx_blocks).astype(jnp.int32)
    blk = lambda w: pl.BlockSpec((TQ, b, w), lambda i: (i + first_block, 0, 0))
    full = lambda a: pl.BlockSpec(a.shape, lambda i: (0,) * a.ndim)
    consts = (lp['hsum'], lp['gn_w'], lp['gn_b'], lp['s5_d'], lp['glu_w'], lp['glu_b'], lp['w_out'], lp['ln1_g'], lp['ln1_b'])
    return pl.pallas_call(
        _outproj_kernel,
        grid=(nb,),
        in_specs=[blk(d), pl.BlockSpec((1, b, d), lambda i: (seg(i), 0, 2)),
                  blk(RWKV_WIDTH), blk(RWKV_WIDTH), blk(RWKV_WIDTH), blk(S5_WIDTH),
                  pl.BlockSpec((2, TQ * b, S5_WIDTH), lambda i: (0, i + first_block, 0)),
                  pl.BlockSpec((TQ, b, ATT_WIDTH), lambda i: (i, 0, 0))] + [full(a) for a in consts],
        out_specs=pl.BlockSpec((TQ, b, d), lambda i: (i, 0, 0)),
        out_shape=jax.ShapeDtypeStruct((nb * TQ, b, d), F32),
        compiler_params=_params("arbitrary"),
    )(xt, mod_tab, yr, bonus, gate, u, ys, ya, *consts)


def _router_kernel(x_ref, sc_ref, sh_ref, w_ref, b_ref, h_ref, idx_ref, wt_ref):
    tq, b, d = x_ref.shape
    rows = tq * b
    h = (x_ref[...] * (1.0 + sc_ref[0]) + sh_ref[0]).reshape(rows, d)
    for c in range(d // LANES):
        h_ref[pl.ds(c, rows, stride=d // LANES), :] = h[:, c * LANES:(c + 1) * LANES]
    logits = _dot_hi(h, w_ref[...]) + b_ref[...]
    lane = lax.broadcasted_iota(jnp.int32, logits.shape, 1)
    neg = -jnp.inf
    first_max = lambda vals, m: jnp.min(jnp.where(vals == m, lane, LANES), axis=1, keepdims=True)
    gl = jnp.where(lane < N_GROUPS, logits, neg)
    g_max = jnp.max(gl, axis=1, keepdims=True)
    g_idx = first_max(gl, g_max)
    g_prob = 1.0 / jnp.sum(jnp.exp(gl - g_max), axis=1, keepdims=True)
    lo = N_GROUPS + EXPERTS_PER_GROUP * g_idx
    el = jnp.where((lane >= lo) & (lane < lo + EXPERTS_PER_GROUP), logits, neg)
    m1 = jnp.max(el, axis=1, keepdims=True)
    i1 = first_max(el, m1)
    el2 = jnp.where(lane == i1, neg, el)
    m2 = jnp.max(el2, axis=1, keepdims=True)
    i2 = first_max(el2, m2)
    e2 = jnp.exp(m2 - m1)
    p1 = 1.0 / (1.0 + e2)
    p2 = e2 / (1.0 + e2)
    idx_ref[...] = jnp.where(lane == 0, i1 - N_GROUPS, jnp.where(lane == 1, i2 - N_GROUPS, 0))
    wt_ref[...] = jnp.where(lane == 0, g_prob * p1, jnp.where(lane == 1, g_prob * p2, 0.0))


def _router(x1, mod_tab, w_r, b_r, n_ctx_blocks, first_block):
    t, b, d = x1.shape
    nb = t // TQ
    seg = lambda i: ((i + first_block) >= n_ctx_blocks).astype(jnp.int32)
    rows = TQ * b
    n = nb * rows
    return pl.pallas_call(
        _router_kernel,
        grid=(nb,),
        in_specs=[pl.BlockSpec((TQ, b, d), lambda i: (i, 0, 0)),
                  pl.BlockSpec((1, b, d), lambda i: (seg(i), 0, 4)),
                  pl.BlockSpec((1, b, d), lambda i: (seg(i), 0, 3)),
                  pl.BlockSpec(w_r.shape, lambda i: (0, 0)),
                  pl.BlockSpec(b_r.shape, lambda i: (0, 0))],
        out_specs=[pl.BlockSpec((rows * (d // LANES), LANES), lambda i: (i, 0)),
                   pl.BlockSpec((rows, LANES), lambda i: (i, 0)),
                   pl.BlockSpec((rows, LANES), lambda i: (i, 0))],
        out_shape=[jax.ShapeDtypeStruct((n * (d // LANES), LANES), F32),
                   jax.ShapeDtypeStruct((n, LANES), jnp.int32),
                   jax.ShapeDtypeStruct((n, LANES), F32)],
        compiler_params=_params("arbitrary"),
    )(x1, mod_tab, mod_tab, w_r, b_r)


def _dispatch_plan(experts, n_tok, slab_rows):
    a = n_tok * 2
    p = -(-a // MOE_BLOCK) * MOE_BLOCK + N_EXPERTS * MOE_BLOCK
    n_blk = p // MOE_BLOCK
    flat_e = experts.reshape(-1)
    ck = 256
    onehot = (flat_e.reshape(a // ck, ck, 1) == jnp.arange(N_EXPERTS, dtype=jnp.int32)).astype(F32)
    tri = (jnp.arange(ck)[:, None] > jnp.arange(ck)[None, :]).astype(F32)
    within = jnp.einsum('ij,cje->cie', tri, onehot)
    tot = jnp.sum(onehot, axis=1)
    before = jnp.cumsum(tot, axis=0) - tot
    rank = jnp.sum((within + before[:, None, :]) * onehot, axis=-1).astype(jnp.int32).reshape(a)
    counts = jnp.sum(tot, axis=0).astype(jnp.int32)
    padded = (counts + MOE_BLOCK - 1) // MOE_BLOCK * MOE_BLOCK
    pend = jnp.cumsum(padded)
    dest = (pend - padded)[flat_e] + rank
    slab = slab_rows
    inv = jnp.full((p,), -1, jnp.int32).at[dest].set(jnp.arange(a, dtype=jnp.int32))
    valid = inv >= 0
    src = jnp.where(valid, (inv // 2) * slab, 0)
    spare = (n_tok + jnp.arange(p, dtype=jnp.int32) % (2 * MOE_BLOCK)) * slab
    dst = jnp.where(valid, ((inv % 2) * (n_tok + 2 * MOE_BLOCK) + inv // 2) * slab, spare)
    n_used = pend[-1] // MOE_BLOCK
    blk = jnp.minimum(jnp.arange(n_blk, dtype=jnp.int32), n_used - 1) * MOE_BLOCK
    blk_exp = jnp.minimum(jnp.sum((pend[None, :] <= blk[:, None]).astype(jnp.int32), axis=1), N_EXPERTS - 1)
    return (blk_exp, n_used.reshape(1).astype(jnp.int32),
            src.reshape(n_blk, MOE_BLOCK), dst.reshape(n_blk, MOE_BLOCK), n_blk)


def _expert_kernel(n_tok, blk_exp_ref, n_used_ref, src_ref, dst_ref, h_hbm, wg_ref, wu_ref, wd_ref, y_hbm,
                   xbuf, ybuf, wg_bf, wu_bf, wd_bf, gsem, ssem):
    i = pl.program_id(0)
    n_used = n_used_ref[0]
    slot = i % 2
    slab = xbuf.shape[1] // MOE_BLOCK
    tok_rows = lambda r: pl.ds(pl.multiple_of(r * slab, slab), slab)

    def gather(blk, sl):
        for r in range(MOE_BLOCK):
            src = pl.multiple_of(src_ref[blk, r], slab)
            pltpu.make_async_copy(h_hbm.at[pl.ds(src, slab), :], xbuf.at[sl, tok_rows(r), :], gsem.at[sl]).start()

    def scatter(blk, sl):
        for r in range(MOE_BLOCK):
            dst = pl.multiple_of(dst_ref[blk, r], slab)
            pltpu.make_async_copy(ybuf.at[sl, tok_rows(r), :], y_hbm.at[pl.ds(dst, slab), :], ssem.at[sl]).start()

    wait_gather = lambda sl: pltpu.make_async_copy(xbuf.at[sl], xbuf.at[sl], gsem.at[sl]).wait()
    wait_scatter = lambda sl: pltpu.make_async_copy(ybuf.at[sl], ybuf.at[sl], ssem.at[sl]).wait()

    @pl.when(i == 0)
    def _():
        gather(0, 0)
        ybuf[...] = jnp.zeros_like(ybuf)
        for sl in range(2):
            spare = lambda j: pltpu.make_async_copy(
                ybuf.at[sl], y_hbm.at[pl.ds((j * (n_tok + 2 * MOE_BLOCK) + n_tok + sl * MOE_BLOCK) * slab, MOE_BLOCK * slab), :],
                ssem.at[sl])
            spare(1).start()
            spare(1).wait()
            spare(0).start()

    @pl.when(i < n_used)
    def _():
        wait_gather(slot)
        gather(jnp.minimum(i + 1, n_used - 1), 1 - slot)
        @pl.when((i == 0) | (blk_exp_ref[i] != blk_exp_ref[jnp.maximum(i - 1, 0)]))
        def _():
            wg_bf[...] = wg_ref[0, 0].astype(BF16)
            wu_bf[...] = wu_ref[0, 0].astype(BF16)
            wd_bf[...] = wd_ref[0, 0].astype(BF16)

        x = jnp.concatenate([xbuf[slot, pl.ds(c, MOE_BLOCK, stride=slab), :] for c in range(slab)], axis=1).astype(BF16)
        g = jnp.dot(x, wg_bf[...], preferred_element_type=F32)
        u = jnp.dot(x, wu_bf[...], preferred_element_type=F32)
        act = (g * _sigmoid(g)) * u
        y = jnp.dot(act.astype(BF16), wd_bf[...], preferred_element_type=F32)
        wait_scatter(slot)
        for c in range(slab):
            ybuf[slot, pl.ds(c, MOE_BLOCK, stride=slab), :] = y[:, c * LANES:(c + 1) * LANES]
        scatter(i, slot)

        @pl.when(i == n_used - 1)
        def _():
            wait_gather(1 - slot)
            wait_scatter(slot)
            wait_scatter(1 - slot)


def _routed_experts(h, n_tok, plan, wg, wu, wd, layer):
    d = wg.shape[2]
    slab = d // LANES
    blk_exp, n_used, src, dst, n_blk = plan
    wspec = lambda s: pl.BlockSpec((1, 1) + s, lambda i, be, nu, sr, ds: (layer, be[i], 0, 0))
    y = pl.pallas_call(
        functools.partial(_expert_kernel, n_tok),
        grid_spec=pltpu.PrefetchScalarGridSpec(
            num_scalar_prefetch=4,
            grid=(n_blk,),
            in_specs=[pl.BlockSpec(memory_space=pl.ANY), wspec((d, D_EXPERT)), wspec((d, D_EXPERT)), wspec((D_EXPERT, d))],
            out_specs=pl.BlockSpec(memory_space=pl.ANY),
            scratch_shapes=[pltpu.VMEM((2, MOE_BLOCK * slab, LANES), F32), pltpu.VMEM((2, MOE_BLOCK * slab, LANES), F32),
                            pltpu.VMEM((d, D_EXPERT), BF16), pltpu.VMEM((d, D_EXPERT), BF16), pltpu.VMEM((D_EXPERT, d), BF16),
                            pltpu.SemaphoreType.DMA((2,)), pltpu.SemaphoreType.DMA((2,))]),
        out_shape=jax.ShapeDtypeStruct((2 * (n_tok + 2 * MOE_BLOCK) * slab, LANES), F32),
        compiler_params=_params("arbitrary"),
    )(blk_exp, n_used, src, dst, h, wg, wu, wd)
    return y.reshape(2, (n_tok + 2 * MOE_BLOCK) * slab, LANES)


def _combine_kernel(x_ref, gt_ref, y_ref, wt_ref, lng_ref, lnb_ref, o_ref):
    tq, b, d = x_ref.shape
    rows = tq * b
    slab = d // LANES
    wt = wt_ref[...]
    w0, w1 = wt[:, 0:1], wt[:, 1:2]
    y = jnp.concatenate([y_ref[0, pl.ds(c, rows, stride=slab), :] * w0 + y_ref[1, pl.ds(c, rows, stride=slab), :] * w1
                         for c in range(slab)], axis=1)
    res = DEEPNORM_ALPHA * x_ref[...] + gt_ref[0] * y.reshape(tq, b, d)
    o_ref[...] = _layer_norm(res, lng_ref[...], lnb_ref[...])


def _moe_combine(x1, mod_tab, y, wts, ln_g, ln_b, n_ctx_blocks, first_block):
    t, b, d = x1.shape
    nb = t // TQ
    rows = TQ * b
    seg = lambda i: ((i + first_block) >= n_ctx_blocks).astype(jnp.int32)
    blk = pl.BlockSpec((TQ, b, d), lambda i: (i, 0, 0))
    return pl.pallas_call(
        _combine_kernel,
        grid=(nb,),
        in_specs=[blk, pl.BlockSpec((1, b, d), lambda i: (seg(i), 0, 5)),
                  pl.BlockSpec((2, rows * (d // LANES), LANES), lambda i: (0, i, 0)),
                  pl.BlockSpec((rows, LANES), lambda i: (i, 0)),
                  pl.BlockSpec(ln_g.shape, lambda i: (0, 0)), pl.BlockSpec(ln_b.shape, lambda i: (0, 0))],
        out_specs=blk,
        out_shape=jax.ShapeDtypeStruct((t, b, d), F32),
        compiler_params=_params("arbitrary"),
    )(x1, mod_tab, y, wts, ln_g, ln_b)


def _layer(xt, c_all, lp, rope, n_ctx, last):
    t, b, d = xt.shape
    ncb = n_ctx // TQ
    first = ncb if last else 0
    mod = _modulation(c_all, lp['w_mod'], lp['b_mod'])
    mod_tab = jnp.stack([jnp.broadcast_to(mod[b:b + 1], (b, mod.shape[1])), mod[:b]], axis=0)

    pr, ps, q, k, v = _input_projection(xt, mod_tab, lp['w_in'], rope, ncb)

    r, vv, kk, d0, d1, k0, k1, b0, b1, bonus, gate = _rwkv_features(pr, lp, ncb)
    sl = _to_scan_layout
    yf, yb = _rwkv_scan((sl(r, r), sl(vv, vv), sl(kk, kk), sl(d0, d1), sl(k0, k1), sl(b0, b1)), n_ctx)
    yr = _from_scan_layout(yf, yb, b)

    ys = _s5_scan(ps, lp['s5'], ncb)

    ya = _attention(q, k, v, lp['sink'], n_ctx, True)
    if not last:
        ya = jnp.concatenate([_attention(q, k, v, lp['sink'], n_ctx, False), ya], axis=0)

    x1 = _output_projection(xt, mod_tab, yr, bonus, gate, ps, ys, ya, lp, ncb, first)
    h, idx, wts = _router(x1, mod_tab, lp['w_router'], lp['b_router'], ncb, first)
    n_tok = idx.shape[0]
    plan = _dispatch_plan(idx[:, :2], n_tok, d // LANES)
    y = _routed_experts(h, n_tok, plan, lp['w_gate'], lp['w_up'], lp['w_down'], lp['layer'])
    return _moe_combine(x1, mod_tab, y, wts, lp['ln2_g'], lp['ln2_b'], ncb, first)


def kernel(x, c, ctx, c_ctx, w_mod, b_mod, w_in, rwkv_conv, rwkv_w0, rwkv_w2, rwkv_a0, rwkv_a2, rwkv_g2, rwkv_k_k, rwkv_k_a, rwkv_r_k, rwkv_gn_w, rwkv_gn_b, s5_lam_re, s5_lam_im, s5_log_dt, s5_b_re, s5_b_im, s5_c_re, s5_c_im, s5_d, s5_glu_w, s5_glu_b, attn_sink, w_out, ln1_g, ln1_b, ln2_g, ln2_b, router_group_w, router_group_b, router_expert_w, router_expert_b, expert_w_gate, expert_w_up, expert_w_down):
    bsz, seq, d = x.shape
    n_ctx = ctx.shape[1]
    depth = w_mod.shape[0]
    W = RWKV_WIDTH
    xt = jnp.transpose(jnp.concatenate([ctx, x], axis=1), (1, 0, 2))
    c_all = jnp.zeros((bsz + 8, d), F32).at[:bsz].set(c).at[bsz].set(c_ctx)
    rope = _rope_tables(n_ctx, seq)
    hsum = jnp.kron(jnp.eye(RWKV_HEADS, dtype=F32), jnp.ones((HEAD_DIM, HEAD_DIM), F32))
    row = lambda a: a.reshape(1, -1)
    zeros_lo = jnp.zeros((RWKV_LORA // 2, 2 * W), F32)
    for i in range(depth):
        n_r = N_GROUPS + N_EXPERTS
        lp = {
            'w_mod': w_mod[i], 'b_mod': b_mod[i], 'w_in': w_in[i].astype(BF16),
            'conv': rwkv_conv[i], 'w0': rwkv_w0[i], 'a0': rwkv_a0[i],
            'w2': jnp.concatenate([jnp.concatenate([rwkv_w2[i, 0], rwkv_w2[i, 1]], axis=1), zeros_lo], axis=0),
            'a2': jnp.concatenate([zeros_lo, jnp.concatenate([rwkv_a2[i, 0], rwkv_a2[i, 1]], axis=1)], axis=0),
            'g2': rwkv_g2[i].astype(BF16), 'k_k': row(rwkv_k_k[i]), 'k_a': row(rwkv_k_a[i]), 'r_k': row(rwkv_r_k[i]),
            'hsum': hsum, 'gn_w': row(rwkv_gn_w[i]), 'gn_b': row(rwkv_gn_b[i]),
            's5': _s5_weights(s5_lam_re[i], s5_lam_im[i], s5_log_dt[i], s5_b_re[i], s5_b_im[i], s5_c_re[i], s5_c_im[i]),
            's5_d': row(s5_d[i]), 'glu_w': s5_glu_w[i].astype(BF16), 'glu_b': row(s5_glu_b[i]),
            'sink': attn_sink[i], 'w_out': w_out[i].astype(BF16),
            'ln1_g': row(ln1_g[i]), 'ln1_b': row(ln1_b[i]), 'ln2_g': row(ln2_g[i]), 'ln2_b': row(ln2_b[i]),
            'w_router': jnp.zeros((d, LANES), F32).at[:, :N_GROUPS].set(router_group_w[i]).at[:, N_GROUPS:n_r].set(router_expert_w[i]),
            'b_router': jnp.zeros((1, LANES), F32).at[0, :N_GROUPS].set(router_group_b[i]).at[0, N_GROUPS:n_r].set(router_expert_b[i]),
            'w_gate': expert_w_gate, 'w_up': expert_w_up, 'w_down': expert_w_down, 'layer': i,
        }
        xt = _layer(xt, c_all, lp, rope, n_ctx, i == depth - 1)
    return jnp.transpose(xt, (1, 0, 2))
```

```python
import functools
import math

import jax
import jax.numpy as jnp
from jax import lax
from jax.experimental import pallas as pl
from jax.experimental.pallas import tpu as pltpu

F32 = jnp.float32
BF16 = jnp.bfloat16

HEAD_DIM = 64
RWKV_WIDTH = 256
RWKV_HEADS = 4
RWKV_LORA = 128
S5_WIDTH = 256
S5_GROUPS = 16
S5_CH = 16
S5_STATE = 64
S5_LANES = S5_GROUPS * S5_STATE
ATT_WIDTH = 512
ATT_HEADS = 8
ATT_KV_HEADS = 2
ATT_GQ = 4
KV_WIDTH = ATT_KV_HEADS * HEAD_DIM
WINDOW = 128
ATT_BLOCK = 128
GRID_W = 64
ROPE_BASE = 10000.0
N_GROUPS = 4
EXPERTS_PER_GROUP = 8
N_EXPERTS = 32
D_EXPERT = 512
MOE_BLOCK = 128
N_MOD = 6
DEPTH = 2
DEEPNORM_ALPHA = (2.0 * DEPTH) ** 0.25
LN_EPS = 1e-5
GN_EPS = 64e-5
LANES = 128
TQ = 32
SCAN_UNROLL = 32
SCAN_CHUNK = 16
VMEM_LIMIT = 48 * 1024 * 1024


def _params(*sem):
    return pltpu.CompilerParams(dimension_semantics=sem, vmem_limit_bytes=VMEM_LIMIT)


def _dot(a, b):
    return jnp.dot(a.astype(BF16), b.astype(BF16), preferred_element_type=F32)


def _dot_hi(a, b):
    return jnp.dot(a, b, precision=lax.Precision.HIGHEST, preferred_element_type=F32)


def _sigmoid(x):
    return 1.0 / (1.0 + jnp.exp(-x))


def _layer_norm(x, g, b):
    mu = jnp.mean(x, axis=-1, keepdims=True)
    xc = x - mu
    var = jnp.mean(xc * xc, axis=-1, keepdims=True)
    return xc * lax.rsqrt(var + LN_EPS) * g + b


def _mod_kernel(c_ref, w_ref, b_ref, o_ref):
    c = c_ref[...]
    o_ref[...] = _dot_hi(c * _sigmoid(c), w_ref[...]) + b_ref[...]


def _modulation(c_all, w_mod, b_mod):
    rows, d = c_all.shape
    n = w_mod.shape[1]
    return pl.pallas_call(
        _mod_kernel,
        grid=(n // d,),
        in_specs=[pl.BlockSpec((rows, d), lambda j: (0, 0)),
                  pl.BlockSpec((d, d), lambda j: (0, j)),
                  pl.BlockSpec((1, d), lambda j: (0, j))],
        out_specs=pl.BlockSpec((rows, d), lambda j: (0, j)),
        out_shape=jax.ShapeDtypeStruct((rows, n), F32),
        compiler_params=_params("arbitrary"),
    )(c_all, w_mod, b_mod.reshape(1, n))


def _swap_halves(t):
    n = t.shape[-1]
    lane = lax.broadcasted_iota(jnp.int32, t.shape, t.ndim - 1)
    first = (lane % HEAD_DIM) < (HEAD_DIM // 2)
    return jnp.where(first, pltpu.roll(t, n - HEAD_DIM // 2, t.ndim - 1), pltpu.roll(t, HEAD_DIM // 2, t.ndim - 1))


def _inproj_kernel(x_ref, sc_ref, sh_ref, w_ref, cq_ref, sq_ref, ck_ref, sk_ref,
                   pr_ref, ps_ref, q_ref, k_ref, v_ref):
    tq, b, d = x_ref.shape
    h = x_ref[...] * (1.0 + sc_ref[0]) + sh_ref[0]
    p = _dot(h.reshape(tq * b, d), w_ref[...])
    o_r = RWKV_WIDTH * 3 + RWKV_LORA + 128
    o_s = o_r + S5_WIDTH
    o_q = o_s + ATT_WIDTH
    o_k = o_q + KV_WIDTH
    pr_ref[...] = p[:, :o_r].reshape(tq, b, o_r)
    ps_ref[...] = p[:, o_r:o_s].reshape(tq, b, S5_WIDTH)
    q = p[:, o_s:o_q]
    k = p[:, o_q:o_k]
    bc = lambda r, w: jnp.broadcast_to(r[...], (tq, b, w)).reshape(tq * b, w)
    q = q * bc(cq_ref, ATT_WIDTH) + _swap_halves(q) * bc(sq_ref, ATT_WIDTH)
    k = k * bc(ck_ref, KV_WIDTH) + _swap_halves(k) * bc(sk_ref, KV_WIDTH)
    q_ref[...] = q.reshape(tq, b, ATT_WIDTH).astype(BF16)
    k_ref[...] = k.reshape(tq, b, KV_WIDTH).astype(BF16)
    v_ref[...] = p[:, o_k:].reshape(tq, b, KV_WIDTH).astype(BF16)


def _input_projection(xt, mod_tab, w_in_bf, rope, n_ctx_blocks):
    t, b, d = xt.shape
    d_in = w_in_bf.shape[1]
    seg = lambda i: (i >= n_ctx_blocks).astype(jnp.int32)
    cq, sq, ck, sk = rope
    widths = (RWKV_WIDTH * 3 + RWKV_LORA + 128, S5_WIDTH, ATT_WIDTH, KV_WIDTH, KV_WIDTH)
    tab = lambda w: pl.BlockSpec((TQ, 1, w), lambda i: (i, 0, 0))
    return pl.pallas_call(
        _inproj_kernel,
        grid=(t // TQ,),
        in_specs=[pl.BlockSpec((TQ, b, d), lambda i: (i, 0, 0)),
                  pl.BlockSpec((1, b, d), lambda i: (seg(i), 0, 1)),
                  pl.BlockSpec((1, b, d), lambda i: (seg(i), 0, 0)),
                  pl.BlockSpec((d, d_in), lambda i: (0, 0)),
                  tab(ATT_WIDTH), tab(ATT_WIDTH), tab(KV_WIDTH), tab(KV_WIDTH)],
        out_specs=[pl.BlockSpec((TQ, b, w), lambda i: (i, 0, 0)) for w in widths],
        out_shape=[jax.ShapeDtypeStruct((t, b, w), dt) for w, dt in zip(widths, (F32, F32, BF16, BF16, BF16))],
        compiler_params=_params("arbitrary"),
    )(xt, mod_tab, mod_tab, w_in_bf, cq, sq, ck, sk)


def _rope_tables(n_ctx, n_lat):
    rows = n_lat // GRID_W
    row_id, col_id = jnp.meshgrid(jnp.arange(rows, dtype=F32), jnp.arange(GRID_W, dtype=F32), indexing='ij')
    n_freq = HEAD_DIM // 4
    inv_freq = ROPE_BASE ** (-jnp.arange(n_freq, dtype=F32) / n_freq)
    ang = jnp.concatenate([row_id.reshape(-1, 1) * inv_freq, col_id.reshape(-1, 1) * inv_freq], axis=-1)
    cos, sin = jnp.cos(ang), jnp.sin(ang)
    cos_h = jnp.concatenate([jnp.ones((n_ctx, HEAD_DIM), F32), jnp.concatenate([cos, cos], axis=-1)], axis=0)
    sin_h = jnp.concatenate([jnp.zeros((n_ctx, HEAD_DIM), F32), jnp.concatenate([-sin, sin], axis=-1)], axis=0)
    scale = HEAD_DIM ** -0.5
    expand = lambda tb, n, s: (jnp.tile(tb, (1, n)) * s)[:, None, :]
    return (expand(cos_h, ATT_HEADS, scale), expand(sin_h, ATT_HEADS, scale),
            expand(cos_h, ATT_KV_HEADS, 1.0), expand(sin_h, ATT_KV_HEADS, 1.0))


def _rwkv_feat_kernel(n_ctx_blocks, cur_ref, prev_ref, next_ref, conv_ref, w0_ref, a0_ref, w2_ref, a2_ref, g2_ref,
                      kk_w_ref, ka_ref, rk_ref, hsum_ref,
                      r_ref, v_ref, kk_ref, d0_ref, d1_ref, k0_ref, k1_ref, b0_ref, b1_ref, bonus_ref, gate_ref):
    i = pl.program_id(0)
    tq, b, w = cur_ref.shape
    rows = tq * b
    cur = cur_ref[...]
    keep_prev = jnp.where((i == 0) | (i == n_ctx_blocks), 0.0, 1.0)
    keep_next = jnp.where((i == n_ctx_blocks - 1) | (i == pl.num_programs(0) - 1), 0.0, 1.0)
    before = jnp.concatenate([prev_ref[...] * keep_prev, cur[:-1]], axis=0)
    after = jnp.concatenate([cur[1:], next_ref[...] * keep_next], axis=0)
    pc = (before * conv_ref[0:1, :] + cur * conv_ref[1:2, :] + after * conv_ref[2:3, :]).reshape(rows, w)
    W = RWKV_WIDTH
    r, k, v = pc[:, :W], pc[:, W:2 * W], pc[:, 2 * W:3 * W]
    lo = pc[:, 3 * W:3 * W + RWKV_LORA]
    g_lo = pc[:, 3 * W + RWKV_LORA:]
    hsum = hsum_ref[...]
    kk = k * kk_w_ref[...]
    kk = kk * lax.rsqrt(_dot_hi(kk * kk, hsum) + 1e-12)
    wl = _dot_hi(jnp.tanh(lo), w2_ref[...])
    al = _dot_hi(lo, a2_ref[...])
    k_sum = None
    for d, (d_ref, kd_ref, b_ref) in enumerate(((d0_ref, k0_ref, b0_ref), (d1_ref, k1_ref, b1_ref))):
        z = -(w0_ref[d:d + 1, :] + wl[:, d * W:(d + 1) * W])
        softplus = jnp.maximum(z, 0.0) + jnp.log(1.0 + jnp.exp(-jnp.abs(z)))
        decay = jnp.exp(-jnp.exp(-softplus - 0.5))
        a = _sigmoid(a0_ref[d:d + 1, :] + al[:, d * W:(d + 1) * W])
        k_d = k * (1.0 + (a - 1.0) * ka_ref[...])
        d_ref[...] = decay.reshape(tq, b, W)
        kd_ref[...] = k_d.reshape(tq, b, W)
        b_ref[...] = (kk * a).reshape(tq, b, W)
        k_sum = k_d if k_sum is None else k_sum + k_d
    r_ref[...] = r.reshape(tq, b, W)
    v_ref[...] = v.reshape(tq, b, W)
    kk_ref[...] = kk.reshape(tq, b, W)
    bonus_ref[...] = (_dot_hi(r * k_sum * rk_ref[...], hsum) * v).reshape(tq, b, W)
    gate_ref[...] = _dot(_sigmoid(g_lo), g2_ref[...]).reshape(tq, b, W)


def _rwkv_features(pr, lp, n_ctx_blocks):
    t, b, w = pr.shape
    nb = t // TQ
    W = RWKV_WIDTH
    full = lambda a: pl.BlockSpec(a.shape, lambda i: (0,) * a.ndim)
    consts = (lp['conv'], lp['w0'], lp['a0'], lp['w2'], lp['a2'], lp['g2'], lp['k_k'], lp['k_a'], lp['r_k'], lp['hsum'])
    return pl.pallas_call(
        functools.partial(_rwkv_feat_kernel, n_ctx_blocks),
        grid=(nb,),
        in_specs=[pl.BlockSpec((TQ, b, w), lambda i: (i, 0, 0)),
                  pl.BlockSpec((1, b, w), lambda i: (jnp.maximum(i * TQ - 1, 0), 0, 0)),
                  pl.BlockSpec((1, b, w), lambda i: (jnp.minimum((i + 1) * TQ, t - 1), 0, 0))]
                 + [full(a) for a in consts],
        out_specs=[pl.BlockSpec((TQ, b, W), lambda i: (i, 0, 0))] * 11,
        out_shape=[jax.ShapeDtypeStruct((t, b, W), F32)] * 11,
        compiler_params=_params("arbitrary"),
    )(pr, pr, pr, *consts)


def _rwkv_scan_kernel(rf_ref, vf_ref, kkf_ref, wf_ref, kf_ref, bf_ref, rb_ref, vb_ref, kkb_ref, wb_ref, kb_ref, bb_ref,
                      yf_ref, yb_ref, s_ref, x_ref):
    @pl.when(pl.program_id(0) == 0)
    def _():
        s_ref[...] = jnp.zeros_like(s_ref)

    n = HEAD_DIM
    tc, _, lanes = rf_ref.shape
    R, V, KK, W, K, B, WR = range(7)
    pairs = ((rf_ref, rb_ref), (vf_ref, vb_ref), (kkf_ref, kkb_ref), (wf_ref, wb_ref), (kf_ref, kb_ref), (bf_ref, bb_ref))
    is_fwd = lax.broadcasted_iota(jnp.int32, (n, lanes), 1) < lanes // 2

    def step(t, d_prev):
        tb = tc - 1 - t
        for q, (f_ref, b_ref) in enumerate(pairs):
            x_ref[q] = jnp.where(is_fwd, f_ref[t], b_ref[tb])
        r = x_ref[R]
        b_row, k_row = x_ref[B], x_ref[K]
        br = jnp.sum(b_row * r, axis=0, keepdims=True)
        kr = jnp.sum(k_row * r, axis=0, keepdims=True)
        d_new = d_prev * x_ref[W]
        inv_d = 1.0 / d_new
        x_ref[KK] = x_ref[KK] * d_prev
        x_ref[WR] = d_new * r
        x_ref[B] = b_row * inv_d
        x_ref[K] = k_row * inv_d

        def reduce_pass(jg, acc):
            s_kk, z = acc
            for jj in range(SCAN_UNROLL):
                j = jg * SCAN_UNROLL + jj
                s = s_ref[j]
                s_kk = s_kk + s * x_ref[KK, pl.ds(j, 1), :]
                z = z + s * x_ref[WR, pl.ds(j, 1), :]
            return s_kk, z

        zero = jnp.zeros((n, lanes), F32)
        s_kk, z = lax.fori_loop(0, n // SCAN_UNROLL, reduce_pass, (zero, zero))
        v = x_ref[V]

        def update_pass(jg, c):
            for jj in range(SCAN_UNROLL):
                j = jg * SCAN_UNROLL + jj
                s_ref[j] = s_ref[j] - s_kk * x_ref[B, pl.ds(j, 1), :] + v * x_ref[K, pl.ds(j, 1), :]
            return c

        lax.fori_loop(0, n // SCAN_UNROLL, update_pass, 0)
        y = z - s_kk * br + v * kr
        yf_ref[t] = y
        yb_ref[tb] = y
        return d_new

    x_ref[W] = lax.fori_loop(0, tc, step, jnp.ones((n, lanes), F32))

    def rescale_pass(jg, c):
        for jj in range(SCAN_UNROLL):
            j = jg * SCAN_UNROLL + jj
            s_ref[j] = s_ref[j] * x_ref[W, pl.ds(j, 1), :]
        return c

    lax.fori_loop(0, n // SCAN_UNROLL, rescale_pass, 0)


def _rwkv_scan(streams, n_ctx, tc=SCAN_CHUNK):
    t, _, lanes = streams[0].shape
    nb, ncb = t // tc, n_ctx // tc
    mirror = lambda i: jnp.where(i < ncb, ncb - 1 - i, ncb + nb - 1 - i)
    fspec = pl.BlockSpec((tc, HEAD_DIM, lanes), lambda i: (i, 0, 0))
    bspec = pl.BlockSpec((tc, HEAD_DIM, lanes), lambda i: (mirror(i), 0, 0))
    out = jax.ShapeDtypeStruct((t, HEAD_DIM, lanes), F32)
    return pl.pallas_call(
        _rwkv_scan_kernel,
        grid=(nb,),
        in_specs=[fspec] * 6 + [bspec] * 6,
        out_specs=[fspec, bspec],
        out_shape=[out, out],
        scratch_shapes=[pltpu.VMEM((HEAD_DIM, HEAD_DIM, lanes), F32), pltpu.VMEM((7, HEAD_DIM, lanes), F32)],
        compiler_params=_params("arbitrary"),
    )(*streams, *streams)


def _to_scan_layout(fwd, bwd):
    t, b, _ = fwd.shape
    tr = lambda a: jnp.transpose(a.reshape(t, b, RWKV_HEADS, HEAD_DIM), (0, 3, 1, 2)).reshape(t, HEAD_DIM, b * RWKV_HEADS)
    return jnp.concatenate([tr(fwd), tr(bwd)], axis=-1)


def _from_scan_layout(yf, yb, b):
    t, _, lanes = yf.shape
    y = yf[:, :, :lanes // 2] + yb[:, :, lanes // 2:]
    return jnp.transpose(y.reshape(t, HEAD_DIM, b, RWKV_HEADS), (0, 2, 3, 1)).reshape(t, b, RWKV_WIDTH)


def _s5_disc_kernel(lre_ref, lim_ref, ldt_ref, bre_ref, bim_ref, are_ref, aim_ref, bbre_ref, bbim_ref):
    lre, lim = lre_ref[...], lim_ref[...]
    dt = jnp.exp(ldt_ref[...])
    mag = jnp.exp(lre * dt)
    ab_re = mag * jnp.cos(lim * dt)
    ab_im = mag * jnp.sin(lim * dt)
    nr, ni = ab_re - 1.0, ab_im
    den = lre * lre + lim * lim
    coef_re = (nr * lre + ni * lim) / den
    coef_im = (ni * lre - nr * lim) / den
    are_ref[...] = ab_re
    aim_ref[...] = ab_im
    for c in range(S5_CH):
        bre, bim = bre_ref[c], bim_ref[c]
        bbre_ref[c] = coef_re * bre - coef_im * bim
        bbim_ref[c] = coef_re * bim + coef_im * bre


def _s5_weights(lam_re, lam_im, log_dt, b_re, b_im, c_re, c_im):
    G, P, CH = S5_GROUPS, S5_STATE, S5_CH
    rows = 2 * G
    bt = lambda a: jnp.broadcast_to(jnp.transpose(a, (2, 0, 1))[:, None], (CH, 2, G, P)).reshape(CH, rows, P)
    shp = jax.ShapeDtypeStruct((rows, P), F32)
    shp_b = jax.ShapeDtypeStruct((CH, rows, P), F32)
    a_re, a_im, bb_re, bb_im = pl.pallas_call(
        _s5_disc_kernel, out_shape=[shp, shp, shp_b, shp_b],
    )(lam_re.reshape(rows, P), lam_im.reshape(rows, P), log_dt.reshape(rows, 1), bt(b_re), bt(b_im))
    eye = jnp.eye(G, dtype=F32)

    def in_map(bb):
        bb = jnp.transpose(bb.reshape(CH, 2, G, P), (1, 2, 0, 3))
        return jnp.einsum('dgcp,gh->dgchp', bb, eye).reshape(2, G * CH, G * P)

    w_in = jnp.concatenate([in_map(bb_re), in_map(bb_im)], axis=-1)
    out_map = lambda c: jnp.einsum('gcp,gh->gphc', c, eye).reshape(G * P, G * CH)
    w_out = jnp.concatenate([out_map(c_re), -out_map(c_im)], axis=0)
    return w_in.astype(BF16), a_re.reshape(2, 1, G * P), a_im.reshape(2, 1, G * P), w_out.astype(BF16)


def _s5_scan_kernel(b, u_ref, win_ref, are_ref, aim_ref, wout_ref, y_ref, h_ref, st_ref):
    d = pl.program_id(0)
    rows = u_ref.shape[0]
    tc = rows // b
    n = S5_LANES

    @pl.when(pl.program_id(1) == 0)
    def _():
        st_ref[...] = jnp.zeros_like(st_ref)

    h_ref[...] = _dot(u_ref[...], win_ref[0])
    chunk = 512
    for c in range(n // chunk):
        re = slice(c * chunk, (c + 1) * chunk)
        im = slice(n + c * chunk, n + (c + 1) * chunk)
        a_re = jnp.broadcast_to(are_ref[0, :, re], (b, chunk))
        a_im = jnp.broadcast_to(aim_ref[0, :, re], (b, chunk))

        def step(t, carry):
            h_re, h_im = carry
            t_eff = jnp.where(d == 0, t, tc - 1 - t)
            row = pl.ds(pl.multiple_of(t_eff * b, b), b)
            n_re = a_re * h_re - a_im * h_im + h_ref[row, re]
            n_im = a_re * h_im + a_im * h_re + h_ref[row, im]
            h_ref[row, re] = n_re
            h_ref[row, im] = n_im
            return n_re, n_im

        h_re, h_im = lax.fori_loop(0, tc, step, (st_ref[:, re], st_ref[:, im]))
        st_ref[:, re] = h_re
        st_ref[:, im] = h_im
    y_ref[0] = _dot(h_ref[...], wout_ref[...])


def _s5_scan(ps, weights, n_ctx_blocks, tc=TQ):
    t, b, w = ps.shape
    w_in, a_re, a_im, w_out = weights
    nb = t // tc
    n_lat_blocks = nb - n_ctx_blocks
    rows = tc * b

    def blk(d, j):
        bwd = jnp.where(j < n_ctx_blocks, n_ctx_blocks - 1 - j, n_ctx_blocks + nb - 1 - j)
        return jnp.where(d == 0, j, bwd)

    del n_lat_blocks
    return pl.pallas_call(
        functools.partial(_s5_scan_kernel, b),
        grid=(2, nb),
        in_specs=[pl.BlockSpec((rows, w), lambda d, j: (blk(d, j), 0)),
                  pl.BlockSpec((1, w, 2 * S5_LANES), lambda d, j: (d, 0, 0)),
                  pl.BlockSpec((1, 1, S5_LANES), lambda d, j: (d, 0, 0)),
                  pl.BlockSpec((1, 1, S5_LANES), lambda d, j: (d, 0, 0)),
                  pl.BlockSpec((2 * S5_LANES, w), lambda d, j: (0, 0))],
        out_specs=pl.BlockSpec((1, rows, w), lambda d, j: (d, blk(d, j), 0)),
        out_shape=jax.ShapeDtypeStruct((2, t * b, w), F32),
        scratch_shapes=[pltpu.VMEM((rows, 2 * S5_LANES), F32), pltpu.VMEM((b, 2 * S5_LANES), F32)],
        compiler_params=_params("arbitrary", "arbitrary"),
    )(ps.reshape(t * b, w), w_in, a_re, a_im, w_out)


def _attn_kernel(band, sink_ref, q_ref, *refs):
    if band:
        kp_ref, k0_ref, kn_ref, vp_ref, v0_ref, vn_ref, kc_ref, vc_ref, o_ref = refs
    else:
        kc_ref, vc_ref, o_ref = refs
    i = pl.program_id(1)
    nq = q_ref.shape[0]
    q = q_ref[...].astype(BF16)
    kc = kc_ref[...].astype(BF16)
    vc = vc_ref[...].astype(BF16)
    if band:
        kb = jnp.concatenate([kp_ref[...], k0_ref[...], kn_ref[...]], axis=0).astype(BF16)
        vb = jnp.concatenate([vp_ref[...], v0_ref[...], vn_ref[...]], axis=0).astype(BF16)
        nk = kb.shape[0]
        q_pos = i * nq + lax.broadcasted_iota(jnp.int32, (nq, nk), 0)
        k_pos = (i - 1) * nq + lax.broadcasted_iota(jnp.int32, (nq, nk), 1)
        n_lat = pl.num_programs(1) * nq
        mask = (jnp.abs(q_pos - k_pos) <= WINDOW) & (k_pos >= 0) & (k_pos < n_lat)
    qk = lambda a, bb: lax.dot_general(a, bb, (((1,), (1,)), ((), ())), preferred_element_type=F32)
    outs = []
    for h in range(ATT_HEADS):
        kv = h // ATT_GQ
        cols = slice(kv * HEAD_DIM, (kv + 1) * HEAD_DIM)
        qh = q[:, h * HEAD_DIM:(h + 1) * HEAD_DIM]
        sink = sink_ref[h]
        s_c = qk(qh, kc[:, cols])
        m = jnp.maximum(jnp.max(s_c, axis=1, keepdims=True), sink)
        if band:
            s_b = jnp.where(mask, qk(qh, kb[:, cols]), -jnp.inf)
            m = jnp.maximum(m, jnp.max(s_b, axis=1, keepdims=True))
        e_c = jnp.exp(s_c - m)
        den = jnp.sum(e_c, axis=1, keepdims=True) + jnp.exp(sink - m)
        o = jnp.dot(e_c.astype(BF16), vc[:, cols], preferred_element_type=F32)
        if band:
            e_b = jnp.exp(s_b - m)
            den = den + jnp.sum(e_b, axis=1, keepdims=True)
            o = o + jnp.dot(e_b.astype(BF16), vb[:, cols], preferred_element_type=F32)
        outs.append(o / den)
    o_ref[...] = jnp.concatenate(outs, axis=1).astype(o_ref.dtype)


def _attention(q, k, v, sink, n_ctx, band):
    t, b, _ = q.shape
    q2 = q.reshape(t, b * ATT_WIDTH)
    k2 = k.reshape(t, b * KV_WIDTH)
    v2 = v.reshape(t, b * KV_WIDTH)
    cb = n_ctx // ATT_BLOCK
    nq = (t - n_ctx) // ATT_BLOCK if band else cb
    off = cb if band else 0
    qspec = pl.BlockSpec((ATT_BLOCK, ATT_WIDTH), lambda bi, i: (off + i, bi))
    kvspec = lambda f: pl.BlockSpec((ATT_BLOCK, KV_WIDTH), lambda bi, i: (cb + f(i), bi))
    prev = lambda i: jnp.maximum(i - 1, 0)
    own = lambda i: i
    nxt = lambda i: jnp.minimum(i + 1, nq - 1)
    cspec = pl.BlockSpec((n_ctx, KV_WIDTH), lambda bi, i: (0, bi))
    in_specs = [pl.BlockSpec(memory_space=pltpu.SMEM), qspec]
    args = [sink, q2]
    if band:
        in_specs += [kvspec(prev), kvspec(own), kvspec(nxt)] * 2
        args += [k2, k2, k2, v2, v2, v2]
    in_specs += [cspec, cspec]
    args += [k2, v2]
    out = pl.pallas_call(
        functools.partial(_attn_kernel, band),
        grid=(b, nq),
        in_specs=in_specs,
        out_specs=pl.BlockSpec((ATT_BLOCK, ATT_WIDTH), lambda bi, i: (i, bi)),
        out_shape=jax.ShapeDtypeStruct((nq * ATT_BLOCK, b * ATT_WIDTH), BF16),
        compiler_params=_params("arbitrary", "arbitrary"),
    )(*args)
    return out.reshape(nq * ATT_BLOCK, b, ATT_WIDTH)


def _gelu_tanh(x):
    return 0.5 * x * (1.0 + jnp.tanh(math.sqrt(2.0 / math.pi) * (x + 0.044715 * (x * x * x))))


def _outproj_kernel(x_ref, gt_ref, yr_ref, bonus_ref, gate_ref, u_ref, ys_ref, ya_ref,
                    hsum_ref, gnw_ref, gnb_ref, dskip_ref, gluw_ref, glub_ref, wo_ref, lng_ref, lnb_ref, o_ref):
    tq, b, d = x_ref.shape
    rows = tq * b
    flat = lambda ref: ref[...].reshape(rows, ref.shape[-1])
    hsum = hsum_ref[...]
    inv_n = 1.0 / HEAD_DIM
    y = flat(yr_ref)
    yc = y - _dot_hi(y, hsum) * inv_n
    var = _dot_hi(yc * yc, hsum) * inv_n
    out_r = (yc * lax.rsqrt(var + GN_EPS) * gnw_ref[...] + gnb_ref[...] + flat(bonus_ref)) * flat(gate_ref)
    z = _gelu_tanh(dskip_ref[...] * flat(u_ref) + ys_ref[0] + ys_ref[1])
    out_s = z * _sigmoid(_dot(z, gluw_ref[...]) + glub_ref[...])
    W = RWKV_WIDTH
    mix = _dot(out_r, wo_ref[0:W, :]) + _dot(out_s, wo_ref[W:W + S5_WIDTH, :]) + _dot(flat(ya_ref), wo_ref[W + S5_WIDTH:, :])
    x = x_ref[...]
    res = DEEPNORM_ALPHA * x + gt_ref[0] * mix.reshape(tq, b, d)
    o_ref[...] = _layer_norm(res, lng_ref[...], lnb_ref[...])


def _output_projection(xt, mod_tab, yr, bonus, gate, u, ys, ya, lp, n_ctx_blocks, first_block):
    t, b, d = xt.shape
    nb = t // TQ - first_block
    seg = lambda i: ((i + first_block) >= n_ctx_blocks).astype(jnp.int32)
    blk = lambda w: pl.BlockSpec((TQ, b, w), lambda i: (i + first_block, 0, 0))
    full = lambda a: pl.BlockSpec(a.shape, lambda i: (0,) * a.ndim)
    consts = (lp['hsum'], lp['gn_w'], lp['gn_b'], lp['s5_d'], lp['glu_w'], lp['glu_b'], lp['w_out'], lp['ln1_g'], lp['ln1_b'])
    return pl.pallas_call(
        _outproj_kernel,
        grid=(nb,),
        in_specs=[blk(d), pl.BlockSpec((1, b, d), lambda i: (seg(i), 0, 2)),
                  blk(RWKV_WIDTH), blk(RWKV_WIDTH), blk(RWKV_WIDTH), blk(S5_WIDTH),
                  pl.BlockSpec((2, TQ * b, S5_WIDTH), lambda i: (0, i + first_block, 0)),
                  pl.BlockSpec((TQ, b, ATT_WIDTH), lambda i: (i, 0, 0))] + [full(a) for a in consts],
        out_specs=pl.BlockSpec((TQ, b, d), lambda i: (i, 0, 0)),
        out_shape=jax.ShapeDtypeStruct((nb * TQ, b, d), F32),
        compiler_params=_params("arbitrary"),
    )(xt, mod_tab, yr, bonus, gate, u, ys, ya, *consts)


def _router_kernel(x_ref, sc_ref, sh_ref, w_ref, b_ref, h_ref, idx_ref, wt_ref):
    tq, b, d = x_ref.shape
    rows = tq * b
    h = (x_ref[...] * (1.0 + sc_ref[0]) + sh_ref[0]).reshape(rows, d)
    for c in range(d // LANES):
        h_ref[pl.ds(c, rows, stride=d // LANES), :] = h[:, c * LANES:(c + 1) * LANES]
    logits = _dot_hi(h, w_ref[...]) + b_ref[...]
    lane = lax.broadcasted_iota(jnp.int32, logits.shape, 1)
    neg = -jnp.inf
    first_max = lambda vals, m: jnp.min(jnp.where(vals == m, lane, LANES), axis=1, keepdims=True)
    gl = jnp.where(lane < N_GROUPS, logits, neg)
    g_max = jnp.max(gl, axis=1, keepdims=True)
    g_idx = first_max(gl, g_max)
    g_prob = 1.0 / jnp.sum(jnp.exp(gl - g_max), axis=1, keepdims=True)
    lo = N_GROUPS + EXPERTS_PER_GROUP * g_idx
    el = jnp.where((lane >= lo) & (lane < lo + EXPERTS_PER_GROUP), logits, neg)
    m1 = jnp.max(el, axis=1, keepdims=True)
    i1 = first_max(el, m1)
    el2 = jnp.where(lane == i1, neg, el)
    m2 = jnp.max(el2, axis=1, keepdims=True)
    i2 = first_max(el2, m2)
    e2 = jnp.exp(m2 - m1)
    p1 = 1.0 / (1.0 + e2)
    p2 = e2 / (1.0 + e2)
    idx_ref[...] = jnp.where(lane == 0, i1 - N_GROUPS, jnp.where(lane == 1, i2 - N_GROUPS, 0))
    wt_ref[...] = jnp.where(lane == 0, g_prob * p1, jnp.where(lane == 1, g_prob * p2, 0.0))


def _router(x1, mod_tab, w_r, b_r, n_ctx_blocks, first_block):
    t, b, d = x1.shape
    nb = t // TQ
    seg = lambda i: ((i + first_block) >= n_ctx_blocks).astype(jnp.int32)
    rows = TQ * b
    n = nb * rows
    return pl.pallas_call(
        _router_kernel,
        grid=(nb,),
        in_specs=[pl.BlockSpec((TQ, b, d), lambda i: (i, 0, 0)),
                  pl.BlockSpec((1, b, d), lambda i: (seg(i), 0, 4)),
                  pl.BlockSpec((1, b, d), lambda i: (seg(i), 0, 3)),
                  pl.BlockSpec(w_r.shape, lambda i: (0, 0)),
                  pl.BlockSpec(b_r.shape, lambda i: (0, 0))],
        out_specs=[pl.BlockSpec((rows * (d // LANES), LANES), lambda i: (i, 0)),
                   pl.BlockSpec((rows, LANES), lambda i: (i, 0)),
                   pl.BlockSpec((rows, LANES), lambda i: (i, 0))],
        out_shape=[jax.ShapeDtypeStruct((n * (d // LANES), LANES), F32),
                   jax.ShapeDtypeStruct((n, LANES), jnp.int32),
                   jax.ShapeDtypeStruct((n, LANES), F32)],
        compiler_params=_params("arbitrary"),
    )(x1, mod_tab, mod_tab, w_r, b_r)


def _dispatch_plan(experts, n_tok, slab_rows):
    a = n_tok * 2
    p = -(-a // MOE_BLOCK) * MOE_BLOCK + N_EXPERTS * MOE_BLOCK
    n_blk = p // MOE_BLOCK
    flat_e = experts.reshape(-1)
    skey = jnp.sort(flat_e * a + jnp.arange(a, dtype=jnp.int32))
    counts = jnp.sum((flat_e[:, None] == jnp.arange(N_EXPERTS, dtype=jnp.int32)).astype(jnp.int32), axis=0)
    start = jnp.cumsum(counts) - counts
    padded = (counts + MOE_BLOCK - 1) // MOE_BLOCK * MOE_BLOCK
    pend = jnp.cumsum(padded)
    rows = jnp.arange(p, dtype=jnp.int32)
    row_exp = jnp.minimum(jnp.sum((pend[None, :] <= rows[:, None]).astype(jnp.int32), axis=1), N_EXPERTS - 1)
    rank = rows - (pend - padded)[row_exp]
    valid = (rank < counts[row_exp]) & (rows < pend[-1])
    slab = slab_rows
    inv = skey[jnp.clip(start[row_exp] + rank, 0, a - 1)] - row_exp * a
    src = jnp.where(valid, (inv // 2) * slab, 0)
    spare = (n_tok + jnp.arange(p, dtype=jnp.int32) % (2 * MOE_BLOCK)) * slab
    dst = jnp.where(valid, ((inv % 2) * (n_tok + 2 * MOE_BLOCK) + inv // 2) * slab, spare)
    n_used = pend[-1] // MOE_BLOCK
    blk = jnp.minimum(jnp.arange(n_blk, dtype=jnp.int32), n_used - 1) * MOE_BLOCK
    blk_exp = jnp.minimum(jnp.sum((pend[None, :] <= blk[:, None]).astype(jnp.int32), axis=1), N_EXPERTS - 1)
    return (blk_exp, n_used.reshape(1).astype(jnp.int32),
            src.reshape(n_blk, MOE_BLOCK), dst.reshape(n_blk, MOE_BLOCK), n_blk)


def _expert_kernel(n_tok, blk_exp_ref, n_used_ref, src_ref, dst_ref, h_hbm, wg_ref, wu_ref, wd_ref, y_hbm,
                   xbuf, ybuf, wg_bf, wu_bf, wd_bf, gsem, ssem):
    i = pl.program_id(0)
    n_used = n_used_ref[0]
    slot = i % 2
    slab = xbuf.shape[1] // MOE_BLOCK
    tok_rows = lambda r: pl.ds(pl.multiple_of(r * slab, slab), slab)

    def gather(blk, sl):
        for r in range(MOE_BLOCK):
            src = pl.multiple_of(src_ref[blk, r], slab)
            pltpu.make_async_copy(h_hbm.at[pl.ds(src, slab), :], xbuf.at[sl, tok_rows(r), :], gsem.at[sl]).start()

    def scatter(blk, sl):
        for r in range(MOE_BLOCK):
            dst = pl.multiple_of(dst_ref[blk, r], slab)
            pltpu.make_async_copy(ybuf.at[sl, tok_rows(r), :], y_hbm.at[pl.ds(dst, slab), :], ssem.at[sl]).start()

    wait_gather = lambda sl: pltpu.make_async_copy(xbuf.at[sl], xbuf.at[sl], gsem.at[sl]).wait()
    wait_scatter = lambda sl: pltpu.make_async_copy(ybuf.at[sl], ybuf.at[sl], ssem.at[sl]).wait()

    @pl.when(i == 0)
    def _():
        gather(0, 0)
        ybuf[...] = jnp.zeros_like(ybuf)
        for sl in range(2):
            spare = lambda j: pltpu.make_async_copy(
                ybuf.at[sl], y_hbm.at[pl.ds((j * (n_tok + 2 * MOE_BLOCK) + n_tok + sl * MOE_BLOCK) * slab, MOE_BLOCK * slab), :],
                ssem.at[sl])
            spare(1).start()
            spare(1).wait()
            spare(0).start()

    @pl.when(i < n_used)
    def _():
        wait_gather(slot)
        gather(jnp.minimum(i + 1, n_used - 1), 1 - slot)
        @pl.when((i == 0) | (blk_exp_ref[i] != blk_exp_ref[jnp.maximum(i - 1, 0)]))
        def _():
            wg_bf[...] = wg_ref[0, 0].astype(BF16)
            wu_bf[...] = wu_ref[0, 0].astype(BF16)
            wd_bf[...] = wd_ref[0, 0].astype(BF16)

        x = jnp.concatenate([xbuf[slot, pl.ds(c, MOE_BLOCK, stride=slab), :] for c in range(slab)], axis=1).astype(BF16)
        g = jnp.dot(x, wg_bf[...], preferred_element_type=F32)
        u = jnp.dot(x, wu_bf[...], preferred_element_type=F32)
        act = (g * _sigmoid(g)) * u
        y = jnp.dot(act.astype(BF16), wd_bf[...], preferred_element_type=F32)
        wait_scatter(slot)
        for c in range(slab):
            ybuf[slot, pl.ds(c, MOE_BLOCK, stride=slab), :] = y[:, c * LANES:(c + 1) * LANES]
        scatter(i, slot)

        @pl.when(i == n_used - 1)
        def _():
            wait_gather(1 - slot)
            wait_scatter(slot)
            wait_scatter(1 - slot)


def _routed_experts(h, n_tok, plan, wg, wu, wd, layer):
    d = wg.shape[2]
    slab = d // LANES
    blk_exp, n_used, src, dst, n_blk = plan
    wspec = lambda s: pl.BlockSpec((1, 1) + s, lambda i, be, nu, sr, ds: (layer, be[i], 0, 0))
    y = pl.pallas_call(
        functools.partial(_expert_kernel, n_tok),
        grid_spec=pltpu.PrefetchScalarGridSpec(
            num_scalar_prefetch=4,
            grid=(n_blk,),
            in_specs=[pl.BlockSpec(memory_space=pl.ANY), wspec((d, D_EXPERT)), wspec((d, D_EXPERT)), wspec((D_EXPERT, d))],
            out_specs=pl.BlockSpec(memory_space=pl.ANY),
            scratch_shapes=[pltpu.VMEM((2, MOE_BLOCK * slab, LANES), F32), pltpu.VMEM((2, MOE_BLOCK * slab, LANES), F32),
                            pltpu.VMEM((d, D_EXPERT), BF16), pltpu.VMEM((d, D_EXPERT), BF16), pltpu.VMEM((D_EXPERT, d), BF16),
                            pltpu.SemaphoreType.DMA((2,)), pltpu.SemaphoreType.DMA((2,))]),
        out_shape=jax.ShapeDtypeStruct((2 * (n_tok + 2 * MOE_BLOCK) * slab, LANES), F32),
        compiler_params=_params("arbitrary"),
    )(blk_exp, n_used, src, dst, h, wg, wu, wd)
    return y.reshape(2, (n_tok + 2 * MOE_BLOCK) * slab, LANES)


def _combine_kernel(x_ref, gt_ref, y_ref, wt_ref, lng_ref, lnb_ref, o_ref):
    tq, b, d = x_ref.shape
    rows = tq * b
    slab = d // LANES
    wt = wt_ref[...]
    w0, w1 = wt[:, 0:1], wt[:, 1:2]
    y = jnp.concatenate([y_ref[0, pl.ds(c, rows, stride=slab), :] * w0 + y_ref[1, pl.ds(c, rows, stride=slab), :] * w1
                         for c in range(slab)], axis=1)
    res = DEEPNORM_ALPHA * x_ref[...] + gt_ref[0] * y.reshape(tq, b, d)
    o_ref[...] = _layer_norm(res, lng_ref[...], lnb_ref[...])


def _moe_combine(x1, mod_tab, y, wts, ln_g, ln_b, n_ctx_blocks, first_block):
    t, b, d = x1.shape
    nb = t // TQ
    rows = TQ * b
    seg = lambda i: ((i + first_block) >= n_ctx_blocks).astype(jnp.int32)
    blk = pl.BlockSpec((TQ, b, d), lambda i: (i, 0, 0))
    return pl.pallas_call(
        _combine_kernel,
        grid=(nb,),
        in_specs=[blk, pl.BlockSpec((1, b, d), lambda i: (seg(i), 0, 5)),
                  pl.BlockSpec((2, rows * (d // LANES), LANES), lambda i: (0, i, 0)),
                  pl.BlockSpec((rows, LANES), lambda i: (i, 0)),
                  pl.BlockSpec(ln_g.shape, lambda i: (0, 0)), pl.BlockSpec(ln_b.shape, lambda i: (0, 0))],
        out_specs=blk,
        out_shape=jax.ShapeDtypeStruct((t, b, d), F32),
        compiler_params=_params("arbitrary"),
    )(x1, mod_tab, y, wts, ln_g, ln_b)


def _layer(xt, c_all, lp, rope, n_ctx, last):
    t, b, d = xt.shape
    ncb = n_ctx // TQ
    first = ncb if last else 0
    mod = _modulation(c_all, lp['w_mod'], lp['b_mod'])
    mod_tab = jnp.stack([jnp.broadcast_to(mod[b:b + 1], (b, mod.shape[1])), mod[:b]], axis=0)

    pr, ps, q, k, v = _input_projection(xt, mod_tab, lp['w_in'], rope, ncb)

    r, vv, kk, d0, d1, k0, k1, b0, b1, bonus, gate = _rwkv_features(pr, lp, ncb)
    sl = _to_scan_layout
    yf, yb = _rwkv_scan((sl(r, r), sl(vv, vv), sl(kk, kk), sl(d0, d1), sl(k0, k1), sl(b0, b1)), n_ctx)
    yr = _from_scan_layout(yf, yb, b)

    ys = _s5_scan(ps, lp['s5'], ncb)

    ya = _attention(q, k, v, lp['sink'], n_ctx, True)
    if not last:
        ya = jnp.concatenate([_attention(q, k, v, lp['sink'], n_ctx, False), ya], axis=0)

    x1 = _output_projection(xt, mod_tab, yr, bonus, gate, ps, ys, ya, lp, ncb, first)
    h, idx, wts = _router(x1, mod_tab, lp['w_router'], lp['b_router'], ncb, first)
    n_tok = idx.shape[0]
    plan = _dispatch_plan(idx[:, :2], n_tok, d // LANES)
    y = _routed_experts(h, n_tok, plan, lp['w_gate'], lp['w_up'], lp['w_down'], lp['layer'])
    return _moe_combine(x1, mod_tab, y, wts, lp['ln2_g'], lp['ln2_b'], ncb, first)


def kernel(x, c, ctx, c_ctx, w_mod, b_mod, w_in, rwkv_conv, rwkv_w0, rwkv_w2, rwkv_a0, rwkv_a2, rwkv_g2, rwkv_k_k, rwkv_k_a, rwkv_r_k, rwkv_gn_w, rwkv_gn_b, s5_lam_re, s5_lam_im, s5_log_dt, s5_b_re, s5_b_im, s5_c_re, s5_c_im, s5_d, s5_glu_w, s5_glu_b, attn_sink, w_out, ln1_g, ln1_b, ln2_g, ln2_b, router_group_w, router_group_b, router_expert_w, router_expert_b, expert_w_gate, expert_w_up, expert_w_down):
    bsz, seq, d = x.shape
    n_ctx = ctx.shape[1]
    depth = w_mod.shape[0]
    W = RWKV_WIDTH
    xt = jnp.transpose(jnp.concatenate([ctx, x], axis=1), (1, 0, 2))
    c_all = jnp.zeros((bsz + 8, d), F32).at[:bsz].set(c).at[bsz].set(c_ctx)
    rope = _rope_tables(n_ctx, seq)
    hsum = jnp.kron(jnp.eye(RWKV_HEADS, dtype=F32), jnp.ones((HEAD_DIM, HEAD_DIM), F32))
    row = lambda a: a.reshape(1, -1)
    zeros_lo = jnp.zeros((RWKV_LORA // 2, 2 * W), F32)
    for i in range(depth):
        n_r = N_GROUPS + N_EXPERTS
        lp = {
            'w_mod': w_mod[i], 'b_mod': b_mod[i], 'w_in': w_in[i].astype(BF16),
            'conv': rwkv_conv[i], 'w0': rwkv_w0[i], 'a0': rwkv_a0[i],
            'w2': jnp.concatenate([jnp.concatenate([rwkv_w2[i, 0], rwkv_w2[i, 1]], axis=1), zeros_lo], axis=0),
            'a2': jnp.concatenate([zeros_lo, jnp.concatenate([rwkv_a2[i, 0], rwkv_a2[i, 1]], axis=1)], axis=0),
            'g2': rwkv_g2[i].astype(BF16), 'k_k': row(rwkv_k_k[i]), 'k_a': row(rwkv_k_a[i]), 'r_k': row(rwkv_r_k[i]),
            'hsum': hsum, 'gn_w': row(rwkv_gn_w[i]), 'gn_b': row(rwkv_gn_b[i]),
            's5': _s5_weights(s5_lam_re[i], s5_lam_im[i], s5_log_dt[i], s5_b_re[i], s5_b_im[i], s5_c_re[i], s5_c_im[i]),
            's5_d': row(s5_d[i]), 'glu_w': s5_glu_w[i].astype(BF16), 'glu_b': row(s5_glu_b[i]),
            'sink': attn_sink[i], 'w_out': w_out[i].astype(BF16),
            'ln1_g': row(ln1_g[i]), 'ln1_b': row(ln1_b[i]), 'ln2_g': row(ln2_g[i]), 'ln2_b': row(ln2_b[i]),
            'w_router': jnp.zeros((d, LANES), F32).at[:, :N_GROUPS].set(router_group_w[i]).at[:, N_GROUPS:n_r].set(router_expert_w[i]),
            'b_router': jnp.zeros((1, LANES), F32).at[0, :N_GROUPS].set(router_group_b[i]).at[0, N_GROUPS:n_r].set(router_expert_b[i]),
            'w_gate': expert_w_gate, 'w_up': expert_w_up, 'w_down': expert_w_down, 'layer': i,
        }
        xt = _layer(xt, c_all, lp, rope, n_ctx, i == depth - 1)
    return jnp.transpose(xt, (1, 0, 2))
```

```python
import functools
import math

import jax
import jax.numpy as jnp
from jax import lax
from jax.experimental import pallas as pl
from jax.experimental.pallas import tpu as pltpu

F32 = jnp.float32
BF16 = jnp.bfloat16

HEAD_DIM = 64
RWKV_WIDTH = 256
RWKV_HEADS = 4
RWKV_LORA = 128
S5_WIDTH = 256
S5_GROUPS = 16
S5_CH = 16
S5_STATE = 64
S5_LANES = S5_GROUPS * S5_STATE
ATT_WIDTH = 512
ATT_HEADS = 8
ATT_KV_HEADS = 2
ATT_GQ = 4
KV_WIDTH = ATT_KV_HEADS * HEAD_DIM
WINDOW = 128
ATT_BLOCK = 128
GRID_W = 64
ROPE_BASE = 10000.0
N_GROUPS = 4
EXPERTS_PER_GROUP = 8
N_EXPERTS = 32
D_EXPERT = 512
MOE_BLOCK = 128
N_MOD = 6
DEPTH = 2
DEEPNORM_ALPHA = (2.0 * DEPTH) ** 0.25
LN_EPS = 1e-5
GN_EPS = 64e-5
LANES = 128
TQ = 32
SCAN_UNROLL = 32
SCAN_CHUNK = 16
VMEM_LIMIT = 48 * 1024 * 1024


def _params(*sem):
    return pltpu.CompilerParams(dimension_semantics=sem, vmem_limit_bytes=VMEM_LIMIT)


def _dot(a, b):
    return jnp.dot(a.astype(BF16), b.astype(BF16), preferred_element_type=F32)


def _dot_hi(a, b):
    return jnp.dot(a, b, precision=lax.Precision.HIGHEST, preferred_element_type=F32)


def _sigmoid(x):
    return 1.0 / (1.0 + jnp.exp(-x))


def _layer_norm(x, g, b):
    mu = jnp.mean(x, axis=-1, keepdims=True)
    xc = x - mu
    var = jnp.mean(xc * xc, axis=-1, keepdims=True)
    return xc * lax.rsqrt(var + LN_EPS) * g + b


def _mod_kernel(c_ref, w_ref, b_ref, o_ref):
    c = c_ref[...]
    o_ref[...] = _dot_hi(c * _sigmoid(c), w_ref[...]) + b_ref[...]


def _modulation(c_all, w_mod, b_mod):
    rows, d = c_all.shape
    n = w_mod.shape[1]
    return pl.pallas_call(
        _mod_kernel,
        grid=(n // d,),
        in_specs=[pl.BlockSpec((rows, d), lambda j: (0, 0)),
                  pl.BlockSpec((d, d), lambda j: (0, j)),
                  pl.BlockSpec((1, d), lambda j: (0, j))],
        out_specs=pl.BlockSpec((rows, d), lambda j: (0, j)),
        out_shape=jax.ShapeDtypeStruct((rows, n), F32),
        compiler_params=_params("arbitrary"),
    )(c_all, w_mod, b_mod.reshape(1, n))


def _swap_halves(t):
    n = t.shape[-1]
    lane = lax.broadcasted_iota(jnp.int32, t.shape, t.ndim - 1)
    first = (lane % HEAD_DIM) < (HEAD_DIM // 2)
    return jnp.where(first, pltpu.roll(t, n - HEAD_DIM // 2, t.ndim - 1), pltpu.roll(t, HEAD_DIM // 2, t.ndim - 1))


def _inproj_kernel(x_ref, sc_ref, sh_ref, w_ref, cq_ref, sq_ref, ck_ref, sk_ref,
                   pr_ref, ps_ref, q_ref, k_ref, v_ref):
    tq, b, d = x_ref.shape
    h = x_ref[...] * (1.0 + sc_ref[0]) + sh_ref[0]
    p = _dot(h.reshape(tq * b, d), w_ref[...])
    o_r = RWKV_WIDTH * 3 + RWKV_LORA + 128
    o_s = o_r + S5_WIDTH
    o_q = o_s + ATT_WIDTH
    o_k = o_q + KV_WIDTH
    pr_ref[...] = p[:, :o_r].reshape(tq, b, o_r)
    ps_ref[...] = p[:, o_r:o_s].reshape(tq, b, S5_WIDTH)
    q = p[:, o_s:o_q]
    k = p[:, o_q:o_k]
    bc = lambda r, w: jnp.broadcast_to(r[...], (tq, b, w)).reshape(tq * b, w)
    q = q * bc(cq_ref, ATT_WIDTH) + _swap_halves(q) * bc(sq_ref, ATT_WIDTH)
    k = k * bc(ck_ref, KV_WIDTH) + _swap_halves(k) * bc(sk_ref, KV_WIDTH)
    q_ref[...] = q.reshape(tq, b, ATT_WIDTH).astype(BF16)
    k_ref[...] = k.reshape(tq, b, KV_WIDTH).astype(BF16)
    v_ref[...] = p[:, o_k:].reshape(tq, b, KV_WIDTH).astype(BF16)


def _input_projection(xt, mod_tab, w_in_bf, rope, n_ctx_blocks):
    t, b, d = xt.shape
    d_in = w_in_bf.shape[1]
    seg = lambda i: (i >= n_ctx_blocks).astype(jnp.int32)
    cq, sq, ck, sk = rope
    widths = (RWKV_WIDTH * 3 + RWKV_LORA + 128, S5_WIDTH, ATT_WIDTH, KV_WIDTH, KV_WIDTH)
    tab = lambda w: pl.BlockSpec((TQ, 1, w), lambda i: (i, 0, 0))
    return pl.pallas_call(
        _inproj_kernel,
        grid=(t // TQ,),
        in_specs=[pl.BlockSpec((TQ, b, d), lambda i: (i, 0, 0)),
                  pl.BlockSpec((1, b, d), lambda i: (seg(i), 0, 1)),
                  pl.BlockSpec((1, b, d), lambda i: (seg(i), 0, 0)),
                  pl.BlockSpec((d, d_in), lambda i: (0, 0)),
                  tab(ATT_WIDTH), tab(ATT_WIDTH), tab(KV_WIDTH), tab(KV_WIDTH)],
        out_specs=[pl.BlockSpec((TQ, b, w), lambda i: (i, 0, 0)) for w in widths],
        out_shape=[jax.ShapeDtypeStruct((t, b, w), dt) for w, dt in zip(widths, (F32, F32, BF16, BF16, BF16))],
        compiler_params=_params("arbitrary"),
    )(xt, mod_tab, mod_tab, w_in_bf, cq, sq, ck, sk)


def _rope_tables(n_ctx, n_lat):
    rows = n_lat // GRID_W
    row_id, col_id = jnp.meshgrid(jnp.arange(rows, dtype=F32), jnp.arange(GRID_W, dtype=F32), indexing='ij')
    n_freq = HEAD_DIM // 4
    inv_freq = ROPE_BASE ** (-jnp.arange(n_freq, dtype=F32) / n_freq)
    ang = jnp.concatenate([row_id.reshape(-1, 1) * inv_freq, col_id.reshape(-1, 1) * inv_freq], axis=-1)
    cos, sin = jnp.cos(ang), jnp.sin(ang)
    cos_h = jnp.concatenate([jnp.ones((n_ctx, HEAD_DIM), F32), jnp.concatenate([cos, cos], axis=-1)], axis=0)
    sin_h = jnp.concatenate([jnp.zeros((n_ctx, HEAD_DIM), F32), jnp.concatenate([-sin, sin], axis=-1)], axis=0)
    scale = HEAD_DIM ** -0.5
    expand = lambda tb, n, s: (jnp.tile(tb, (1, n)) * s)[:, None, :]
    return (expand(cos_h, ATT_HEADS, scale), expand(sin_h, ATT_HEADS, scale),
            expand(cos_h, ATT_KV_HEADS, 1.0), expand(sin_h, ATT_KV_HEADS, 1.0))


def _rwkv_feat_kernel(n_ctx_blocks, cur_ref, prev_ref, next_ref, conv_ref, w0_ref, a0_ref, w2_ref, a2_ref, g2_ref,
                      kk_w_ref, ka_ref, rk_ref, hsum_ref,
                      r_ref, v_ref, kk_ref, d0_ref, d1_ref, k0_ref, k1_ref, b0_ref, b1_ref, bonus_ref, gate_ref):
    i = pl.program_id(0)
    tq, b, w = cur_ref.shape
    rows = tq * b
    cur = cur_ref[...]
    keep_prev = jnp.where((i == 0) | (i == n_ctx_blocks), 0.0, 1.0)
    keep_next = jnp.where((i == n_ctx_blocks - 1) | (i == pl.num_programs(0) - 1), 0.0, 1.0)
    before = jnp.concatenate([prev_ref[...] * keep_prev, cur[:-1]], axis=0)
    after = jnp.concatenate([cur[1:], next_ref[...] * keep_next], axis=0)
    pc = (before * conv_ref[0:1, :] + cur * conv_ref[1:2, :] + after * conv_ref[2:3, :]).reshape(rows, w)
    W = RWKV_WIDTH
    r, k, v = pc[:, :W], pc[:, W:2 * W], pc[:, 2 * W:3 * W]
    lo = pc[:, 3 * W:3 * W + RWKV_LORA]
    g_lo = pc[:, 3 * W + RWKV_LORA:]
    hsum = hsum_ref[...]
    kk = k * kk_w_ref[...]
    kk = kk * lax.rsqrt(_dot_hi(kk * kk, hsum) + 1e-12)
    wl = _dot_hi(jnp.tanh(lo), w2_ref[...])
    al = _dot_hi(lo, a2_ref[...])
    k_sum = None
    for d, (d_ref, kd_ref, b_ref) in enumerate(((d0_ref, k0_ref, b0_ref), (d1_ref, k1_ref, b1_ref))):
        z = -(w0_ref[d:d + 1, :] + wl[:, d * W:(d + 1) * W])
        softplus = jnp.maximum(z, 0.0) + jnp.log(1.0 + jnp.exp(-jnp.abs(z)))
        decay = jnp.exp(-jnp.exp(-softplus - 0.5))
        a = _sigmoid(a0_ref[d:d + 1, :] + al[:, d * W:(d + 1) * W])
        k_d = k * (1.0 + (a - 1.0) * ka_ref[...])
        d_ref[...] = decay.reshape(tq, b, W)
        kd_ref[...] = k_d.reshape(tq, b, W)
        b_ref[...] = (kk * a).reshape(tq, b, W)
        k_sum = k_d if k_sum is None else k_sum + k_d
    r_ref[...] = r.reshape(tq, b, W)
    v_ref[...] = v.reshape(tq, b, W)
    kk_ref[...] = kk.reshape(tq, b, W)
    bonus_ref[...] = (_dot_hi(r * k_sum * rk_ref[...], hsum) * v).reshape(tq, b, W)
    gate_ref[...] = _dot(_sigmoid(g_lo), g2_ref[...]).reshape(tq, b, W)


def _rwkv_features(pr, lp, n_ctx_blocks):
    t, b, w = pr.shape
    nb = t // TQ
    W = RWKV_WIDTH
    full = lambda a: pl.BlockSpec(a.shape, lambda i: (0,) * a.ndim)
    consts = (lp['conv'], lp['w0'], lp['a0'], lp['w2'], lp['a2'], lp['g2'], lp['k_k'], lp['k_a'], lp['r_k'], lp['hsum'])
    return pl.pallas_call(
        functools.partial(_rwkv_feat_kernel, n_ctx_blocks),
        grid=(nb,),
        in_specs=[pl.BlockSpec((TQ, b, w), lambda i: (i, 0, 0)),
                  pl.BlockSpec((1, b, w), lambda i: (jnp.maximum(i * TQ - 1, 0), 0, 0)),
                  pl.BlockSpec((1, b, w), lambda i: (jnp.minimum((i + 1) * TQ, t - 1), 0, 0))]
                 + [full(a) for a in consts],
        out_specs=[pl.BlockSpec((TQ, b, W), lambda i: (i, 0, 0))] * 11,
        out_shape=[jax.ShapeDtypeStruct((t, b, W), F32)] * 11,
        compiler_params=_params("arbitrary"),
    )(pr, pr, pr, *consts)


def _rwkv_scan_kernel(rf_ref, vf_ref, kkf_ref, wf_ref, kf_ref, bf_ref, rb_ref, vb_ref, kkb_ref, wb_ref, kb_ref, bb_ref,
                      yf_ref, yb_ref, s_ref, x_ref):
    @pl.when(pl.program_id(0) == 0)
    def _():
        s_ref[...] = jnp.zeros_like(s_ref)

    n = HEAD_DIM
    tc, _, lanes = rf_ref.shape
    R, V, KK, W, K, B, WR = range(7)
    pairs = ((rf_ref, rb_ref), (vf_ref, vb_ref), (kkf_ref, kkb_ref), (wf_ref, wb_ref), (kf_ref, kb_ref), (bf_ref, bb_ref))
    is_fwd = lax.broadcasted_iota(jnp.int32, (n, lanes), 1) < lanes // 2

    def step(t, d_prev):
        tb = tc - 1 - t
        for q, (f_ref, b_ref) in enumerate(pairs):
            x_ref[q] = jnp.where(is_fwd, f_ref[t], b_ref[tb])
        r = x_ref[R]
        b_row, k_row = x_ref[B], x_ref[K]
        br = jnp.sum(b_row * r, axis=0, keepdims=True)
        kr = jnp.sum(k_row * r, axis=0, keepdims=True)
        d_new = d_prev * x_ref[W]
        inv_d = 1.0 / d_new
        x_ref[KK] = x_ref[KK] * d_prev
        x_ref[WR] = d_new * r
        x_ref[B] = b_row * inv_d
        x_ref[K] = k_row * inv_d

        def reduce_pass(jg, acc):
            s_kk, z = acc
            for jj in range(SCAN_UNROLL):
                j = jg * SCAN_UNROLL + jj
                s = s_ref[j]
                s_kk = s_kk + s * x_ref[KK, pl.ds(j, 1), :]
                z = z + s * x_ref[WR, pl.ds(j, 1), :]
            return s_kk, z

        zero = jnp.zeros((n, lanes), F32)
        s_kk, z = lax.fori_loop(0, n // SCAN_UNROLL, reduce_pass, (zero, zero))
        v = x_ref[V]

        def update_pass(jg, c):
            for jj in range(SCAN_UNROLL):
                j = jg * SCAN_UNROLL + jj
                s_ref[j] = s_ref[j] - s_kk * x_ref[B, pl.ds(j, 1), :] + v * x_ref[K, pl.ds(j, 1), :]
            return c

        lax.fori_loop(0, n // SCAN_UNROLL, update_pass, 0)
        y = z - s_kk * br + v * kr
        yf_ref[t] = y
        yb_ref[tb] = y
        return d_new

    x_ref[W] = lax.fori_loop(0, tc, step, jnp.ones((n, lanes), F32))

    def rescale_pass(jg, c):
        for jj in range(SCAN_UNROLL):
            j = jg * SCAN_UNROLL + jj
            s_ref[j] = s_ref[j] * x_ref[W, pl.ds(j, 1), :]
        return c

    lax.fori_loop(0, n // SCAN_UNROLL, rescale_pass, 0)


def _rwkv_scan(streams, n_ctx, tc=SCAN_CHUNK):
    t, _, lanes = streams[0].shape
    nb, ncb = t // tc, n_ctx // tc
    mirror = lambda i: jnp.where(i < ncb, ncb - 1 - i, ncb + nb - 1 - i)
    fspec = pl.BlockSpec((tc, HEAD_DIM, lanes), lambda i: (i, 0, 0))
    bspec = pl.BlockSpec((tc, HEAD_DIM, lanes), lambda i: (mirror(i), 0, 0))
    out = jax.ShapeDtypeStruct((t, HEAD_DIM, lanes), F32)
    return pl.pallas_call(
        _rwkv_scan_kernel,
        grid=(nb,),
        in_specs=[fspec] * 6 + [bspec] * 6,
        out_specs=[fspec, bspec],
        out_shape=[out, out],
        scratch_shapes=[pltpu.VMEM((HEAD_DIM, HEAD_DIM, lanes), F32), pltpu.VMEM((7, HEAD_DIM, lanes), F32)],
        compiler_params=_params("arbitrary"),
    )(*streams, *streams)


def _to_scan_layout(fwd, bwd):
    t, b, _ = fwd.shape
    tr = lambda a: jnp.transpose(a.reshape(t, b, RWKV_HEADS, HEAD_DIM), (0, 3, 1, 2)).reshape(t, HEAD_DIM, b * RWKV_HEADS)
    return jnp.concatenate([tr(fwd), tr(bwd)], axis=-1)


def _from_scan_layout(yf, yb, b):
    t, _, lanes = yf.shape
    y = yf[:, :, :lanes // 2] + yb[:, :, lanes // 2:]
    return jnp.transpose(y.reshape(t, HEAD_DIM, b, RWKV_HEADS), (0, 2, 3, 1)).reshape(t, b, RWKV_WIDTH)


def _s5_disc_kernel(lre_ref, lim_ref, ldt_ref, bre_ref, bim_ref, are_ref, aim_ref, bbre_ref, bbim_ref):
    lre, lim = lre_ref[...], lim_ref[...]
    dt = jnp.exp(ldt_ref[...])
    mag = jnp.exp(lre * dt)
    ab_re = mag * jnp.cos(lim * dt)
    ab_im = mag * jnp.sin(lim * dt)
    nr, ni = ab_re - 1.0, ab_im
    den = lre * lre + lim * lim
    coef_re = (nr * lre + ni * lim) / den
    coef_im = (ni * lre - nr * lim) / den
    are_ref[...] = ab_re
    aim_ref[...] = ab_im
    for c in range(S5_CH):
        bre, bim = bre_ref[c], bim_ref[c]
        bbre_ref[c] = coef_re * bre - coef_im * bim
        bbim_ref[c] = coef_re * bim + coef_im * bre


def _s5_weights(lam_re, lam_im, log_dt, b_re, b_im, c_re, c_im):
    G, P, CH = S5_GROUPS, S5_STATE, S5_CH
    rows = 2 * G
    bt = lambda a: jnp.broadcast_to(jnp.transpose(a, (2, 0, 1))[:, None], (CH, 2, G, P)).reshape(CH, rows, P)
    shp = jax.ShapeDtypeStruct((rows, P), F32)
    shp_b = jax.ShapeDtypeStruct((CH, rows, P), F32)
    a_re, a_im, bb_re, bb_im = pl.pallas_call(
        _s5_disc_kernel, out_shape=[shp, shp, shp_b, shp_b],
    )(lam_re.reshape(rows, P), lam_im.reshape(rows, P), log_dt.reshape(rows, 1), bt(b_re), bt(b_im))
    eye = jnp.eye(G, dtype=F32)

    def in_map(bb):
        bb = jnp.transpose(bb.reshape(CH, 2, G, P), (1, 2, 0, 3))
        return jnp.einsum('dgcp,gh->dgchp', bb, eye).reshape(2, G * CH, G * P)

    w_in = jnp.concatenate([in_map(bb_re), in_map(bb_im)], axis=-1)
    out_map = lambda c: jnp.einsum('gcp,gh->gphc', c, eye).reshape(G * P, G * CH)
    w_out = jnp.concatenate([out_map(c_re), -out_map(c_im)], axis=0)
    return w_in.astype(BF16), a_re.reshape(2, 1, G * P), a_im.reshape(2, 1, G * P), w_out.astype(BF16)


def _s5_scan_kernel(b, u_ref, win_ref, are_ref, aim_ref, wout_ref, y_ref, h_ref, st_ref):
    d = pl.program_id(0)
    rows = u_ref.shape[0]
    tc = rows // b
    n = S5_LANES

    @pl.when(pl.program_id(1) == 0)
    def _():
        st_ref[...] = jnp.zeros_like(st_ref)

    h_ref[...] = _dot(u_ref[...], win_ref[0])
    chunk = 512
    for c in range(n // chunk):
        re = slice(c * chunk, (c + 1) * chunk)
        im = slice(n + c * chunk, n + (c + 1) * chunk)
        a_re = jnp.broadcast_to(are_ref[0, :, re], (b, chunk))
        a_im = jnp.broadcast_to(aim_ref[0, :, re], (b, chunk))

        def step(t, carry):
            h_re, h_im = carry
            t_eff = jnp.where(d == 0, t, tc - 1 - t)
            row = pl.ds(pl.multiple_of(t_eff * b, b), b)
            n_re = a_re * h_re - a_im * h_im + h_ref[row, re]
            n_im = a_re * h_im + a_im * h_re + h_ref[row, im]
            h_ref[row, re] = n_re
            h_ref[row, im] = n_im
            return n_re, n_im

        h_re, h_im = lax.fori_loop(0, tc, step, (st_ref[:, re], st_ref[:, im]))
        st_ref[:, re] = h_re
        st_ref[:, im] = h_im
    y_ref[0] = _dot(h_ref[...], wout_ref[...])


def _s5_scan(ps, weights, n_ctx_blocks, tc=TQ):
    t, b, w = ps.shape
    w_in, a_re, a_im, w_out = weights
    nb = t // tc
    n_lat_blocks = nb - n_ctx_blocks
    rows = tc * b

    def blk(d, j):
        bwd = jnp.where(j < n_ctx_blocks, n_ctx_blocks - 1 - j, n_ctx_blocks + nb - 1 - j)
        return jnp.where(d == 0, j, bwd)

    del n_lat_blocks
    return pl.pallas_call(
        functools.partial(_s5_scan_kernel, b),
        grid=(2, nb),
        in_specs=[pl.BlockSpec((rows, w), lambda d, j: (blk(d, j), 0)),
                  pl.BlockSpec((1, w, 2 * S5_LANES), lambda d, j: (d, 0, 0)),
                  pl.BlockSpec((1, 1, S5_LANES), lambda d, j: (d, 0, 0)),
                  pl.BlockSpec((1, 1, S5_LANES), lambda d, j: (d, 0, 0)),
                  pl.BlockSpec((2 * S5_LANES, w), lambda d, j: (0, 0))],
        out_specs=pl.BlockSpec((1, rows, w), lambda d, j: (d, blk(d, j), 0)),
        out_shape=jax.ShapeDtypeStruct((2, t * b, w), F32),
        scratch_shapes=[pltpu.VMEM((rows, 2 * S5_LANES), F32), pltpu.VMEM((b, 2 * S5_LANES), F32)],
        compiler_params=_params("arbitrary", "arbitrary"),
    )(ps.reshape(t * b, w), w_in, a_re, a_im, w_out)


def _attn_kernel(band, sink_ref, q_ref, *refs):
    if band:
        kp_ref, k0_ref, kn_ref, vp_ref, v0_ref, vn_ref, kc_ref, vc_ref, o_ref, s_ref, p_ref = refs
        keys = jnp.concatenate([kp_ref[...], k0_ref[...], kn_ref[...], kc_ref[...]], axis=0)
        vals = jnp.concatenate([vp_ref[...], v0_ref[...], vn_ref[...], vc_ref[...]], axis=0)
    else:
        kc_ref, vc_ref, o_ref, s_ref, p_ref = refs
        keys, vals = kc_ref[...], vc_ref[...]
    i = pl.program_id(1)
    nq = q_ref.shape[0]
    nk = keys.shape[0]
    q = q_ref[...]
    qk = lambda a, bb: lax.dot_general(a, bb, (((1,), (1,)), ((), ())), preferred_element_type=F32)
    kv_cols = lambda h: slice((h // ATT_GQ) * HEAD_DIM, (h // ATT_GQ + 1) * HEAD_DIM)
    for h in range(ATT_HEADS):
        s_ref[h] = qk(q[:, h * HEAD_DIM:(h + 1) * HEAD_DIM], keys[:, kv_cols(h)])
    if band:
        n_band = nk - kc_ref.shape[0]
        col = lax.broadcasted_iota(jnp.int32, (nq, nk), 1)
        q_pos = i * nq + lax.broadcasted_iota(jnp.int32, (nq, nk), 0)
        k_pos = (i - 1) * nq + col
        n_lat = pl.num_programs(1) * nq
        seen = (col >= n_band) | ((jnp.abs(q_pos - k_pos) <= WINDOW) & (k_pos >= 0) & (k_pos < n_lat))
        bias = jnp.where(seen, 0.0, -jnp.inf)
    inv_den = []
    for h in range(ATT_HEADS):
        s = s_ref[h] + bias if band else s_ref[h]
        sink = sink_ref[h]
        m = jnp.maximum(jnp.max(s, axis=1, keepdims=True), sink)
        e = jnp.exp(s - m)
        inv_den.append(1.0 / (jnp.sum(e, axis=1, keepdims=True) + jnp.exp(sink - m)))
        p_ref[h] = e.astype(BF16)
    outs = [jnp.dot(p_ref[h], vals[:, kv_cols(h)], preferred_element_type=F32) * inv_den[h] for h in range(ATT_HEADS)]
    o_ref[...] = jnp.concatenate(outs, axis=1).astype(o_ref.dtype)


def _attention(q, k, v, sink, n_ctx, band):
    t, b, _ = q.shape
    q2 = q.reshape(t, b * ATT_WIDTH)
    k2 = k.reshape(t, b * KV_WIDTH)
    v2 = v.reshape(t, b * KV_WIDTH)
    cb = n_ctx // ATT_BLOCK
    nq = (t - n_ctx) // ATT_BLOCK if band else cb
    off = cb if band else 0
    qspec = pl.BlockSpec((ATT_BLOCK, ATT_WIDTH), lambda bi, i: (off + i, bi))
    kvspec = lambda f: pl.BlockSpec((ATT_BLOCK, KV_WIDTH), lambda bi, i: (cb + f(i), bi))
    prev = lambda i: jnp.maximum(i - 1, 0)
    own = lambda i: i
    nxt = lambda i: jnp.minimum(i + 1, nq - 1)
    cspec = pl.BlockSpec((n_ctx, KV_WIDTH), lambda bi, i: (0, bi))
    in_specs = [pl.BlockSpec(memory_space=pltpu.SMEM), qspec]
    args = [sink, q2]
    if band:
        in_specs += [kvspec(prev), kvspec(own), kvspec(nxt)] * 2
        args += [k2, k2, k2, v2, v2, v2]
    in_specs += [cspec, cspec]
    args += [k2, v2]
    n_keys = (3 * ATT_BLOCK if band else 0) + n_ctx
    out = pl.pallas_call(
        functools.partial(_attn_kernel, band),
        grid=(b, nq),
        in_specs=in_specs,
        out_specs=pl.BlockSpec((ATT_BLOCK, ATT_WIDTH), lambda bi, i: (i, bi)),
        out_shape=jax.ShapeDtypeStruct((nq * ATT_BLOCK, b * ATT_WIDTH), BF16),
        scratch_shapes=[pltpu.VMEM((ATT_HEADS, ATT_BLOCK, n_keys), F32), pltpu.VMEM((ATT_HEADS, ATT_BLOCK, n_keys), BF16)],
        compiler_params=_params("arbitrary", "arbitrary"),
    )(*args)
    return out.reshape(nq * ATT_BLOCK, b, ATT_WIDTH)


def _gelu_tanh(x):
    return 0.5 * x * (1.0 + jnp.tanh(math.sqrt(2.0 / math.pi) * (x + 0.044715 * (x * x * x))))


def _outproj_kernel(x_ref, gt_ref, yr_ref, bonus_ref, gate_ref, u_ref, ys_ref, ya_ref,
                    hsum_ref, gnw_ref, gnb_ref, dskip_ref, gluw_ref, glub_ref, wo_ref, lng_ref, lnb_ref, o_ref):
    tq, b, d = x_ref.shape
    rows = tq * b
    flat = lambda ref: ref[...].reshape(rows, ref.shape[-1])
    hsum = hsum_ref[...]
    inv_n = 1.0 / HEAD_DIM
    y = flat(yr_ref)
    yc = y - _dot_hi(y, hsum) * inv_n
    var = _dot_hi(yc * yc, hsum) * inv_n
    out_r = (yc * lax.rsqrt(var + GN_EPS) * gnw_ref[...] + gnb_ref[...] + flat(bonus_ref)) * flat(gate_ref)
    z = _gelu_tanh(dskip_ref[...] * flat(u_ref) + ys_ref[0] + ys_ref[1])
    out_s = z * _sigmoid(_dot(z, gluw_ref[...]) + glub_ref[...])
    W = RWKV_WIDTH
    mix = _dot(out_r, wo_ref[0:W, :]) + _dot(out_s, wo_ref[W:W + S5_WIDTH, :]) + _dot(flat(ya_ref), wo_ref[W + S5_WIDTH:, :])
    x = x_ref[...]
    res = DEEPNORM_ALPHA * x + gt_ref[0] * mix.reshape(tq, b, d)
    o_ref[...] = _layer_norm(res, lng_ref[...], lnb_ref[...])


def _output_projection(xt, mod_tab, yr, bonus, gate, u, ys, ya, lp, n_ctx_blocks, first_block):
    t, b, d = xt.shape
    nb = t // TQ - first_block
    seg = lambda i: ((i + first_block) >= n_ctx_blocks).astype(jnp.int32)
    blk = lambda w: pl.BlockSpec((TQ, b, w), lambda i: (i + first_block, 0, 0))
    full = lambda a: pl.BlockSpec(a.shape, lambda i: (0,) * a.ndim)
    consts = (lp['hsum'], lp['gn_w'], lp['gn_b'], lp['s5_d'], lp['glu_w'], lp['glu_b'], lp['w_out'], lp['ln1_g'], lp['ln1_b'])
    return pl.pallas_call(
        _outproj_kernel,
        grid=(nb,),
        in_specs=[blk(d), pl.BlockSpec((1, b, d), lambda i: (seg(i), 0, 2)),
                  blk(RWKV_WIDTH), blk(RWKV_WIDTH), blk(RWKV_WIDTH), blk(S5_WIDTH),
                  pl.BlockSpec((2, TQ * b, S5_WIDTH), lambda i: (0, i + first_block, 0)),
                  pl.BlockSpec((TQ, b, ATT_WIDTH), lambda i: (i, 0, 0))] + [full(a) for a in consts],
        out_specs=pl.BlockSpec((TQ, b, d), lambda i: (i, 0, 0)),
        out_shape=jax.ShapeDtypeStruct((nb * TQ, b, d), F32),
        compiler_params=_params("arbitrary"),
    )(xt, mod_tab, yr, bonus, gate, u, ys, ya, *consts)


def _router_kernel(x_ref, sc_ref, sh_ref, w_ref, b_ref, h_ref, idx_ref, wt_ref):
    tq, b, d = x_ref.shape
    rows = tq * b
    h = (x_ref[...] * (1.0 + sc_ref[0]) + sh_ref[0]).reshape(rows, d)
    for c in range(d // LANES):
        h_ref[pl.ds(c, rows, stride=d // LANES), :] = h[:, c * LANES:(c + 1) * LANES]
    logits = _dot_hi(h, w_ref[...]) + b_ref[...]
    lane = lax.broadcasted_iota(jnp.int32, logits.shape, 1)
    neg = -jnp.inf
    first_max = lambda vals, m: jnp.min(jnp.where(vals == m, lane, LANES), axis=1, keepdims=True)
    gl = jnp.where(lane < N_GROUPS, logits, neg)
    g_max = jnp.max(gl, axis=1, keepdims=True)
    g_idx = first_max(gl, g_max)
    g_prob = 1.0 / jnp.sum(jnp.exp(gl - g_max), axis=1, keepdims=True)
    lo = N_GROUPS + EXPERTS_PER_GROUP * g_idx
    el = jnp.where((lane >= lo) & (lane < lo + EXPERTS_PER_GROUP), logits, neg)
    m1 = jnp.max(el, axis=1, keepdims=True)
    i1 = first_max(el, m1)
    el2 = jnp.where(lane == i1, neg, el)
    m2 = jnp.max(el2, axis=1, keepdims=True)
    i2 = first_max(el2, m2)
    e2 = jnp.exp(m2 - m1)
    p1 = 1.0 / (1.0 + e2)
    p2 = e2 / (1.0 + e2)
    idx_ref[...] = jnp.where(lane == 0, i1 - N_GROUPS, jnp.where(lane == 1, i2 - N_GROUPS, 0))
    wt_ref[...] = jnp.where(lane == 0, g_prob * p1, jnp.where(lane == 1, g_prob * p2, 0.0))


def _router(x1, mod_tab, w_r, b_r, n_ctx_blocks, first_block):
    t, b, d = x1.shape
    nb = t // TQ
    seg = lambda i: ((i + first_block) >= n_ctx_blocks).astype(jnp.int32)
    rows = TQ * b
    n = nb * rows
    return pl.pallas_call(
        _router_kernel,
        grid=(nb,),
        in_specs=[pl.BlockSpec((TQ, b, d), lambda i: (i, 0, 0)),
                  pl.BlockSpec((1, b, d), lambda i: (seg(i), 0, 4)),
                  pl.BlockSpec((1, b, d), lambda i: (seg(i), 0, 3)),
                  pl.BlockSpec(w_r.shape, lambda i: (0, 0)),
                  pl.BlockSpec(b_r.shape, lambda i: (0, 0))],
        out_specs=[pl.BlockSpec((rows * (d // LANES), LANES), lambda i: (i, 0)),
                   pl.BlockSpec((rows, LANES), lambda i: (i, 0)),
                   pl.BlockSpec((rows, LANES), lambda i: (i, 0))],
        out_shape=[jax.ShapeDtypeStruct((n * (d // LANES), LANES), F32),
                   jax.ShapeDtypeStruct((n, LANES), jnp.int32),
                   jax.ShapeDtypeStruct((n, LANES), F32)],
        compiler_params=_params("arbitrary"),
    )(x1, mod_tab, mod_tab, w_r, b_r)


def _dispatch_plan(experts, n_tok, slab_rows):
    a = n_tok * 2
    p = -(-a // MOE_BLOCK) * MOE_BLOCK + N_EXPERTS * MOE_BLOCK
    n_blk = p // MOE_BLOCK
    flat_e = experts.reshape(-1)
    skey = jnp.sort(flat_e * a + jnp.arange(a, dtype=jnp.int32))
    counts = jnp.sum((flat_e[:, None] == jnp.arange(N_EXPERTS, dtype=jnp.int32)).astype(jnp.int32), axis=0)
    start = jnp.cumsum(counts) - counts
    padded = (counts + MOE_BLOCK - 1) // MOE_BLOCK * MOE_BLOCK
    pend = jnp.cumsum(padded)
    rows = jnp.arange(p, dtype=jnp.int32)
    row_exp = jnp.minimum(jnp.sum((pend[None, :] <= rows[:, None]).astype(jnp.int32), axis=1), N_EXPERTS - 1)
    rank = rows - (pend - padded)[row_exp]
    valid = (rank < counts[row_exp]) & (rows < pend[-1])
    slab = slab_rows
    inv = skey[jnp.clip(start[row_exp] + rank, 0, a - 1)] - row_exp * a
    src = jnp.where(valid, (inv // 2) * slab, 0)
    spare = (n_tok + jnp.arange(p, dtype=jnp.int32) % (2 * MOE_BLOCK)) * slab
    dst = jnp.where(valid, ((inv % 2) * (n_tok + 2 * MOE_BLOCK) + inv // 2) * slab, spare)
    n_used = pend[-1] // MOE_BLOCK
    blk = jnp.minimum(jnp.arange(n_blk, dtype=jnp.int32), n_used - 1) * MOE_BLOCK
    blk_exp = jnp.minimum(jnp.sum((pend[None, :] <= blk[:, None]).astype(jnp.int32), axis=1), N_EXPERTS - 1)
    return (blk_exp, n_used.reshape(1).astype(jnp.int32),
            src.reshape(n_blk, MOE_BLOCK), dst.reshape(n_blk, MOE_BLOCK), n_blk)


def _expert_kernel(n_tok, blk_exp_ref, n_used_ref, src_ref, dst_ref, h_hbm, wg_ref, wu_ref, wd_ref, y_hbm,
                   xbuf, ybuf, wg_bf, wu_bf, wd_bf, gsem, ssem):
    i = pl.program_id(0)
    n_used = n_used_ref[0]
    slot = i % 2
    slab = xbuf.shape[1] // MOE_BLOCK
    tok_rows = lambda r: pl.ds(pl.multiple_of(r * slab, slab), slab)

    def gather(blk, sl):
        for r in range(MOE_BLOCK):
            src = pl.multiple_of(src_ref[blk, r], slab)
            pltpu.make_async_copy(h_hbm.at[pl.ds(src, slab), :], xbuf.at[sl, tok_rows(r), :], gsem.at[sl]).start()

    def scatter(blk, sl):
        for r in range(MOE_BLOCK):
            dst = pl.multiple_of(dst_ref[blk, r], slab)
            pltpu.make_async_copy(ybuf.at[sl, tok_rows(r), :], y_hbm.at[pl.ds(dst, slab), :], ssem.at[sl]).start()

    wait_gather = lambda sl: pltpu.make_async_copy(xbuf.at[sl], xbuf.at[sl], gsem.at[sl]).wait()
    wait_scatter = lambda sl: pltpu.make_async_copy(ybuf.at[sl], ybuf.at[sl], ssem.at[sl]).wait()

    @pl.when(i == 0)
    def _():
        gather(0, 0)
        ybuf[...] = jnp.zeros_like(ybuf)
        for sl in range(2):
            spare = lambda j: pltpu.make_async_copy(
                ybuf.at[sl], y_hbm.at[pl.ds((j * (n_tok + 2 * MOE_BLOCK) + n_tok + sl * MOE_BLOCK) * slab, MOE_BLOCK * slab), :],
                ssem.at[sl])
            spare(1).start()
            spare(1).wait()
            spare(0).start()

    @pl.when(i < n_used)
    def _():
        wait_gather(slot)
        gather(jnp.minimum(i + 1, n_used - 1), 1 - slot)
        @pl.when((i == 0) | (blk_exp_ref[i] != blk_exp_ref[jnp.maximum(i - 1, 0)]))
        def _():
            wg_bf[...] = wg_ref[0, 0].astype(BF16)
            wu_bf[...] = wu_ref[0, 0].astype(BF16)
            wd_bf[...] = wd_ref[0, 0].astype(BF16)

        x = jnp.concatenate([xbuf[slot, pl.ds(c, MOE_BLOCK, stride=slab), :] for c in range(slab)], axis=1).astype(BF16)
        g = jnp.dot(x, wg_bf[...], preferred_element_type=F32)
        u = jnp.dot(x, wu_bf[...], preferred_element_type=F32)
        act = (g * _sigmoid(g)) * u
        y = jnp.dot(act.astype(BF16), wd_bf[...], preferred_element_type=F32)
        wait_scatter(slot)
        for c in range(slab):
            ybuf[slot, pl.ds(c, MOE_BLOCK, stride=slab), :] = y[:, c * LANES:(c + 1) * LANES]
        scatter(i, slot)

        @pl.when(i == n_used - 1)
        def _():
            wait_gather(1 - slot)
            wait_scatter(slot)
            wait_scatter(1 - slot)


def _routed_experts(h, n_tok, plan, wg, wu, wd, layer):
    d = wg.shape[2]
    slab = d // LANES
    blk_exp, n_used, src, dst, n_blk = plan
    wspec = lambda s: pl.BlockSpec((1, 1) + s, lambda i, be, nu, sr, ds: (layer, be[i], 0, 0))
    y = pl.pallas_call(
        functools.partial(_expert_kernel, n_tok),
        grid_spec=pltpu.PrefetchScalarGridSpec(
            num_scalar_prefetch=4,
            grid=(n_blk,),
            in_specs=[pl.BlockSpec(memory_space=pl.ANY), wspec((d, D_EXPERT)), wspec((d, D_EXPERT)), wspec((D_EXPERT, d))],
            out_specs=pl.BlockSpec(memory_space=pl.ANY),
            scratch_shapes=[pltpu.VMEM((2, MOE_BLOCK * slab, LANES), F32), pltpu.VMEM((2, MOE_BLOCK * slab, LANES), F32),
                            pltpu.VMEM((d, D_EXPERT), BF16), pltpu.VMEM((d, D_EXPERT), BF16), pltpu.VMEM((D_EXPERT, d), BF16),
                            pltpu.SemaphoreType.DMA((2,)), pltpu.SemaphoreType.DMA((2,))]),
        out_shape=jax.ShapeDtypeStruct((2 * (n_tok + 2 * MOE_BLOCK) * slab, LANES), F32),
        compiler_params=_params("arbitrary"),
    )(blk_exp, n_used, src, dst, h, wg, wu, wd)
    return y.reshape(2, (n_tok + 2 * MOE_BLOCK) * slab, LANES)


def _combine_kernel(x_ref, gt_ref, y_ref, wt_ref, lng_ref, lnb_ref, o_ref):
    tq, b, d = x_ref.shape
    rows = tq * b
    slab = d // LANES
    wt = wt_ref[...]
    w0, w1 = wt[:, 0:1], wt[:, 1:2]
    y = jnp.concatenate([y_ref[0, pl.ds(c, rows, stride=slab), :] * w0 + y_ref[1, pl.ds(c, rows, stride=slab), :] * w1
                         for c in range(slab)], axis=1)
    res = DEEPNORM_ALPHA * x_ref[...] + gt_ref[0] * y.reshape(tq, b, d)
    o_ref[...] = _layer_norm(res, lng_ref[...], lnb_ref[...])


def _moe_combine(x1, mod_tab, y, wts, ln_g, ln_b, n_ctx_blocks, first_block):
    t, b, d = x1.shape
    nb = t // TQ
    rows = TQ * b
    seg = lambda i: ((i + first_block) >= n_ctx_blocks).astype(jnp.int32)
    blk = pl.BlockSpec((TQ, b, d), lambda i: (i, 0, 0))
    return pl.pallas_call(
        _combine_kernel,
        grid=(nb,),
        in_specs=[blk, pl.BlockSpec((1, b, d), lambda i: (seg(i), 0, 5)),
                  pl.BlockSpec((2, rows * (d // LANES), LANES), lambda i: (0, i, 0)),
                  pl.BlockSpec((rows, LANES), lambda i: (i, 0)),
                  pl.BlockSpec(ln_g.shape, lambda i: (0, 0)), pl.BlockSpec(ln_b.shape, lambda i: (0, 0))],
        out_specs=blk,
        out_shape=jax.ShapeDtypeStruct((t, b, d), F32),
        compiler_params=_params("arbitrary"),
    )(x1, mod_tab, y, wts, ln_g, ln_b)


def _layer(xt, c_all, lp, rope, n_ctx, last):
    t, b, d = xt.shape
    ncb = n_ctx // TQ
    first = ncb if last else 0
    mod = _modulation(c_all, lp['w_mod'], lp['b_mod'])
    mod_tab = jnp.stack([jnp.broadcast_to(mod[b:b + 1], (b, mod.shape[1])), mod[:b]], axis=0)

    pr, ps, q, k, v = _input_projection(xt, mod_tab, lp['w_in'], rope, ncb)

    r, vv, kk, d0, d1, k0, k1, b0, b1, bonus, gate = _rwkv_features(pr, lp, ncb)
    sl = _to_scan_layout
    yf, yb = _rwkv_scan((sl(r, r), sl(vv, vv), sl(kk, kk), sl(d0, d1), sl(k0, k1), sl(b0, b1)), n_ctx)
    yr = _from_scan_layout(yf, yb, b)

    ys = _s5_scan(ps, lp['s5'], ncb)

    ya = _attention(q, k, v, lp['sink'], n_ctx, True)
    if not last:
        ya = jnp.concatenate([_attention(q, k, v, lp['sink'], n_ctx, False), ya], axis=0)

    x1 = _output_projection(xt, mod_tab, yr, bonus, gate, ps, ys, ya, lp, ncb, first)
    h, idx, wts = _router(x1, mod_tab, lp['w_router'], lp['b_router'], ncb, first)
    n_tok = idx.shape[0]
    plan = _dispatch_plan(idx[:, :2], n_tok, d // LANES)
    y = _routed_experts(h, n_tok, plan, lp['w_gate'], lp['w_up'], lp['w_down'], lp['layer'])
    return _moe_combine(x1, mod_tab, y, wts, lp['ln2_g'], lp['ln2_b'], ncb, first)


def kernel(x, c, ctx, c_ctx, w_mod, b_mod, w_in, rwkv_conv, rwkv_w0, rwkv_w2, rwkv_a0, rwkv_a2, rwkv_g2, rwkv_k_k, rwkv_k_a, rwkv_r_k, rwkv_gn_w, rwkv_gn_b, s5_lam_re, s5_lam_im, s5_log_dt, s5_b_re, s5_b_im, s5_c_re, s5_c_im, s5_d, s5_glu_w, s5_glu_b, attn_sink, w_out, ln1_g, ln1_b, ln2_g, ln2_b, router_group_w, router_group_b, router_expert_w, router_expert_b, expert_w_gate, expert_w_up, expert_w_down):
    bsz, seq, d = x.shape
    n_ctx = ctx.shape[1]
    depth = w_mod.shape[0]
    W = RWKV_WIDTH
    xt = jnp.transpose(jnp.concatenate([ctx, x], axis=1), (1, 0, 2))
    c_all = jnp.zeros((bsz + 8, d), F32).at[:bsz].set(c).at[bsz].set(c_ctx)
    rope = _rope_tables(n_ctx, seq)
    hsum = jnp.kron(jnp.eye(RWKV_HEADS, dtype=F32), jnp.ones((HEAD_DIM, HEAD_DIM), F32))
    row = lambda a: a.reshape(1, -1)
    zeros_lo = jnp.zeros((RWKV_LORA // 2, 2 * W), F32)
    for i in range(depth):
        n_r = N_GROUPS + N_EXPERTS
        lp = {
            'w_mod': w_mod[i], 'b_mod': b_mod[i], 'w_in': w_in[i].astype(BF16),
            'conv': rwkv_conv[i], 'w0': rwkv_w0[i], 'a0': rwkv_a0[i],
            'w2': jnp.concatenate([jnp.concatenate([rwkv_w2[i, 0], rwkv_w2[i, 1]], axis=1), zeros_lo], axis=0),
            'a2': jnp.concatenate([zeros_lo, jnp.concatenate([rwkv_a2[i, 0], rwkv_a2[i, 1]], axis=1)], axis=0),
            'g2': rwkv_g2[i].astype(BF16), 'k_k': row(rwkv_k_k[i]), 'k_a': row(rwkv_k_a[i]), 'r_k': row(rwkv_r_k[i]),
            'hsum': hsum, 'gn_w': row(rwkv_gn_w[i]), 'gn_b': row(rwkv_gn_b[i]),
            's5': _s5_weights(s5_lam_re[i], s5_lam_im[i], s5_log_dt[i], s5_b_re[i], s5_b_im[i], s5_c_re[i], s5_c_im[i]),
            's5_d': row(s5_d[i]), 'glu_w': s5_glu_w[i].astype(BF16), 'glu_b': row(s5_glu_b[i]),
            'sink': attn_sink[i], 'w_out': w_out[i].astype(BF16),
            'ln1_g': row(ln1_g[i]), 'ln1_b': row(ln1_b[i]), 'ln2_g': row(ln2_g[i]), 'ln2_b': row(ln2_b[i]),
            'w_router': jnp.zeros((d, LANES), F32).at[:, :N_GROUPS].set(router_group_w[i]).at[:, N_GROUPS:n_r].set(router_expert_w[i]),
            'b_router': jnp.zeros((1, LANES), F32).at[0, :N_GROUPS].set(router_group_b[i]).at[0, N_GROUPS:n_r].set(router_expert_b[i]),
            'w_gate': expert_w_gate, 'w_up': expert_w_up, 'w_down': expert_w_down, 'layer': i,
        }
        xt = _layer(xt, c_all, lp, rope, n_ctx, i == depth - 1)
    return jnp.transpose(xt, (1, 0, 2))
```

```python
import functools
import math

import jax
import jax.numpy as jnp
from jax import lax
from jax.experimental import pallas as pl
from jax.experimental.pallas import tpu as pltpu

F32 = jnp.float32
BF16 = jnp.bfloat16

HEAD_DIM = 64
RWKV_WIDTH = 256
RWKV_HEADS = 4
RWKV_LORA = 128
S5_WIDTH = 256
S5_GROUPS = 16
S5_CH = 16
S5_STATE = 64
S5_LANES = S5_GROUPS * S5_STATE
ATT_WIDTH = 512
ATT_HEADS = 8
ATT_KV_HEADS = 2
ATT_GQ = 4
KV_WIDTH = ATT_KV_HEADS * HEAD_DIM
WINDOW = 128
ATT_BLOCK = 128
GRID_W = 64
ROPE_BASE = 10000.0
N_GROUPS = 4
EXPERTS_PER_GROUP = 8
N_EXPERTS = 32
D_EXPERT = 512
MOE_BLOCK = 128
N_MOD = 6
DEPTH = 2
DEEPNORM_ALPHA = (2.0 * DEPTH) ** 0.25
LN_EPS = 1e-5
GN_EPS = 64e-5
LANES = 128
TQ = 32
SCAN_UNROLL = 32
SCAN_CHUNK = 16
VMEM_LIMIT = 48 * 1024 * 1024


def _params(*sem):
    return pltpu.CompilerParams(dimension_semantics=sem, vmem_limit_bytes=VMEM_LIMIT)


def _dot(a, b):
    return jnp.dot(a.astype(BF16), b.astype(BF16), preferred_element_type=F32)


def _dot_hi(a, b):
    return jnp.dot(a, b, precision=lax.Precision.HIGHEST, preferred_element_type=F32)


def _split_bf16(x):
    hi = x.astype(BF16)
    return hi, (x - hi.astype(F32)).astype(BF16)


def _dot_sum(a, ones):
    a_hi, a_lo = _split_bf16(a)
    return jnp.dot(a_hi, ones, preferred_element_type=F32) + jnp.dot(a_lo, ones, preferred_element_type=F32)


def _dot_split(a, b):
    a_hi, a_lo = _split_bf16(a)
    b_hi, b_lo = _split_bf16(b)
    d = lambda x, y: jnp.dot(x, y, preferred_element_type=F32)
    return d(a_hi, b_hi) + d(a_lo, b_hi) + d(a_hi, b_lo)


def _sigmoid(x):
    return 1.0 / (1.0 + jnp.exp(-x))


def _layer_norm(x, g, b):
    mu = jnp.mean(x, axis=-1, keepdims=True)
    xc = x - mu
    var = jnp.mean(xc * xc, axis=-1, keepdims=True)
    return xc * lax.rsqrt(var + LN_EPS) * g + b


def _mod_kernel(c_ref, w_ref, b_ref, o_ref):
    c = c_ref[...]
    o_ref[...] = _dot_hi(c * _sigmoid(c), w_ref[...]) + b_ref[...]


def _modulation(c_all, w_mod, b_mod):
    rows, d = c_all.shape
    n = w_mod.shape[1]
    return pl.pallas_call(
        _mod_kernel,
        grid=(n // d,),
        in_specs=[pl.BlockSpec((rows, d), lambda j: (0, 0)),
                  pl.BlockSpec((d, d), lambda j: (0, j)),
                  pl.BlockSpec((1, d), lambda j: (0, j))],
        out_specs=pl.BlockSpec((rows, d), lambda j: (0, j)),
        out_shape=jax.ShapeDtypeStruct((rows, n), F32),
        compiler_params=_params("arbitrary"),
    )(c_all, w_mod, b_mod.reshape(1, n))


def _swap_halves(t):
    n = t.shape[-1]
    lane = lax.broadcasted_iota(jnp.int32, t.shape, t.ndim - 1)
    first = (lane % HEAD_DIM) < (HEAD_DIM // 2)
    return jnp.where(first, pltpu.roll(t, n - HEAD_DIM // 2, t.ndim - 1), pltpu.roll(t, HEAD_DIM // 2, t.ndim - 1))


def _inproj_kernel(x_ref, sc_ref, sh_ref, w_ref, cq_ref, sq_ref, ck_ref, sk_ref,
                   pr_ref, ps_ref, q_ref, k_ref, v_ref):
    tq, b, d = x_ref.shape
    h = x_ref[...] * (1.0 + sc_ref[0]) + sh_ref[0]
    p = _dot(h.reshape(tq * b, d), w_ref[...])
    o_r = RWKV_WIDTH * 3 + RWKV_LORA + 128
    o_s = o_r + S5_WIDTH
    o_q = o_s + ATT_WIDTH
    o_k = o_q + KV_WIDTH
    pr_ref[...] = p[:, :o_r].reshape(tq, b, o_r)
    ps_ref[...] = p[:, o_r:o_s].reshape(tq, b, S5_WIDTH)
    q = p[:, o_s:o_q]
    k = p[:, o_q:o_k]
    bc = lambda r, w: jnp.broadcast_to(r[...], (tq, b, w)).reshape(tq * b, w)
    q = q * bc(cq_ref, ATT_WIDTH) + _swap_halves(q) * bc(sq_ref, ATT_WIDTH)
    k = k * bc(ck_ref, KV_WIDTH) + _swap_halves(k) * bc(sk_ref, KV_WIDTH)
    q_ref[...] = q.reshape(tq, b, ATT_WIDTH).astype(BF16)
    k_ref[...] = k.reshape(tq, b, KV_WIDTH).astype(BF16)
    v_ref[...] = p[:, o_k:].reshape(tq, b, KV_WIDTH).astype(BF16)


def _input_projection(xt, mod_tab, w_in_bf, rope, n_ctx_blocks):
    t, b, d = xt.shape
    d_in = w_in_bf.shape[1]
    seg = lambda i: (i >= n_ctx_blocks).astype(jnp.int32)
    cq, sq, ck, sk = rope
    widths = (RWKV_WIDTH * 3 + RWKV_LORA + 128, S5_WIDTH, ATT_WIDTH, KV_WIDTH, KV_WIDTH)
    tab = lambda w: pl.BlockSpec((TQ, 1, w), lambda i: (i, 0, 0))
    return pl.pallas_call(
        _inproj_kernel,
        grid=(t // TQ,),
        in_specs=[pl.BlockSpec((TQ, b, d), lambda i: (i, 0, 0)),
                  pl.BlockSpec((1, b, d), lambda i: (seg(i), 0, 1)),
                  pl.BlockSpec((1, b, d), lambda i: (seg(i), 0, 0)),
                  pl.BlockSpec((d, d_in), lambda i: (0, 0)),
                  tab(ATT_WIDTH), tab(ATT_WIDTH), tab(KV_WIDTH), tab(KV_WIDTH)],
        out_specs=[pl.BlockSpec((TQ, b, w), lambda i: (i, 0, 0)) for w in widths],
        out_shape=[jax.ShapeDtypeStruct((t, b, w), dt) for w, dt in zip(widths, (F32, F32, BF16, BF16, BF16))],
        compiler_params=_params("arbitrary"),
    )(xt, mod_tab, mod_tab, w_in_bf, cq, sq, ck, sk)


def _rope_tables(n_ctx, n_lat):
    rows = n_lat // GRID_W
    row_id, col_id = jnp.meshgrid(jnp.arange(rows, dtype=F32), jnp.arange(GRID_W, dtype=F32), indexing='ij')
    n_freq = HEAD_DIM // 4
    inv_freq = ROPE_BASE ** (-jnp.arange(n_freq, dtype=F32) / n_freq)
    ang = jnp.concatenate([row_id.reshape(-1, 1) * inv_freq, col_id.reshape(-1, 1) * inv_freq], axis=-1)
    cos, sin = jnp.cos(ang), jnp.sin(ang)
    cos_h = jnp.concatenate([jnp.ones((n_ctx, HEAD_DIM), F32), jnp.concatenate([cos, cos], axis=-1)], axis=0)
    sin_h = jnp.concatenate([jnp.zeros((n_ctx, HEAD_DIM), F32), jnp.concatenate([-sin, sin], axis=-1)], axis=0)
    scale = HEAD_DIM ** -0.5
    expand = lambda tb, n, s: (jnp.tile(tb, (1, n)) * s)[:, None, :]
    return (expand(cos_h, ATT_HEADS, scale), expand(sin_h, ATT_HEADS, scale),
            expand(cos_h, ATT_KV_HEADS, 1.0), expand(sin_h, ATT_KV_HEADS, 1.0))


def _rwkv_feat_kernel(n_ctx_blocks, cur_ref, prev_ref, next_ref, conv_ref, w0_ref, a0_ref, w2_ref, a2_ref, g2_ref,
                      kk_w_ref, ka_ref, rk_ref, hsum_ref,
                      r_ref, v_ref, kk_ref, d0_ref, d1_ref, k0_ref, k1_ref, b0_ref, b1_ref, bonus_ref, gate_ref):
    i = pl.program_id(0)
    tq, b, w = cur_ref.shape
    rows = tq * b
    cur = cur_ref[...]
    keep_prev = jnp.where((i == 0) | (i == n_ctx_blocks), 0.0, 1.0)
    keep_next = jnp.where((i == n_ctx_blocks - 1) | (i == pl.num_programs(0) - 1), 0.0, 1.0)
    before = jnp.concatenate([prev_ref[...] * keep_prev, cur[:-1]], axis=0)
    after = jnp.concatenate([cur[1:], next_ref[...] * keep_next], axis=0)
    pc = (before * conv_ref[0:1, :] + cur * conv_ref[1:2, :] + after * conv_ref[2:3, :]).reshape(rows, w)
    W = RWKV_WIDTH
    r, k, v = pc[:, :W], pc[:, W:2 * W], pc[:, 2 * W:3 * W]
    lo = pc[:, 3 * W:3 * W + RWKV_LORA]
    g_lo = pc[:, 3 * W + RWKV_LORA:]
    hsum = hsum_ref[...]
    kk = k * kk_w_ref[...]
    kk = kk * lax.rsqrt(_dot_sum(kk * kk, hsum) + 1e-12)
    wl = _dot_split(jnp.tanh(lo), w2_ref[...])
    al = _dot_split(lo, a2_ref[...])
    k_sum = None
    for d, (d_ref, kd_ref, b_ref) in enumerate(((d0_ref, k0_ref, b0_ref), (d1_ref, k1_ref, b1_ref))):
        z = -(w0_ref[d:d + 1, :] + wl[:, d * W:(d + 1) * W])
        softplus = jnp.maximum(z, 0.0) + jnp.log(1.0 + jnp.exp(-jnp.abs(z)))
        decay = jnp.exp(-jnp.exp(-softplus - 0.5))
        a = _sigmoid(a0_ref[d:d + 1, :] + al[:, d * W:(d + 1) * W])
        k_d = k * (1.0 + (a - 1.0) * ka_ref[...])
        d_ref[...] = decay.reshape(tq, b, W)
        kd_ref[...] = k_d.reshape(tq, b, W)
        b_ref[...] = (kk * a).reshape(tq, b, W)
        k_sum = k_d if k_sum is None else k_sum + k_d
    r_ref[...] = r.reshape(tq, b, W)
    v_ref[...] = v.reshape(tq, b, W)
    kk_ref[...] = kk.reshape(tq, b, W)
    bonus_ref[...] = (_dot_sum(r * k_sum * rk_ref[...], hsum) * v).reshape(tq, b, W)
    gate_ref[...] = _dot(_sigmoid(g_lo), g2_ref[...]).reshape(tq, b, W)


def _rwkv_features(pr, lp, n_ctx_blocks):
    t, b, w = pr.shape
    nb = t // TQ
    W = RWKV_WIDTH
    full = lambda a: pl.BlockSpec(a.shape, lambda i: (0,) * a.ndim)
    consts = (lp['conv'], lp['w0'], lp['a0'], lp['w2'], lp['a2'], lp['g2'], lp['k_k'], lp['k_a'], lp['r_k'], lp['hsum'])
    return pl.pallas_call(
        functools.partial(_rwkv_feat_kernel, n_ctx_blocks),
        grid=(nb,),
        in_specs=[pl.BlockSpec((TQ, b, w), lambda i: (i, 0, 0)),
                  pl.BlockSpec((1, b, w), lambda i: (jnp.maximum(i * TQ - 1, 0), 0, 0)),
                  pl.BlockSpec((1, b, w), lambda i: (jnp.minimum((i + 1) * TQ, t - 1), 0, 0))]
                 + [full(a) for a in consts],
        out_specs=[pl.BlockSpec((TQ, b, W), lambda i: (i, 0, 0))] * 11,
        out_shape=[jax.ShapeDtypeStruct((t, b, W), F32)] * 11,
        compiler_params=_params("arbitrary"),
    )(pr, pr, pr, *consts)


def _rwkv_scan_kernel(rf_ref, vf_ref, kkf_ref, wf_ref, kf_ref, bf_ref, rb_ref, vb_ref, kkb_ref, wb_ref, kb_ref, bb_ref,
                      yf_ref, yb_ref, s_ref, x_ref):
    @pl.when(pl.program_id(0) == 0)
    def _():
        s_ref[...] = jnp.zeros_like(s_ref)

    n = HEAD_DIM
    tc, _, lanes = rf_ref.shape
    R, V, KK, W, K, B, WR = range(7)
    pairs = ((rf_ref, rb_ref), (vf_ref, vb_ref), (kkf_ref, kkb_ref), (wf_ref, wb_ref), (kf_ref, kb_ref), (bf_ref, bb_ref))
    is_fwd = lax.broadcasted_iota(jnp.int32, (n, lanes), 1) < lanes // 2

    def step(t, d_prev):
        tb = tc - 1 - t
        for q, (f_ref, b_ref) in enumerate(pairs):
            x_ref[q] = jnp.where(is_fwd, f_ref[t], b_ref[tb])
        r = x_ref[R]
        b_row, k_row = x_ref[B], x_ref[K]
        br = jnp.sum(b_row * r, axis=0, keepdims=True)
        kr = jnp.sum(k_row * r, axis=0, keepdims=True)
        d_new = d_prev * x_ref[W]
        inv_d = 1.0 / d_new
        x_ref[KK] = x_ref[KK] * d_prev
        x_ref[WR] = d_new * r
        x_ref[B] = b_row * inv_d
        x_ref[K] = k_row * inv_d

        def reduce_pass(jg, acc):
            s_kk, z = acc
            for jj in range(SCAN_UNROLL):
                j = jg * SCAN_UNROLL + jj
                s = s_ref[j]
                s_kk = s_kk + s * x_ref[KK, pl.ds(j, 1), :]
                z = z + s * x_ref[WR, pl.ds(j, 1), :]
            return s_kk, z

        zero = jnp.zeros((n, lanes), F32)
        s_kk, z = lax.fori_loop(0, n // SCAN_UNROLL, reduce_pass, (zero, zero))
        v = x_ref[V]

        def update_pass(jg, c):
            for jj in range(SCAN_UNROLL):
                j = jg * SCAN_UNROLL + jj
                s_ref[j] = s_ref[j] - s_kk * x_ref[B, pl.ds(j, 1), :] + v * x_ref[K, pl.ds(j, 1), :]
            return c

        lax.fori_loop(0, n // SCAN_UNROLL, update_pass, 0)
        y = z - s_kk * br + v * kr
        yf_ref[t] = y
        yb_ref[tb] = y
        return d_new

    x_ref[W] = lax.fori_loop(0, tc, step, jnp.ones((n, lanes), F32))

    def rescale_pass(jg, c):
        for jj in range(SCAN_UNROLL):
            j = jg * SCAN_UNROLL + jj
            s_ref[j] = s_ref[j] * x_ref[W, pl.ds(j, 1), :]
        return c

    lax.fori_loop(0, n // SCAN_UNROLL, rescale_pass, 0)


def _rwkv_scan(streams, n_ctx, tc=SCAN_CHUNK):
    t, _, lanes = streams[0].shape
    nb, ncb = t // tc, n_ctx // tc
    mirror = lambda i: jnp.where(i < ncb, ncb - 1 - i, ncb + nb - 1 - i)
    fspec = pl.BlockSpec((tc, HEAD_DIM, lanes), lambda i: (i, 0, 0))
    bspec = pl.BlockSpec((tc, HEAD_DIM, lanes), lambda i: (mirror(i), 0, 0))
    out = jax.ShapeDtypeStruct((t, HEAD_DIM, lanes), F32)
    return pl.pallas_call(
        _rwkv_scan_kernel,
        grid=(nb,),
        in_specs=[fspec] * 6 + [bspec] * 6,
        out_specs=[fspec, bspec],
        out_shape=[out, out],
        scratch_shapes=[pltpu.VMEM((HEAD_DIM, HEAD_DIM, lanes), F32), pltpu.VMEM((7, HEAD_DIM, lanes), F32)],
        compiler_params=_params("arbitrary"),
    )(*streams, *streams)


def _to_scan_layout(fwd, bwd):
    t, b, _ = fwd.shape
    tr = lambda a: jnp.transpose(a.reshape(t, b, RWKV_HEADS, HEAD_DIM), (0, 3, 1, 2)).reshape(t, HEAD_DIM, b * RWKV_HEADS)
    return jnp.concatenate([tr(fwd), tr(bwd)], axis=-1)


def _from_scan_layout(yf, yb, b):
    t, _, lanes = yf.shape
    y = yf[:, :, :lanes // 2] + yb[:, :, lanes // 2:]
    return jnp.transpose(y.reshape(t, HEAD_DIM, b, RWKV_HEADS), (0, 2, 3, 1)).reshape(t, b, RWKV_WIDTH)


def _s5_disc_kernel(lre_ref, lim_ref, ldt_ref, bre_ref, bim_ref, are_ref, aim_ref, bbre_ref, bbim_ref):
    lre, lim = lre_ref[...], lim_ref[...]
    dt = jnp.exp(ldt_ref[...])
    mag = jnp.exp(lre * dt)
    ab_re = mag * jnp.cos(lim * dt)
    ab_im = mag * jnp.sin(lim * dt)
    nr, ni = ab_re - 1.0, ab_im
    den = lre * lre + lim * lim
    coef_re = (nr * lre + ni * lim) / den
    coef_im = (ni * lre - nr * lim) / den
    are_ref[...] = ab_re
    aim_ref[...] = ab_im
    for c in range(S5_CH):
        bre, bim = bre_ref[c], bim_ref[c]
        bbre_ref[c] = coef_re * bre - coef_im * bim
        bbim_ref[c] = coef_re * bim + coef_im * bre


def _s5_weights(lam_re, lam_im, log_dt, b_re, b_im, c_re, c_im):
    G, P, CH = S5_GROUPS, S5_STATE, S5_CH
    rows = 2 * G
    bt = lambda a: jnp.broadcast_to(jnp.transpose(a, (2, 0, 1))[:, None], (CH, 2, G, P)).reshape(CH, rows, P)
    shp = jax.ShapeDtypeStruct((rows, P), F32)
    shp_b = jax.ShapeDtypeStruct((CH, rows, P), F32)
    a_re, a_im, bb_re, bb_im = pl.pallas_call(
        _s5_disc_kernel, out_shape=[shp, shp, shp_b, shp_b],
    )(lam_re.reshape(rows, P), lam_im.reshape(rows, P), log_dt.reshape(rows, 1), bt(b_re), bt(b_im))
    eye = jnp.eye(G, dtype=F32)

    def in_map(bb):
        bb = jnp.transpose(bb.reshape(CH, 2, G, P), (1, 2, 0, 3))
        return jnp.einsum('dgcp,gh->dgchp', bb, eye).reshape(2, G * CH, G * P)

    w_in = jnp.concatenate([in_map(bb_re), in_map(bb_im)], axis=-1)
    out_map = lambda c: jnp.einsum('gcp,gh->gphc', c, eye).reshape(G * P, G * CH)
    w_out = jnp.concatenate([out_map(c_re), -out_map(c_im)], axis=0)
    return w_in.astype(BF16), a_re.reshape(2, 1, G * P), a_im.reshape(2, 1, G * P), w_out.astype(BF16)


def _s5_scan_kernel(b, u_ref, win_ref, are_ref, aim_ref, wout_ref, y_ref, h_ref, st_ref):
    d = pl.program_id(0)
    rows = u_ref.shape[0]
    tc = rows // b
    n = S5_LANES

    @pl.when(pl.program_id(1) == 0)
    def _():
        st_ref[...] = jnp.zeros_like(st_ref)

    h_ref[...] = _dot(u_ref[...], win_ref[0])
    chunk = 512
    for c in range(n // chunk):
        re = slice(c * chunk, (c + 1) * chunk)
        im = slice(n + c * chunk, n + (c + 1) * chunk)
        a_re = jnp.broadcast_to(are_ref[0, :, re], (b, chunk))
        a_im = jnp.broadcast_to(aim_ref[0, :, re], (b, chunk))

        def step(t, carry):
            h_re, h_im = carry
            t_eff = jnp.where(d == 0, t, tc - 1 - t)
            row = pl.ds(pl.multiple_of(t_eff * b, b), b)
            n_re = a_re * h_re - a_im * h_im + h_ref[row, re]
            n_im = a_re * h_im + a_im * h_re + h_ref[row, im]
            h_ref[row, re] = n_re
            h_ref[row, im] = n_im
            return n_re, n_im

        h_re, h_im = lax.fori_loop(0, tc, step, (st_ref[:, re], st_ref[:, im]))
        st_ref[:, re] = h_re
        st_ref[:, im] = h_im
    y_ref[0] = _dot(h_ref[...], wout_ref[...])


def _s5_scan(ps, weights, n_ctx_blocks, tc=TQ):
    t, b, w = ps.shape
    w_in, a_re, a_im, w_out = weights
    nb = t // tc
    n_lat_blocks = nb - n_ctx_blocks
    rows = tc * b

    def blk(d, j):
        bwd = jnp.where(j < n_ctx_blocks, n_ctx_blocks - 1 - j, n_ctx_blocks + nb - 1 - j)
        return jnp.where(d == 0, j, bwd)

    del n_lat_blocks
    return pl.pallas_call(
        functools.partial(_s5_scan_kernel, b),
        grid=(2, nb),
        in_specs=[pl.BlockSpec((rows, w), lambda d, j: (blk(d, j), 0)),
                  pl.BlockSpec((1, w, 2 * S5_LANES), lambda d, j: (d, 0, 0)),
                  pl.BlockSpec((1, 1, S5_LANES), lambda d, j: (d, 0, 0)),
                  pl.BlockSpec((1, 1, S5_LANES), lambda d, j: (d, 0, 0)),
                  pl.BlockSpec((2 * S5_LANES, w), lambda d, j: (0, 0))],
        out_specs=pl.BlockSpec((1, rows, w), lambda d, j: (d, blk(d, j), 0)),
        out_shape=jax.ShapeDtypeStruct((2, t * b, w), F32),
        scratch_shapes=[pltpu.VMEM((rows, 2 * S5_LANES), F32), pltpu.VMEM((b, 2 * S5_LANES), F32)],
        compiler_params=_params("arbitrary", "arbitrary"),
    )(ps.reshape(t * b, w), w_in, a_re, a_im, w_out)


def _attn_kernel(band, sink_ref, q_ref, *refs):
    if band:
        kp_ref, k0_ref, kn_ref, vp_ref, v0_ref, vn_ref, kc_ref, vc_ref, o_ref, s_ref, p_ref = refs
        keys = jnp.concatenate([kp_ref[...], k0_ref[...], kn_ref[...], kc_ref[...]], axis=0)
        vals = jnp.concatenate([vp_ref[...], v0_ref[...], vn_ref[...], vc_ref[...]], axis=0)
    else:
        kc_ref, vc_ref, o_ref, s_ref, p_ref = refs
        keys, vals = kc_ref[...], vc_ref[...]
    i = pl.program_id(1)
    nq = q_ref.shape[0]
    nk = keys.shape[0]
    q = q_ref[...]
    qk = lambda a, bb: lax.dot_general(a, bb, (((1,), (1,)), ((), ())), preferred_element_type=F32)
    kv_cols = lambda h: slice((h // ATT_GQ) * HEAD_DIM, (h // ATT_GQ + 1) * HEAD_DIM)
    for h in range(ATT_HEADS):
        s_ref[h] = qk(q[:, h * HEAD_DIM:(h + 1) * HEAD_DIM], keys[:, kv_cols(h)])
    if band:
        n_band = nk - kc_ref.shape[0]
        col = lax.broadcasted_iota(jnp.int32, (nq, nk), 1)
        q_pos = i * nq + lax.broadcasted_iota(jnp.int32, (nq, nk), 0)
        k_pos = (i - 1) * nq + col
        n_lat = pl.num_programs(1) * nq
        seen = (col >= n_band) | ((jnp.abs(q_pos - k_pos) <= WINDOW) & (k_pos >= 0) & (k_pos < n_lat))
        bias = jnp.where(seen, 0.0, -jnp.inf)
    inv_den = []
    for h in range(ATT_HEADS):
        s = s_ref[h] + bias if band else s_ref[h]
        sink = sink_ref[h]
        m = jnp.maximum(jnp.max(s, axis=1, keepdims=True), sink)
        e = jnp.exp(s - m)
        inv_den.append(1.0 / (jnp.sum(e, axis=1, keepdims=True) + jnp.exp(sink - m)))
        p_ref[h] = e.astype(BF16)
    outs = [jnp.dot(p_ref[h], vals[:, kv_cols(h)], preferred_element_type=F32) * inv_den[h] for h in range(ATT_HEADS)]
    o_ref[...] = jnp.concatenate(outs, axis=1).astype(o_ref.dtype)


def _attention(q, k, v, sink, n_ctx, band):
    t, b, _ = q.shape
    q2 = q.reshape(t, b * ATT_WIDTH)
    k2 = k.reshape(t, b * KV_WIDTH)
    v2 = v.reshape(t, b * KV_WIDTH)
    cb = n_ctx // ATT_BLOCK
    nq = (t - n_ctx) // ATT_BLOCK if band else cb
    off = cb if band else 0
    qspec = pl.BlockSpec((ATT_BLOCK, ATT_WIDTH), lambda bi, i: (off + i, bi))
    kvspec = lambda f: pl.BlockSpec((ATT_BLOCK, KV_WIDTH), lambda bi, i: (cb + f(i), bi))
    prev = lambda i: jnp.maximum(i - 1, 0)
    own = lambda i: i
    nxt = lambda i: jnp.minimum(i + 1, nq - 1)
    cspec = pl.BlockSpec((n_ctx, KV_WIDTH), lambda bi, i: (0, bi))
    in_specs = [pl.BlockSpec(memory_space=pltpu.SMEM), qspec]
    args = [sink, q2]
    if band:
        in_specs += [kvspec(prev), kvspec(own), kvspec(nxt)] * 2
        args += [k2, k2, k2, v2, v2, v2]
    in_specs += [cspec, cspec]
    args += [k2, v2]
    n_keys = (3 * ATT_BLOCK if band else 0) + n_ctx
    out = pl.pallas_call(
        functools.partial(_attn_kernel, band),
        grid=(b, nq),
        in_specs=in_specs,
        out_specs=pl.BlockSpec((ATT_BLOCK, ATT_WIDTH), lambda bi, i: (i, bi)),
        out_shape=jax.ShapeDtypeStruct((nq * ATT_BLOCK, b * ATT_WIDTH), BF16),
        scratch_shapes=[pltpu.VMEM((ATT_HEADS, ATT_BLOCK, n_keys), F32), pltpu.VMEM((ATT_HEADS, ATT_BLOCK, n_keys), BF16)],
        compiler_params=_params("arbitrary", "arbitrary"),
    )(*args)
    return out.reshape(nq * ATT_BLOCK, b, ATT_WIDTH)


def _gelu_tanh(x):
    return 0.5 * x * (1.0 + jnp.tanh(math.sqrt(2.0 / math.pi) * (x + 0.044715 * (x * x * x))))


def _outproj_kernel(x_ref, gt_ref, yr_ref, bonus_ref, gate_ref, u_ref, ys_ref, ya_ref,
                    hsum_ref, gnw_ref, gnb_ref, dskip_ref, gluw_ref, glub_ref, wo_ref, lng_ref, lnb_ref, o_ref):
    tq, b, d = x_ref.shape
    rows = tq * b
    flat = lambda ref: ref[...].reshape(rows, ref.shape[-1])
    hsum = hsum_ref[...]
    inv_n = 1.0 / HEAD_DIM
    y = flat(yr_ref)
    yc = y - _dot_sum(y, hsum) * inv_n
    var = _dot_sum(yc * yc, hsum) * inv_n
    out_r = (yc * lax.rsqrt(var + GN_EPS) * gnw_ref[...] + gnb_ref[...] + flat(bonus_ref)) * flat(gate_ref)
    z = _gelu_tanh(dskip_ref[...] * flat(u_ref) + ys_ref[0] + ys_ref[1])
    out_s = z * _sigmoid(_dot(z, gluw_ref[...]) + glub_ref[...])
    W = RWKV_WIDTH
    mix = _dot(out_r, wo_ref[0:W, :]) + _dot(out_s, wo_ref[W:W + S5_WIDTH, :]) + _dot(flat(ya_ref), wo_ref[W + S5_WIDTH:, :])
    x = x_ref[...]
    res = DEEPNORM_ALPHA * x + gt_ref[0] * mix.reshape(tq, b, d)
    o_ref[...] = _layer_norm(res, lng_ref[...], lnb_ref[...])


def _output_projection(xt, mod_tab, yr, bonus, gate, u, ys, ya, lp, n_ctx_blocks, first_block):
    t, b, d = xt.shape
    nb = t // TQ - first_block
    seg = lambda i: ((i + first_block) >= n_ctx_blocks).astype(jnp.int32)
    blk = lambda w: pl.BlockSpec((TQ, b, w), lambda i: (i + first_block, 0, 0))
    full = lambda a: pl.BlockSpec(a.shape, lambda i: (0,) * a.ndim)
    consts = (lp['hsum'], lp['gn_w'], lp['gn_b'], lp['s5_d'], lp['glu_w'], lp['glu_b'], lp['w_out'], lp['ln1_g'], lp['ln1_b'])
    return pl.pallas_call(
        _outproj_kernel,
        grid=(nb,),
        in_specs=[blk(d), pl.BlockSpec((1, b, d), lambda i: (seg(i), 0, 2)),
                  blk(RWKV_WIDTH), blk(RWKV_WIDTH), blk(RWKV_WIDTH), blk(S5_WIDTH),
                  pl.BlockSpec((2, TQ * b, S5_WIDTH), lambda i: (0, i + first_block, 0)),
                  pl.BlockSpec((TQ, b, ATT_WIDTH), lambda i: (i, 0, 0))] + [full(a) for a in consts],
        out_specs=pl.BlockSpec((TQ, b, d), lambda i: (i, 0, 0)),
        out_shape=jax.ShapeDtypeStruct((nb * TQ, b, d), F32),
        compiler_params=_params("arbitrary"),
    )(xt, mod_tab, yr, bonus, gate, u, ys, ya, *consts)


def _router_kernel(x_ref, sc_ref, sh_ref, w_ref, b_ref, h_ref, idx_ref, wt_ref):
    tq, b, d = x_ref.shape
    rows = tq * b
    h = (x_ref[...] * (1.0 + sc_ref[0]) + sh_ref[0]).reshape(rows, d)
    for c in range(d // LANES):
        h_ref[pl.ds(c, rows, stride=d // LANES), :] = h[:, c * LANES:(c + 1) * LANES]
    logits = _dot_split(h, w_ref[...]) + b_ref[...]
    lane = lax.broadcasted_iota(jnp.int32, logits.shape, 1)
    neg = -jnp.inf
    first_max = lambda vals, m: jnp.min(jnp.where(vals == m, lane, LANES), axis=1, keepdims=True)
    gl = jnp.where(lane < N_GROUPS, logits, neg)
    g_max = jnp.max(gl, axis=1, keepdims=True)
    g_idx = first_max(gl, g_max)
    g_prob = 1.0 / jnp.sum(jnp.exp(gl - g_max), axis=1, keepdims=True)
    lo = N_GROUPS + EXPERTS_PER_GROUP * g_idx
    el = jnp.where((lane >= lo) & (lane < lo + EXPERTS_PER_GROUP), logits, neg)
    m1 = jnp.max(el, axis=1, keepdims=True)
    i1 = first_max(el, m1)
    el2 = jnp.where(lane == i1, neg, el)
    m2 = jnp.max(el2, axis=1, keepdims=True)
    i2 = first_max(el2, m2)
    e2 = jnp.exp(m2 - m1)
    p1 = 1.0 / (1.0 + e2)
    p2 = e2 / (1.0 + e2)
    idx_ref[...] = jnp.where(lane == 0, i1 - N_GROUPS, jnp.where(lane == 1, i2 - N_GROUPS, 0))
    wt_ref[...] = jnp.where(lane == 0, g_prob * p1, jnp.where(lane == 1, g_prob * p2, 0.0))


def _router(x1, mod_tab, w_r, b_r, n_ctx_blocks, first_block):
    t, b, d = x1.shape
    nb = t // TQ
    seg = lambda i: ((i + first_block) >= n_ctx_blocks).astype(jnp.int32)
    rows = TQ * b
    n = nb * rows
    return pl.pallas_call(
        _router_kernel,
        grid=(nb,),
        in_specs=[pl.BlockSpec((TQ, b, d), lambda i: (i, 0, 0)),
                  pl.BlockSpec((1, b, d), lambda i: (seg(i), 0, 4)),
                  pl.BlockSpec((1, b, d), lambda i: (seg(i), 0, 3)),
                  pl.BlockSpec(w_r.shape, lambda i: (0, 0)),
                  pl.BlockSpec(b_r.shape, lambda i: (0, 0))],
        out_specs=[pl.BlockSpec((rows * (d // LANES), LANES), lambda i: (i, 0)),
                   pl.BlockSpec((rows, LANES), lambda i: (i, 0)),
                   pl.BlockSpec((rows, LANES), lambda i: (i, 0))],
        out_shape=[jax.ShapeDtypeStruct((n * (d // LANES), LANES), F32),
                   jax.ShapeDtypeStruct((n, LANES), jnp.int32),
                   jax.ShapeDtypeStruct((n, LANES), F32)],
        compiler_params=_params("arbitrary"),
    )(x1, mod_tab, mod_tab, w_r, b_r)


def _dispatch_plan(experts, n_tok, slab_rows):
    a = n_tok * 2
    p = -(-a // MOE_BLOCK) * MOE_BLOCK + N_EXPERTS * MOE_BLOCK
    n_blk = p // MOE_BLOCK
    flat_e = experts.reshape(-1)
    skey = jnp.sort(flat_e * a + jnp.arange(a, dtype=jnp.int32))
    counts = jnp.sum((flat_e[:, None] == jnp.arange(N_EXPERTS, dtype=jnp.int32)).astype(jnp.int32), axis=0)
    start = jnp.cumsum(counts) - counts
    padded = (counts + MOE_BLOCK - 1) // MOE_BLOCK * MOE_BLOCK
    pend = jnp.cumsum(padded)
    rows = jnp.arange(p, dtype=jnp.int32)
    row_exp = jnp.minimum(jnp.sum((pend[None, :] <= rows[:, None]).astype(jnp.int32), axis=1), N_EXPERTS - 1)
    rank = rows - (pend - padded)[row_exp]
    valid = (rank < counts[row_exp]) & (rows < pend[-1])
    slab = slab_rows
    inv = skey[jnp.clip(start[row_exp] + rank, 0, a - 1)] - row_exp * a
    src = jnp.where(valid, (inv // 2) * slab, 0)
    spare = (n_tok + jnp.arange(p, dtype=jnp.int32) % (2 * MOE_BLOCK)) * slab
    dst = jnp.where(valid, ((inv % 2) * (n_tok + 2 * MOE_BLOCK) + inv // 2) * slab, spare)
    n_used = pend[-1] // MOE_BLOCK
    blk = jnp.minimum(jnp.arange(n_blk, dtype=jnp.int32), n_used - 1) * MOE_BLOCK
    blk_exp = jnp.minimum(jnp.sum((pend[None, :] <= blk[:, None]).astype(jnp.int32), axis=1), N_EXPERTS - 1)
    return (blk_exp, n_used.reshape(1).astype(jnp.int32),
            src.reshape(n_blk, MOE_BLOCK), dst.reshape(n_blk, MOE_BLOCK), n_blk)


def _expert_kernel(n_tok, blk_exp_ref, n_used_ref, src_ref, dst_ref, h_hbm, wg_ref, wu_ref, wd_ref, y_hbm,
                   xbuf, ybuf, wg_bf, wu_bf, wd_bf, gsem, ssem):
    i = pl.program_id(0)
    n_used = n_used_ref[0]
    slot = i % 2
    slab = xbuf.shape[1] // MOE_BLOCK
    tok_rows = lambda r: pl.ds(pl.multiple_of(r * slab, slab), slab)

    def gather(blk, sl):
        for r in range(MOE_BLOCK):
            src = pl.multiple_of(src_ref[blk, r], slab)
            pltpu.make_async_copy(h_hbm.at[pl.ds(src, slab), :], xbuf.at[sl, tok_rows(r), :], gsem.at[sl]).start()

    def scatter(blk, sl):
        for r in range(MOE_BLOCK):
            dst = pl.multiple_of(dst_ref[blk, r], slab)
            pltpu.make_async_copy(ybuf.at[sl, tok_rows(r), :], y_hbm.at[pl.ds(dst, slab), :], ssem.at[sl]).start()

    wait_gather = lambda sl: pltpu.make_async_copy(xbuf.at[sl], xbuf.at[sl], gsem.at[sl]).wait()
    wait_scatter = lambda sl: pltpu.make_async_copy(ybuf.at[sl], ybuf.at[sl], ssem.at[sl]).wait()

    @pl.when(i == 0)
    def _():
        gather(0, 0)
        ybuf[...] = jnp.zeros_like(ybuf)
        for sl in range(2):
            spare = lambda j: pltpu.make_async_copy(
                ybuf.at[sl], y_hbm.at[pl.ds((j * (n_tok + 2 * MOE_BLOCK) + n_tok + sl * MOE_BLOCK) * slab, MOE_BLOCK * slab), :],
                ssem.at[sl])
            spare(1).start()
            spare(1).wait()
            spare(0).start()

    @pl.when(i < n_used)
    def _():
        wait_gather(slot)
        gather(jnp.minimum(i + 1, n_used - 1), 1 - slot)
        @pl.when((i == 0) | (blk_exp_ref[i] != blk_exp_ref[jnp.maximum(i - 1, 0)]))
        def _():
            wg_bf[...] = wg_ref[0, 0].astype(BF16)
            wu_bf[...] = wu_ref[0, 0].astype(BF16)
            wd_bf[...] = wd_ref[0, 0].astype(BF16)

        x = jnp.concatenate([xbuf[slot, pl.ds(c, MOE_BLOCK, stride=slab), :] for c in range(slab)], axis=1).astype(BF16)
        g = jnp.dot(x, wg_bf[...], preferred_element_type=F32)
        u = jnp.dot(x, wu_bf[...], preferred_element_type=F32)
        act = (g * _sigmoid(g)) * u
        y = jnp.dot(act.astype(BF16), wd_bf[...], preferred_element_type=F32)
        wait_scatter(slot)
        for c in range(slab):
            ybuf[slot, pl.ds(c, MOE_BLOCK, stride=slab), :] = y[:, c * LANES:(c + 1) * LANES]
        scatter(i, slot)

        @pl.when(i == n_used - 1)
        def _():
            wait_gather(1 - slot)
            wait_scatter(slot)
            wait_scatter(1 - slot)


def _routed_experts(h, n_tok, plan, wg, wu, wd, layer):
    d = wg.shape[2]
    slab = d // LANES
    blk_exp, n_used, src, dst, n_blk = plan
    wspec = lambda s: pl.BlockSpec((1, 1) + s, lambda i, be, nu, sr, ds: (layer, be[i], 0, 0))
    y = pl.pallas_call(
        functools.partial(_expert_kernel, n_tok),
        grid_spec=pltpu.PrefetchScalarGridSpec(
            num_scalar_prefetch=4,
            grid=(n_blk,),
            in_specs=[pl.BlockSpec(memory_space=pl.ANY), wspec((d, D_EXPERT)), wspec((d, D_EXPERT)), wspec((D_EXPERT, d))],
            out_specs=pl.BlockSpec(memory_space=pl.ANY),
            scratch_shapes=[pltpu.VMEM((2, MOE_BLOCK * slab, LANES), F32), pltpu.VMEM((2, MOE_BLOCK * slab, LANES), F32),
                            pltpu.VMEM((d, D_EXPERT), BF16), pltpu.VMEM((d, D_EXPERT), BF16), pltpu.VMEM((D_EXPERT, d), BF16),
                            pltpu.SemaphoreType.DMA((2,)), pltpu.SemaphoreType.DMA((2,))]),
        out_shape=jax.ShapeDtypeStruct((2 * (n_tok + 2 * MOE_BLOCK) * slab, LANES), F32),
        compiler_params=_params("arbitrary"),
    )(blk_exp, n_used, src, dst, h, wg, wu, wd)
    return y.reshape(2, (n_tok + 2 * MOE_BLOCK) * slab, LANES)


def _combine_kernel(x_ref, gt_ref, y_ref, wt_ref, lng_ref, lnb_ref, o_ref):
    tq, b, d = x_ref.shape
    rows = tq * b
    slab = d // LANES
    wt = wt_ref[...]
    w0, w1 = wt[:, 0:1], wt[:, 1:2]
    y = jnp.concatenate([y_ref[0, pl.ds(c, rows, stride=slab), :] * w0 + y_ref[1, pl.ds(c, rows, stride=slab), :] * w1
                         for c in range(slab)], axis=1)
    res = DEEPNORM_ALPHA * x_ref[...] + gt_ref[0] * y.reshape(tq, b, d)
    o_ref[...] = _layer_norm(res, lng_ref[...], lnb_ref[...])


def _moe_combine(x1, mod_tab, y, wts, ln_g, ln_b, n_ctx_blocks, first_block):
    t, b, d = x1.shape
    nb = t // TQ
    rows = TQ * b
    seg = lambda i: ((i + first_block) >= n_ctx_blocks).astype(jnp.int32)
    blk = pl.BlockSpec((TQ, b, d), lambda i: (i, 0, 0))
    return pl.pallas_call(
        _combine_kernel,
        grid=(nb,),
        in_specs=[blk, pl.BlockSpec((1, b, d), lambda i: (seg(i), 0, 5)),
                  pl.BlockSpec((2, rows * (d // LANES), LANES), lambda i: (0, i, 0)),
                  pl.BlockSpec((rows, LANES), lambda i: (i, 0)),
                  pl.BlockSpec(ln_g.shape, lambda i: (0, 0)), pl.BlockSpec(ln_b.shape, lambda i: (0, 0))],
        out_specs=blk,
        out_shape=jax.ShapeDtypeStruct((t, b, d), F32),
        compiler_params=_params("arbitrary"),
    )(x1, mod_tab, y, wts, ln_g, ln_b)


def _layer(xt, c_all, lp, rope, n_ctx, last):
    t, b, d = xt.shape
    ncb = n_ctx // TQ
    first = ncb if last else 0
    mod = _modulation(c_all, lp['w_mod'], lp['b_mod'])
    mod_tab = jnp.stack([jnp.broadcast_to(mod[b:b + 1], (b, mod.shape[1])), mod[:b]], axis=0)

    pr, ps, q, k, v = _input_projection(xt, mod_tab, lp['w_in'], rope, ncb)

    r, vv, kk, d0, d1, k0, k1, b0, b1, bonus, gate = _rwkv_features(pr, lp, ncb)
    sl = _to_scan_layout
    yf, yb = _rwkv_scan((sl(r, r), sl(vv, vv), sl(kk, kk), sl(d0, d1), sl(k0, k1), sl(b0, b1)), n_ctx)
    yr = _from_scan_layout(yf, yb, b)

    ys = _s5_scan(ps, lp['s5'], ncb)

    ya = _attention(q, k, v, lp['sink'], n_ctx, True)
    if not last:
        ya = jnp.concatenate([_attention(q, k, v, lp['sink'], n_ctx, False), ya], axis=0)

    x1 = _output_projection(xt, mod_tab, yr, bonus, gate, ps, ys, ya, lp, ncb, first)
    h, idx, wts = _router(x1, mod_tab, lp['w_router'], lp['b_router'], ncb, first)
    n_tok = idx.shape[0]
    plan = _dispatch_plan(idx[:, :2], n_tok, d // LANES)
    y = _routed_experts(h, n_tok, plan, lp['w_gate'], lp['w_up'], lp['w_down'], lp['layer'])
    return _moe_combine(x1, mod_tab, y, wts, lp['ln2_g'], lp['ln2_b'], ncb, first)


def kernel(x, c, ctx, c_ctx, w_mod, b_mod, w_in, rwkv_conv, rwkv_w0, rwkv_w2, rwkv_a0, rwkv_a2, rwkv_g2, rwkv_k_k, rwkv_k_a, rwkv_r_k, rwkv_gn_w, rwkv_gn_b, s5_lam_re, s5_lam_im, s5_log_dt, s5_b_re, s5_b_im, s5_c_re, s5_c_im, s5_d, s5_glu_w, s5_glu_b, attn_sink, w_out, ln1_g, ln1_b, ln2_g, ln2_b, router_group_w, router_group_b, router_expert_w, router_expert_b, expert_w_gate, expert_w_up, expert_w_down):
    bsz, seq, d = x.shape
    n_ctx = ctx.shape[1]
    depth = w_mod.shape[0]
    W = RWKV_WIDTH
    xt = jnp.transpose(jnp.concatenate([ctx, x], axis=1), (1, 0, 2))
    c_all = jnp.zeros((bsz + 8, d), F32).at[:bsz].set(c).at[bsz].set(c_ctx)
    rope = _rope_tables(n_ctx, seq)
    hsum = jnp.kron(jnp.eye(RWKV_HEADS, dtype=F32), jnp.ones((HEAD_DIM, HEAD_DIM), F32)).astype(BF16)
    row = lambda a: a.reshape(1, -1)
    zeros_lo = jnp.zeros((RWKV_LORA // 2, 2 * W), F32)
    for i in range(depth):
        n_r = N_GROUPS + N_EXPERTS
        lp = {
            'w_mod': w_mod[i], 'b_mod': b_mod[i], 'w_in': w_in[i].astype(BF16),
            'conv': rwkv_conv[i], 'w0': rwkv_w0[i], 'a0': rwkv_a0[i],
            'w2': jnp.concatenate([jnp.concatenate([rwkv_w2[i, 0], rwkv_w2[i, 1]], axis=1), zeros_lo], axis=0),
            'a2': jnp.concatenate([zeros_lo, jnp.concatenate([rwkv_a2[i, 0], rwkv_a2[i, 1]], axis=1)], axis=0),
            'g2': rwkv_g2[i].astype(BF16), 'k_k': row(rwkv_k_k[i]), 'k_a': row(rwkv_k_a[i]), 'r_k': row(rwkv_r_k[i]),
            'hsum': hsum, 'gn_w': row(rwkv_gn_w[i]), 'gn_b': row(rwkv_gn_b[i]),
            's5': _s5_weights(s5_lam_re[i], s5_lam_im[i], s5_log_dt[i], s5_b_re[i], s5_b_im[i], s5_c_re[i], s5_c_im[i]),
            's5_d': row(s5_d[i]), 'glu_w': s5_glu_w[i].astype(BF16), 'glu_b': row(s5_glu_b[i]),
            'sink': attn_sink[i], 'w_out': w_out[i].astype(BF16),
            'ln1_g': row(ln1_g[i]), 'ln1_b': row(ln1_b[i]), 'ln2_g': row(ln2_g[i]), 'ln2_b': row(ln2_b[i]),
            'w_router': jnp.zeros((d, LANES), F32).at[:, :N_GROUPS].set(router_group_w[i]).at[:, N_GROUPS:n_r].set(router_expert_w[i]),
            'b_router': jnp.zeros((1, LANES), F32).at[0, :N_GROUPS].set(router_group_b[i]).at[0, N_GROUPS:n_r].set(router_expert_b[i]),
            'w_gate': expert_w_gate, 'w_up': expert_w_up, 'w_down': expert_w_down, 'layer': i,
        }
        xt = _layer(xt, c_all, lp, rope, n_ctx, i == depth - 1)
    return jnp.transpose(xt, (1, 0, 2))
```

```python
import functools
import math

import jax
import jax.numpy as jnp
from jax import lax
from jax.experimental import pallas as pl
from jax.experimental.pallas import tpu as pltpu

F32 = jnp.float32
BF16 = jnp.bfloat16

HEAD_DIM = 64
RWKV_WIDTH = 256
RWKV_HEADS = 4
RWKV_LORA = 128
S5_WIDTH = 256
S5_GROUPS = 16
S5_CH = 16
S5_STATE = 64
S5_LANES = S5_GROUPS * S5_STATE
S5_LANE_CHUNKS = 4
ATT_WIDTH = 512
ATT_HEADS = 8
ATT_KV_HEADS = 2
ATT_GQ = 4
KV_WIDTH = ATT_KV_HEADS * HEAD_DIM
WINDOW = 128
ATT_BLOCK = 128
GRID_W = 64
ROPE_BASE = 10000.0
N_GROUPS = 4
EXPERTS_PER_GROUP = 8
N_EXPERTS = 32
D_EXPERT = 512
MOE_BLOCK = 128
N_MOD = 6
DEPTH = 2
DEEPNORM_ALPHA = (2.0 * DEPTH) ** 0.25
LN_EPS = 1e-5
GN_EPS = 64e-5
LANES = 128
TQ = 32
SCAN_UNROLL = 32
SCAN_CHUNK = 16
VMEM_LIMIT = 48 * 1024 * 1024


def _params(*sem):
    return pltpu.CompilerParams(dimension_semantics=sem, vmem_limit_bytes=VMEM_LIMIT)


def _dot(a, b):
    return jnp.dot(a.astype(BF16), b.astype(BF16), preferred_element_type=F32)


def _dot_hi(a, b):
    return jnp.dot(a, b, precision=lax.Precision.HIGHEST, preferred_element_type=F32)


def _split_bf16(x):
    hi = x.astype(BF16)
    return hi, (x - hi.astype(F32)).astype(BF16)


def _dot_sum(a, ones):
    a_hi, a_lo = _split_bf16(a)
    return jnp.dot(a_hi, ones, preferred_element_type=F32) + jnp.dot(a_lo, ones, preferred_element_type=F32)


def _dot_split(a, b):
    a_hi, a_lo = _split_bf16(a)
    b_hi, b_lo = _split_bf16(b)
    d = lambda x, y: jnp.dot(x, y, preferred_element_type=F32)
    return d(a_hi, b_hi) + d(a_lo, b_hi) + d(a_hi, b_lo)


def _sigmoid(x):
    return 1.0 / (1.0 + jnp.exp(-x))


def _layer_norm(x, g, b):
    mu = jnp.mean(x, axis=-1, keepdims=True)
    xc = x - mu
    var = jnp.mean(xc * xc, axis=-1, keepdims=True)
    return xc * lax.rsqrt(var + LN_EPS) * g + b


def _mod_kernel(c_ref, w_ref, b_ref, o_ref):
    c = c_ref[...]
    o_ref[...] = _dot_hi(c * _sigmoid(c), w_ref[...]) + b_ref[...]


def _modulation(c_all, w_mod, b_mod):
    rows, d = c_all.shape
    n = w_mod.shape[1]
    return pl.pallas_call(
        _mod_kernel,
        grid=(n // d,),
        in_specs=[pl.BlockSpec((rows, d), lambda j: (0, 0)),
                  pl.BlockSpec((d, d), lambda j: (0, j)),
                  pl.BlockSpec((1, d), lambda j: (0, j))],
        out_specs=pl.BlockSpec((rows, d), lambda j: (0, j)),
        out_shape=jax.ShapeDtypeStruct((rows, n), F32),
        compiler_params=_params("arbitrary"),
    )(c_all, w_mod, b_mod.reshape(1, n))


def _swap_halves(t):
    n = t.shape[-1]
    lane = lax.broadcasted_iota(jnp.int32, t.shape, t.ndim - 1)
    first = (lane % HEAD_DIM) < (HEAD_DIM // 2)
    return jnp.where(first, pltpu.roll(t, n - HEAD_DIM // 2, t.ndim - 1), pltpu.roll(t, HEAD_DIM // 2, t.ndim - 1))


def _inproj_kernel(x_ref, sc_ref, sh_ref, w_ref, cq_ref, sq_ref, ck_ref, sk_ref,
                   pr_ref, ps_ref, q_ref, k_ref, v_ref):
    tq, b, d = x_ref.shape
    h = x_ref[...] * (1.0 + sc_ref[0]) + sh_ref[0]
    p = _dot(h.reshape(tq * b, d), w_ref[...])
    o_r = RWKV_WIDTH * 3 + RWKV_LORA + 128
    o_s = o_r + S5_WIDTH
    o_q = o_s + ATT_WIDTH
    o_k = o_q + KV_WIDTH
    pr_ref[...] = p[:, :o_r].reshape(tq, b, o_r)
    ps_ref[...] = p[:, o_r:o_s].reshape(tq, b, S5_WIDTH)
    q = p[:, o_s:o_q]
    k = p[:, o_q:o_k]
    bc = lambda r, w: jnp.broadcast_to(r[...], (tq, b, w)).reshape(tq * b, w)
    q = q * bc(cq_ref, ATT_WIDTH) + _swap_halves(q) * bc(sq_ref, ATT_WIDTH)
    k = k * bc(ck_ref, KV_WIDTH) + _swap_halves(k) * bc(sk_ref, KV_WIDTH)
    q_ref[...] = q.reshape(tq, b, ATT_WIDTH).astype(BF16)
    k_ref[...] = k.reshape(tq, b, KV_WIDTH).astype(BF16)
    v_ref[...] = p[:, o_k:].reshape(tq, b, KV_WIDTH).astype(BF16)


def _input_projection(xt, mod_tab, w_in_bf, rope, n_ctx_blocks):
    t, b, d = xt.shape
    d_in = w_in_bf.shape[1]
    seg = lambda i: (i >= n_ctx_blocks).astype(jnp.int32)
    cq, sq, ck, sk = rope
    widths = (RWKV_WIDTH * 3 + RWKV_LORA + 128, S5_WIDTH, ATT_WIDTH, KV_WIDTH, KV_WIDTH)
    tab = lambda w: pl.BlockSpec((TQ, 1, w), lambda i: (i, 0, 0))
    return pl.pallas_call(
        _inproj_kernel,
        grid=(t // TQ,),
        in_specs=[pl.BlockSpec((TQ, b, d), lambda i: (i, 0, 0)),
                  pl.BlockSpec((1, b, d), lambda i: (seg(i), 0, 1)),
                  pl.BlockSpec((1, b, d), lambda i: (seg(i), 0, 0)),
                  pl.BlockSpec((d, d_in), lambda i: (0, 0)),
                  tab(ATT_WIDTH), tab(ATT_WIDTH), tab(KV_WIDTH), tab(KV_WIDTH)],
        out_specs=[pl.BlockSpec((TQ, b, w), lambda i: (i, 0, 0)) for w in widths],
        out_shape=[jax.ShapeDtypeStruct((t, b, w), dt) for w, dt in zip(widths, (F32, F32, BF16, BF16, BF16))],
        compiler_params=_params("arbitrary"),
    )(xt, mod_tab, mod_tab, w_in_bf, cq, sq, ck, sk)


def _rope_tables(n_ctx, n_lat):
    rows = n_lat // GRID_W
    row_id, col_id = jnp.meshgrid(jnp.arange(rows, dtype=F32), jnp.arange(GRID_W, dtype=F32), indexing='ij')
    n_freq = HEAD_DIM // 4
    inv_freq = ROPE_BASE ** (-jnp.arange(n_freq, dtype=F32) / n_freq)
    ang = jnp.concatenate([row_id.reshape(-1, 1) * inv_freq, col_id.reshape(-1, 1) * inv_freq], axis=-1)
    cos, sin = jnp.cos(ang), jnp.sin(ang)
    cos_h = jnp.concatenate([jnp.ones((n_ctx, HEAD_DIM), F32), jnp.concatenate([cos, cos], axis=-1)], axis=0)
    sin_h = jnp.concatenate([jnp.zeros((n_ctx, HEAD_DIM), F32), jnp.concatenate([-sin, sin], axis=-1)], axis=0)
    scale = HEAD_DIM ** -0.5
    expand = lambda tb, n, s: (jnp.tile(tb, (1, n)) * s)[:, None, :]
    return (expand(cos_h, ATT_HEADS, scale), expand(sin_h, ATT_HEADS, scale),
            expand(cos_h, ATT_KV_HEADS, 1.0), expand(sin_h, ATT_KV_HEADS, 1.0))


def _rwkv_feat_kernel(n_ctx_blocks, cur_ref, prev_ref, next_ref, conv_ref, w0_ref, a0_ref, w2_ref, a2_ref, g2_ref,
                      kk_w_ref, ka_ref, rk_ref, hsum_ref,
                      r_ref, v_ref, kk_ref, d0_ref, d1_ref, k0_ref, k1_ref, b0_ref, b1_ref, bonus_ref, gate_ref):
    i = pl.program_id(0)
    tq, b, w = cur_ref.shape
    rows = tq * b
    cur = cur_ref[...]
    keep_prev = jnp.where((i == 0) | (i == n_ctx_blocks), 0.0, 1.0)
    keep_next = jnp.where((i == n_ctx_blocks - 1) | (i == pl.num_programs(0) - 1), 0.0, 1.0)
    before = jnp.concatenate([prev_ref[...] * keep_prev, cur[:-1]], axis=0)
    after = jnp.concatenate([cur[1:], next_ref[...] * keep_next], axis=0)
    pc = (before * conv_ref[0:1, :] + cur * conv_ref[1:2, :] + after * conv_ref[2:3, :]).reshape(rows, w)
    W = RWKV_WIDTH
    r, k, v = pc[:, :W], pc[:, W:2 * W], pc[:, 2 * W:3 * W]
    lo = pc[:, 3 * W:3 * W + RWKV_LORA]
    g_lo = pc[:, 3 * W + RWKV_LORA:]
    hsum = hsum_ref[...]
    kk = k * kk_w_ref[...]
    kk = kk * lax.rsqrt(_dot_sum(kk * kk, hsum) + 1e-12)
    wl = _dot_split(jnp.tanh(lo), w2_ref[...])
    al = _dot_split(lo, a2_ref[...])
    k_sum = None
    for d, (d_ref, kd_ref, b_ref) in enumerate(((d0_ref, k0_ref, b0_ref), (d1_ref, k1_ref, b1_ref))):
        z = -(w0_ref[d:d + 1, :] + wl[:, d * W:(d + 1) * W])
        softplus = jnp.maximum(z, 0.0) + jnp.log(1.0 + jnp.exp(-jnp.abs(z)))
        decay = jnp.exp(-jnp.exp(-softplus - 0.5))
        a = _sigmoid(a0_ref[d:d + 1, :] + al[:, d * W:(d + 1) * W])
        k_d = k * (1.0 + (a - 1.0) * ka_ref[...])
        d_ref[...] = decay.reshape(tq, b, W)
        kd_ref[...] = k_d.reshape(tq, b, W)
        b_ref[...] = (kk * a).reshape(tq, b, W)
        k_sum = k_d if k_sum is None else k_sum + k_d
    r_ref[...] = r.reshape(tq, b, W)
    v_ref[...] = v.reshape(tq, b, W)
    kk_ref[...] = kk.reshape(tq, b, W)
    bonus_ref[...] = (_dot_sum(r * k_sum * rk_ref[...], hsum) * v).reshape(tq, b, W)
    gate_ref[...] = _dot(_sigmoid(g_lo), g2_ref[...]).reshape(tq, b, W)


def _rwkv_features(pr, lp, n_ctx_blocks):
    t, b, w = pr.shape
    nb = t // TQ
    W = RWKV_WIDTH
    full = lambda a: pl.BlockSpec(a.shape, lambda i: (0,) * a.ndim)
    consts = (lp['conv'], lp['w0'], lp['a0'], lp['w2'], lp['a2'], lp['g2'], lp['k_k'], lp['k_a'], lp['r_k'], lp['hsum'])
    return pl.pallas_call(
        functools.partial(_rwkv_feat_kernel, n_ctx_blocks),
        grid=(nb,),
        in_specs=[pl.BlockSpec((TQ, b, w), lambda i: (i, 0, 0)),
                  pl.BlockSpec((1, b, w), lambda i: (jnp.maximum(i * TQ - 1, 0), 0, 0)),
                  pl.BlockSpec((1, b, w), lambda i: (jnp.minimum((i + 1) * TQ, t - 1), 0, 0))]
                 + [full(a) for a in consts],
        out_specs=[pl.BlockSpec((TQ, b, W), lambda i: (i, 0, 0))] * 11,
        out_shape=[jax.ShapeDtypeStruct((t, b, W), F32)] * 11,
        compiler_params=_params("arbitrary"),
    )(pr, pr, pr, *consts)


def _rwkv_scan_kernel(rf_ref, vf_ref, kkf_ref, wf_ref, kf_ref, bf_ref, rb_ref, vb_ref, kkb_ref, wb_ref, kb_ref, bb_ref,
                      yf_ref, yb_ref, s_ref, x_ref):
    @pl.when(pl.program_id(0) == 0)
    def _():
        s_ref[...] = jnp.zeros_like(s_ref)

    n = HEAD_DIM
    tc, _, lanes = rf_ref.shape
    R, V, KK, W, K, B, WR = range(7)
    pairs = ((rf_ref, rb_ref), (vf_ref, vb_ref), (kkf_ref, kkb_ref), (wf_ref, wb_ref), (kf_ref, kb_ref), (bf_ref, bb_ref))
    is_fwd = lax.broadcasted_iota(jnp.int32, (n, lanes), 1) < lanes // 2

    def step(t, d_prev):
        tb = tc - 1 - t
        for q, (f_ref, b_ref) in enumerate(pairs):
            x_ref[q] = jnp.where(is_fwd, f_ref[t], b_ref[tb])
        r = x_ref[R]
        b_row, k_row = x_ref[B], x_ref[K]
        br = jnp.sum(b_row * r, axis=0, keepdims=True)
        kr = jnp.sum(k_row * r, axis=0, keepdims=True)
        d_new = d_prev * x_ref[W]
        inv_d = 1.0 / d_new
        x_ref[KK] = x_ref[KK] * d_prev
        x_ref[WR] = d_new * r
        x_ref[B] = b_row * inv_d
        x_ref[K] = k_row * inv_d

        def reduce_pass(jg, acc):
            s_kk, z = acc
            for jj in range(SCAN_UNROLL):
                j = jg * SCAN_UNROLL + jj
                s = s_ref[j]
                s_kk = s_kk + s * x_ref[KK, pl.ds(j, 1), :]
                z = z + s * x_ref[WR, pl.ds(j, 1), :]
            return s_kk, z

        zero = jnp.zeros((n, lanes), F32)
        s_kk, z = lax.fori_loop(0, n // SCAN_UNROLL, reduce_pass, (zero, zero))
        v = x_ref[V]

        def update_pass(jg, c):
            for jj in range(SCAN_UNROLL):
                j = jg * SCAN_UNROLL + jj
                s_ref[j] = s_ref[j] - s_kk * x_ref[B, pl.ds(j, 1), :] + v * x_ref[K, pl.ds(j, 1), :]
            return c

        lax.fori_loop(0, n // SCAN_UNROLL, update_pass, 0)
        y = z - s_kk * br + v * kr
        yf_ref[t] = y
        yb_ref[tb] = y
        return d_new

    x_ref[W] = lax.fori_loop(0, tc, step, jnp.ones((n, lanes), F32))

    def rescale_pass(jg, c):
        for jj in range(SCAN_UNROLL):
            j = jg * SCAN_UNROLL + jj
            s_ref[j] = s_ref[j] * x_ref[W, pl.ds(j, 1), :]
        return c

    lax.fori_loop(0, n // SCAN_UNROLL, rescale_pass, 0)


def _rwkv_scan(streams, n_ctx, tc=SCAN_CHUNK):
    t, _, lanes = streams[0].shape
    nb, ncb = t // tc, n_ctx // tc
    mirror = lambda i: jnp.where(i < ncb, ncb - 1 - i, ncb + nb - 1 - i)
    fspec = pl.BlockSpec((tc, HEAD_DIM, lanes), lambda i: (i, 0, 0))
    bspec = pl.BlockSpec((tc, HEAD_DIM, lanes), lambda i: (mirror(i), 0, 0))
    out = jax.ShapeDtypeStruct((t, HEAD_DIM, lanes), F32)
    return pl.pallas_call(
        _rwkv_scan_kernel,
        grid=(nb,),
        in_specs=[fspec] * 6 + [bspec] * 6,
        out_specs=[fspec, bspec],
        out_shape=[out, out],
        scratch_shapes=[pltpu.VMEM((HEAD_DIM, HEAD_DIM, lanes), F32), pltpu.VMEM((7, HEAD_DIM, lanes), F32)],
        compiler_params=_params("arbitrary"),
    )(*streams, *streams)


def _to_scan_layout(fwd, bwd):
    t, b, _ = fwd.shape
    tr = lambda a: jnp.transpose(a.reshape(t, b, RWKV_HEADS, HEAD_DIM), (0, 3, 1, 2)).reshape(t, HEAD_DIM, b * RWKV_HEADS)
    return jnp.concatenate([tr(fwd), tr(bwd)], axis=-1)


def _from_scan_layout(yf, yb, b):
    t, _, lanes = yf.shape
    y = yf[:, :, :lanes // 2] + yb[:, :, lanes // 2:]
    return jnp.transpose(y.reshape(t, HEAD_DIM, b, RWKV_HEADS), (0, 2, 3, 1)).reshape(t, b, RWKV_WIDTH)


def _s5_disc_kernel(lre_ref, lim_ref, ldt_ref, bre_ref, bim_ref, are_ref, aim_ref, bbre_ref, bbim_ref):
    lre, lim = lre_ref[...], lim_ref[...]
    dt = jnp.exp(ldt_ref[...])
    mag = jnp.exp(lre * dt)
    ab_re = mag * jnp.cos(lim * dt)
    ab_im = mag * jnp.sin(lim * dt)
    nr, ni = ab_re - 1.0, ab_im
    den = lre * lre + lim * lim
    coef_re = (nr * lre + ni * lim) / den
    coef_im = (ni * lre - nr * lim) / den
    are_ref[...] = ab_re
    aim_ref[...] = ab_im
    for c in range(S5_CH):
        bre, bim = bre_ref[c], bim_ref[c]
        bbre_ref[c] = coef_re * bre - coef_im * bim
        bbim_ref[c] = coef_re * bim + coef_im * bre


def _s5_weights(lam_re, lam_im, log_dt, b_re, b_im, c_re, c_im):
    G, P, CH = S5_GROUPS, S5_STATE, S5_CH
    rows = 2 * G
    bt = lambda a: jnp.broadcast_to(jnp.transpose(a, (2, 0, 1))[:, None], (CH, 2, G, P)).reshape(CH, rows, P)
    shp = jax.ShapeDtypeStruct((rows, P), F32)
    shp_b = jax.ShapeDtypeStruct((CH, rows, P), F32)
    a_re, a_im, bb_re, bb_im = pl.pallas_call(
        _s5_disc_kernel, out_shape=[shp, shp, shp_b, shp_b],
    )(lam_re.reshape(rows, P), lam_im.reshape(rows, P), log_dt.reshape(rows, 1), bt(b_re), bt(b_im))
    eye = jnp.eye(G, dtype=F32)

    def in_map(bb):
        bb = jnp.transpose(bb.reshape(CH, 2, G, P), (1, 2, 0, 3))
        return jnp.einsum('dgcp,gh->dgchp', bb, eye).reshape(2, G * CH, G * P)

    w_in = jnp.concatenate([in_map(bb_re), in_map(bb_im)], axis=-1)
    out_map = lambda c: jnp.einsum('gcp,gh->gphc', c, eye).reshape(G * P, G * CH)
    w_out = jnp.concatenate([out_map(c_re), -out_map(c_im)], axis=0)
    return w_in.astype(BF16), a_re.reshape(2, 1, G * P), a_im.reshape(2, 1, G * P), w_out.astype(BF16)


def _s5_scan_kernel(b, u_ref, win_ref, are_ref, aim_ref, wout_ref, y_ref, st_ref, *h_refs):
    d = pl.program_id(0)
    rows = u_ref.shape[0]
    tc = rows // b
    n = S5_LANES
    chunk = n // len(h_refs)

    @pl.when(pl.program_id(1) == 0)
    def _():
        st_ref[...] = jnp.zeros_like(st_ref)

    u = u_ref[...].astype(BF16)
    y = None
    for c, h_ref in enumerate(h_refs):
        re = slice(c * chunk, (c + 1) * chunk)
        im = slice(n + c * chunk, n + (c + 1) * chunk)
        h_ref[:, :chunk] = jnp.dot(u, win_ref[0, :, re], preferred_element_type=F32)
        h_ref[:, chunk:] = jnp.dot(u, win_ref[0, :, im], preferred_element_type=F32)
        a_re = jnp.broadcast_to(are_ref[0, :, re], (b, chunk))
        a_im = jnp.broadcast_to(aim_ref[0, :, re], (b, chunk))
        h_re, h_im = st_ref[:, re], st_ref[:, im]
        for t in range(tc):
            t_eff = jnp.where(d == 0, t, tc - 1 - t)
            row = pl.ds(pl.multiple_of(t_eff * b, b), b)
            n_re = a_re * h_re - a_im * h_im + h_ref[row, :chunk]
            n_im = a_re * h_im + a_im * h_re + h_ref[row, chunk:]
            h_ref[row, :chunk] = n_re
            h_ref[row, chunk:] = n_im
            h_re, h_im = n_re, n_im
        st_ref[:, re] = h_re
        st_ref[:, im] = h_im
        part = (jnp.dot(h_ref[:, :chunk].astype(BF16), wout_ref[re, :], preferred_element_type=F32)
                + jnp.dot(h_ref[:, chunk:].astype(BF16), wout_ref[im, :], preferred_element_type=F32))
        y = part if y is None else y + part
    y_ref[0] = y


def _s5_scan(ps, weights, n_ctx_blocks, tc=TQ):
    t, b, w = ps.shape
    w_in, a_re, a_im, w_out = weights
    nb = t // tc
    n_lat_blocks = nb - n_ctx_blocks
    rows = tc * b

    def blk(d, j):
        bwd = jnp.where(j < n_ctx_blocks, n_ctx_blocks - 1 - j, n_ctx_blocks + nb - 1 - j)
        return jnp.where(d == 0, j, bwd)

    del n_lat_blocks
    return pl.pallas_call(
        functools.partial(_s5_scan_kernel, b),
        grid=(2, nb),
        in_specs=[pl.BlockSpec((rows, w), lambda d, j: (blk(d, j), 0)),
                  pl.BlockSpec((1, w, 2 * S5_LANES), lambda d, j: (d, 0, 0)),
                  pl.BlockSpec((1, 1, S5_LANES), lambda d, j: (d, 0, 0)),
                  pl.BlockSpec((1, 1, S5_LANES), lambda d, j: (d, 0, 0)),
                  pl.BlockSpec((2 * S5_LANES, w), lambda d, j: (0, 0))],
        out_specs=pl.BlockSpec((1, rows, w), lambda d, j: (d, blk(d, j), 0)),
        out_shape=jax.ShapeDtypeStruct((2, t * b, w), F32),
        scratch_shapes=[pltpu.VMEM((b, 2 * S5_LANES), F32)]
                       + [pltpu.VMEM((rows, 2 * S5_LANES // S5_LANE_CHUNKS), F32)] * S5_LANE_CHUNKS,
        compiler_params=_params("arbitrary", "arbitrary"),
    )(ps.reshape(t * b, w), w_in, a_re, a_im, w_out)


def _attn_kernel(band, sink_ref, q_ref, *refs):
    if band:
        kp_ref, k0_ref, kn_ref, vp_ref, v0_ref, vn_ref, kc_ref, vc_ref, o_ref, s_ref, p_ref = refs
        keys = jnp.concatenate([kp_ref[...], k0_ref[...], kn_ref[...], kc_ref[...]], axis=0)
        vals = jnp.concatenate([vp_ref[...], v0_ref[...], vn_ref[...], vc_ref[...]], axis=0)
    else:
        kc_ref, vc_ref, o_ref, s_ref, p_ref = refs
        keys, vals = kc_ref[...], vc_ref[...]
    i = pl.program_id(1)
    nq = q_ref.shape[0]
    nk = keys.shape[0]
    q = q_ref[...]
    qk = lambda a, bb: lax.dot_general(a, bb, (((1,), (1,)), ((), ())), preferred_element_type=F32)
    kv_cols = lambda h: slice((h // ATT_GQ) * HEAD_DIM, (h // ATT_GQ + 1) * HEAD_DIM)
    for h in range(ATT_HEADS):
        s_ref[h] = qk(q[:, h * HEAD_DIM:(h + 1) * HEAD_DIM], keys[:, kv_cols(h)])
    if band:
        n_band = nk - kc_ref.shape[0]
        col = lax.broadcasted_iota(jnp.int32, (nq, nk), 1)
        q_pos = i * nq + lax.broadcasted_iota(jnp.int32, (nq, nk), 0)
        k_pos = (i - 1) * nq + col
        n_lat = pl.num_programs(1) * nq
        seen = (col >= n_band) | ((jnp.abs(q_pos - k_pos) <= WINDOW) & (k_pos >= 0) & (k_pos < n_lat))
        bias = jnp.where(seen, 0.0, -jnp.inf)
    inv_den = []
    for h in range(ATT_HEADS):
        s = s_ref[h] + bias if band else s_ref[h]
        sink = sink_ref[h]
        m = jnp.maximum(jnp.max(s, axis=1, keepdims=True), sink)
        e = jnp.exp(s - m)
        inv_den.append(1.0 / (jnp.sum(e, axis=1, keepdims=True) + jnp.exp(sink - m)))
        p_ref[h] = e.astype(BF16)
    outs = [jnp.dot(p_ref[h], vals[:, kv_cols(h)], preferred_element_type=F32) * inv_den[h] for h in range(ATT_HEADS)]
    o_ref[...] = jnp.concatenate(outs, axis=1).astype(o_ref.dtype)


def _attention(q, k, v, sink, n_ctx, band):
    t, b, _ = q.shape
    q2 = q.reshape(t, b * ATT_WIDTH)
    k2 = k.reshape(t, b * KV_WIDTH)
    v2 = v.reshape(t, b * KV_WIDTH)
    cb = n_ctx // ATT_BLOCK
    nq = (t - n_ctx) // ATT_BLOCK if band else cb
    off = cb if band else 0
    qspec = pl.BlockSpec((ATT_BLOCK, ATT_WIDTH), lambda bi, i: (off + i, bi))
    kvspec = lambda f: pl.BlockSpec((ATT_BLOCK, KV_WIDTH), lambda bi, i: (cb + f(i), bi))
    prev = lambda i: jnp.maximum(i - 1, 0)
    own = lambda i: i
    nxt = lambda i: jnp.minimum(i + 1, nq - 1)
    cspec = pl.BlockSpec((n_ctx, KV_WIDTH), lambda bi, i: (0, bi))
    in_specs = [pl.BlockSpec(memory_space=pltpu.SMEM), qspec]
    args = [sink, q2]
    if band:
        in_specs += [kvspec(prev), kvspec(own), kvspec(nxt)] * 2
        args += [k2, k2, k2, v2, v2, v2]
    in_specs += [cspec, cspec]
    args += [k2, v2]
    n_keys = (3 * ATT_BLOCK if band else 0) + n_ctx
    out = pl.pallas_call(
        functools.partial(_attn_kernel, band),
        grid=(b, nq),
        in_specs=in_specs,
        out_specs=pl.BlockSpec((ATT_BLOCK, ATT_WIDTH), lambda bi, i: (i, bi)),
        out_shape=jax.ShapeDtypeStruct((nq * ATT_BLOCK, b * ATT_WIDTH), BF16),
        scratch_shapes=[pltpu.VMEM((ATT_HEADS, ATT_BLOCK, n_keys), F32), pltpu.VMEM((ATT_HEADS, ATT_BLOCK, n_keys), BF16)],
        compiler_params=_params("arbitrary", "arbitrary"),
    )(*args)
    return out.reshape(nq * ATT_BLOCK, b, ATT_WIDTH)


def _gelu_tanh(x):
    return 0.5 * x * (1.0 + jnp.tanh(math.sqrt(2.0 / math.pi) * (x + 0.044715 * (x * x * x))))


def _outproj_kernel(x_ref, gt_ref, yr_ref, bonus_ref, gate_ref, u_ref, ys_ref, ya_ref,
                    hsum_ref, gnw_ref, gnb_ref, dskip_ref, gluw_ref, glub_ref, wo_ref, lng_ref, lnb_ref, o_ref):
    tq, b, d = x_ref.shape
    rows = tq * b
    flat = lambda ref: ref[...].reshape(rows, ref.shape[-1])
    hsum = hsum_ref[...]
    inv_n = 1.0 / HEAD_DIM
    y = flat(yr_ref)
    yc = y - _dot_sum(y, hsum) * inv_n
    var = _dot_sum(yc * yc, hsum) * inv_n
    out_r = (yc * lax.rsqrt(var + GN_EPS) * gnw_ref[...] + gnb_ref[...] + flat(bonus_ref)) * flat(gate_ref)
    z = _gelu_tanh(dskip_ref[...] * flat(u_ref) + ys_ref[0] + ys_ref[1])
    out_s = z * _sigmoid(_dot(z, gluw_ref[...]) + glub_ref[...])
    W = RWKV_WIDTH
    mix = _dot(out_r, wo_ref[0:W, :]) + _dot(out_s, wo_ref[W:W + S5_WIDTH, :]) + _dot(flat(ya_ref), wo_ref[W + S5_WIDTH:, :])
    x = x_ref[...]
    res = DEEPNORM_ALPHA * x + gt_ref[0] * mix.reshape(tq, b, d)
    o_ref[...] = _layer_norm(res, lng_ref[...], lnb_ref[...])


def _output_projection(xt, mod_tab, yr, bonus, gate, u, ys, ya, lp, n_ctx_blocks, first_block):
    t, b, d = xt.shape
    nb = t // TQ - first_block
    seg = lambda i: ((i + first_block) >= n_ctx_blocks).astype(jnp.int32)
    blk = lambda w: pl.BlockSpec((TQ, b, w), lambda i: (i + first_block, 0, 0))
    full = lambda a: pl.BlockSpec(a.shape, lambda i: (0,) * a.ndim)
    consts = (lp['hsum'], lp['gn_w'], lp['gn_b'], lp['s5_d'], lp['glu_w'], lp['glu_b'], lp['w_out'], lp['ln1_g'], lp['ln1_b'])
    return pl.pallas_call(
        _outproj_kernel,
        grid=(nb,),
        in_specs=[blk(d), pl.BlockSpec((1, b, d), lambda i: (seg(i), 0, 2)),
                  blk(RWKV_WIDTH), blk(RWKV_WIDTH), blk(RWKV_WIDTH), blk(S5_WIDTH),
                  pl.BlockSpec((2, TQ * b, S5_WIDTH), lambda i: (0, i + first_block, 0)),
                  pl.BlockSpec((TQ, b, ATT_WIDTH), lambda i: (i, 0, 0))] + [full(a) for a in consts],
        out_specs=pl.BlockSpec((TQ, b, d), lambda i: (i, 0, 0)),
        out_shape=jax.ShapeDtypeStruct((nb * TQ, b, d), F32),
        compiler_params=_params("arbitrary"),
    )(xt, mod_tab, yr, bonus, gate, u, ys, ya, *consts)


def _router_kernel(x_ref, sc_ref, sh_ref, w_ref, b_ref, h_ref, idx_ref, wt_ref):
    tq, b, d = x_ref.shape
    rows = tq * b
    h = (x_ref[...] * (1.0 + sc_ref[0]) + sh_ref[0]).reshape(rows, d)
    for c in range(d // LANES):
        h_ref[pl.ds(c, rows, stride=d // LANES), :] = h[:, c * LANES:(c + 1) * LANES]
    logits = _dot_split(h, w_ref[...]) + b_ref[...]
    lane = lax.broadcasted_iota(jnp.int32, logits.shape, 1)
    neg = -jnp.inf
    first_max = lambda vals, m: jnp.min(jnp.where(vals == m, lane, LANES), axis=1, keepdims=True)
    gl = jnp.where(lane < N_GROUPS, logits, neg)
    g_max = jnp.max(gl, axis=1, keepdims=True)
    g_idx = first_max(gl, g_max)
    g_prob = 1.0 / jnp.sum(jnp.exp(gl - g_max), axis=1, keepdims=True)
    lo = N_GROUPS + EXPERTS_PER_GROUP * g_idx
    el = jnp.where((lane >= lo) & (lane < lo + EXPERTS_PER_GROUP), logits, neg)
    m1 = jnp.max(el, axis=1, keepdims=True)
    i1 = first_max(el, m1)
    el2 = jnp.where(lane == i1, neg, el)
    m2 = jnp.max(el2, axis=1, keepdims=True)
    i2 = first_max(el2, m2)
    e2 = jnp.exp(m2 - m1)
    p1 = 1.0 / (1.0 + e2)
    p2 = e2 / (1.0 + e2)
    idx_ref[...] = jnp.where(lane == 0, i1 - N_GROUPS, jnp.where(lane == 1, i2 - N_GROUPS, 0))
    wt_ref[...] = jnp.where(lane == 0, g_prob * p1, jnp.where(lane == 1, g_prob * p2, 0.0))


def _router(x1, mod_tab, w_r, b_r, n_ctx_blocks, first_block):
    t, b, d = x1.shape
    nb = t // TQ
    seg = lambda i: ((i + first_block) >= n_ctx_blocks).astype(jnp.int32)
    rows = TQ * b
    n = nb * rows
    return pl.pallas_call(
        _router_kernel,
        grid=(nb,),
        in_specs=[pl.BlockSpec((TQ, b, d), lambda i: (i, 0, 0)),
                  pl.BlockSpec((1, b, d), lambda i: (seg(i), 0, 4)),
                  pl.BlockSpec((1, b, d), lambda i: (seg(i), 0, 3)),
                  pl.BlockSpec(w_r.shape, lambda i: (0, 0)),
                  pl.BlockSpec(b_r.shape, lambda i: (0, 0))],
        out_specs=[pl.BlockSpec((rows * (d // LANES), LANES), lambda i: (i, 0)),
                   pl.BlockSpec((rows, LANES), lambda i: (i, 0)),
                   pl.BlockSpec((rows, LANES), lambda i: (i, 0))],
        out_shape=[jax.ShapeDtypeStruct((n * (d // LANES), LANES), F32),
                   jax.ShapeDtypeStruct((n, LANES), jnp.int32),
                   jax.ShapeDtypeStruct((n, LANES), F32)],
        compiler_params=_params("arbitrary"),
    )(x1, mod_tab, mod_tab, w_r, b_r)


def _dispatch_plan(experts, n_tok, slab_rows):
    a = n_tok * 2
    p = -(-a // MOE_BLOCK) * MOE_BLOCK + N_EXPERTS * MOE_BLOCK
    n_blk = p // MOE_BLOCK
    flat_e = experts.reshape(-1)
    skey = jnp.sort(flat_e * a + jnp.arange(a, dtype=jnp.int32))
    counts = jnp.sum((flat_e[:, None] == jnp.arange(N_EXPERTS, dtype=jnp.int32)).astype(jnp.int32), axis=0)
    start = jnp.cumsum(counts) - counts
    padded = (counts + MOE_BLOCK - 1) // MOE_BLOCK * MOE_BLOCK
    pend = jnp.cumsum(padded)
    rows = jnp.arange(p, dtype=jnp.int32)
    row_exp = jnp.minimum(jnp.sum((pend[None, :] <= rows[:, None]).astype(jnp.int32), axis=1), N_EXPERTS - 1)
    rank = rows - (pend - padded)[row_exp]
    valid = (rank < counts[row_exp]) & (rows < pend[-1])
    slab = slab_rows
    inv = skey[jnp.clip(start[row_exp] + rank, 0, a - 1)] - row_exp * a
    src = jnp.where(valid, (inv // 2) * slab, 0)
    spare = (n_tok + jnp.arange(p, dtype=jnp.int32) % (2 * MOE_BLOCK)) * slab
    dst = jnp.where(valid, ((inv % 2) * (n_tok + 2 * MOE_BLOCK) + inv // 2) * slab, spare)
    n_used = pend[-1] // MOE_BLOCK
    blk = jnp.minimum(jnp.arange(n_blk, dtype=jnp.int32), n_used - 1) * MOE_BLOCK
    blk_exp = jnp.minimum(jnp.sum((pend[None, :] <= blk[:, None]).astype(jnp.int32), axis=1), N_EXPERTS - 1)
    return (blk_exp, n_used.reshape(1).astype(jnp.int32),
            src.reshape(n_blk, MOE_BLOCK), dst.reshape(n_blk, MOE_BLOCK), n_blk)


def _expert_kernel(n_tok, blk_exp_ref, n_used_ref, src_ref, dst_ref, h_hbm, wg_ref, wu_ref, wd_ref, y_hbm,
                   xbuf, ybuf, wg_bf, wu_bf, wd_bf, gsem, ssem):
    i = pl.program_id(0)
    n_used = n_used_ref[0]
    slot = i % 2
    slab = xbuf.shape[1] // MOE_BLOCK
    tok_rows = lambda r: pl.ds(pl.multiple_of(r * slab, slab), slab)

    def gather(blk, sl):
        for r in range(MOE_BLOCK):
            src = pl.multiple_of(src_ref[blk, r], slab)
            pltpu.make_async_copy(h_hbm.at[pl.ds(src, slab), :], xbuf.at[sl, tok_rows(r), :], gsem.at[sl]).start()

    def scatter(blk, sl):
        for r in range(MOE_BLOCK):
            dst = pl.multiple_of(dst_ref[blk, r], slab)
            pltpu.make_async_copy(ybuf.at[sl, tok_rows(r), :], y_hbm.at[pl.ds(dst, slab), :], ssem.at[sl]).start()

    wait_gather = lambda sl: pltpu.make_async_copy(xbuf.at[sl], xbuf.at[sl], gsem.at[sl]).wait()
    wait_scatter = lambda sl: pltpu.make_async_copy(ybuf.at[sl], ybuf.at[sl], ssem.at[sl]).wait()

    @pl.when(i == 0)
    def _():
        gather(0, 0)
        ybuf[...] = jnp.zeros_like(ybuf)
        for sl in range(2):
            spare = lambda j: pltpu.make_async_copy(
                ybuf.at[sl], y_hbm.at[pl.ds((j * (n_tok + 2 * MOE_BLOCK) + n_tok + sl * MOE_BLOCK) * slab, MOE_BLOCK * slab), :],
                ssem.at[sl])
            spare(1).start()
            spare(1).wait()
            spare(0).start()

    @pl.when(i < n_used)
    def _():
        wait_gather(slot)
        gather(jnp.minimum(i + 1, n_used - 1), 1 - slot)
        @pl.when((i == 0) | (blk_exp_ref[i] != blk_exp_ref[jnp.maximum(i - 1, 0)]))
        def _():
            wg_bf[...] = wg_ref[0, 0].astype(BF16)
            wu_bf[...] = wu_ref[0, 0].astype(BF16)
            wd_bf[...] = wd_ref[0, 0].astype(BF16)

        x = jnp.concatenate([xbuf[slot, pl.ds(c, MOE_BLOCK, stride=slab), :] for c in range(slab)], axis=1).astype(BF16)
        g = jnp.dot(x, wg_bf[...], preferred_element_type=F32)
        u = jnp.dot(x, wu_bf[...], preferred_element_type=F32)
        act = (g * _sigmoid(g)) * u
        y = jnp.dot(act.astype(BF16), wd_bf[...], preferred_element_type=F32)
        wait_scatter(slot)
        for c in range(slab):
            ybuf[slot, pl.ds(c, MOE_BLOCK, stride=slab), :] = y[:, c * LANES:(c + 1) * LANES]
        scatter(i, slot)

        @pl.when(i == n_used - 1)
        def _():
            wait_gather(1 - slot)
            wait_scatter(slot)
            wait_scatter(1 - slot)


def _routed_experts(h, n_tok, plan, wg, wu, wd, layer):
    d = wg.shape[2]
    slab = d // LANES
    blk_exp, n_used, src, dst, n_blk = plan
    wspec = lambda s: pl.BlockSpec((1, 1) + s, lambda i, be, nu, sr, ds: (layer, be[i], 0, 0))
    y = pl.pallas_call(
        functools.partial(_expert_kernel, n_tok),
        grid_spec=pltpu.PrefetchScalarGridSpec(
            num_scalar_prefetch=4,
            grid=(n_blk,),
            in_specs=[pl.BlockSpec(memory_space=pl.ANY), wspec((d, D_EXPERT)), wspec((d, D_EXPERT)), wspec((D_EXPERT, d))],
            out_specs=pl.BlockSpec(memory_space=pl.ANY),
            scratch_shapes=[pltpu.VMEM((2, MOE_BLOCK * slab, LANES), F32), pltpu.VMEM((2, MOE_BLOCK * slab, LANES), F32),
                            pltpu.VMEM((d, D_EXPERT), BF16), pltpu.VMEM((d, D_EXPERT), BF16), pltpu.VMEM((D_EXPERT, d), BF16),
                            pltpu.SemaphoreType.DMA((2,)), pltpu.SemaphoreType.DMA((2,))]),
        out_shape=jax.ShapeDtypeStruct((2 * (n_tok + 2 * MOE_BLOCK) * slab, LANES), F32),
        compiler_params=_params("arbitrary"),
    )(blk_exp, n_used, src, dst, h, wg, wu, wd)
    return y.reshape(2, (n_tok + 2 * MOE_BLOCK) * slab, LANES)


def _combine_kernel(x_ref, gt_ref, y_ref, wt_ref, lng_ref, lnb_ref, o_ref):
    tq, b, d = x_ref.shape
    rows = tq * b
    slab = d // LANES
    wt = wt_ref[...]
    w0, w1 = wt[:, 0:1], wt[:, 1:2]
    y = jnp.concatenate([y_ref[0, pl.ds(c, rows, stride=slab), :] * w0 + y_ref[1, pl.ds(c, rows, stride=slab), :] * w1
                         for c in range(slab)], axis=1)
    res = DEEPNORM_ALPHA * x_ref[...] + gt_ref[0] * y.reshape(tq, b, d)
    o_ref[...] = _layer_norm(res, lng_ref[...], lnb_ref[...])


def _moe_combine(x1, mod_tab, y, wts, ln_g, ln_b, n_ctx_blocks, first_block):
    t, b, d = x1.shape
    nb = t // TQ
    rows = TQ * b
    seg = lambda i: ((i + first_block) >= n_ctx_blocks).astype(jnp.int32)
    blk = pl.BlockSpec((TQ, b, d), lambda i: (i, 0, 0))
    return pl.pallas_call(
        _combine_kernel,
        grid=(nb,),
        in_specs=[blk, pl.BlockSpec((1, b, d), lambda i: (seg(i), 0, 5)),
                  pl.BlockSpec((2, rows * (d // LANES), LANES), lambda i: (0, i, 0)),
                  pl.BlockSpec((rows, LANES), lambda i: (i, 0)),
                  pl.BlockSpec(ln_g.shape, lambda i: (0, 0)), pl.BlockSpec(ln_b.shape, lambda i: (0, 0))],
        out_specs=blk,
        out_shape=jax.ShapeDtypeStruct((t, b, d), F32),
        compiler_params=_params("arbitrary"),
    )(x1, mod_tab, y, wts, ln_g, ln_b)


def _layer(xt, c_all, lp, rope, n_ctx, last):
    t, b, d = xt.shape
    ncb = n_ctx // TQ
    first = ncb if last else 0
    mod = _modulation(c_all, lp['w_mod'], lp['b_mod'])
    mod_tab = jnp.stack([jnp.broadcast_to(mod[b:b + 1], (b, mod.shape[1])), mod[:b]], axis=0)

    pr, ps, q, k, v = _input_projection(xt, mod_tab, lp['w_in'], rope, ncb)

    r, vv, kk, d0, d1, k0, k1, b0, b1, bonus, gate = _rwkv_features(pr, lp, ncb)
    sl = _to_scan_layout
    yf, yb = _rwkv_scan((sl(r, r), sl(vv, vv), sl(kk, kk), sl(d0, d1), sl(k0, k1), sl(b0, b1)), n_ctx)
    yr = _from_scan_layout(yf, yb, b)

    ys = _s5_scan(ps, lp['s5'], ncb)

    ya = _attention(q, k, v, lp['sink'], n_ctx, True)
    if not last:
        ya = jnp.concatenate([_attention(q, k, v, lp['sink'], n_ctx, False), ya], axis=0)

    x1 = _output_projection(xt, mod_tab, yr, bonus, gate, ps, ys, ya, lp, ncb, first)
    h, idx, wts = _router(x1, mod_tab, lp['w_router'], lp['b_router'], ncb, first)
    n_tok = idx.shape[0]
    plan = _dispatch_plan(idx[:, :2], n_tok, d // LANES)
    y = _routed_experts(h, n_tok, plan, lp['w_gate'], lp['w_up'], lp['w_down'], lp['layer'])
    return _moe_combine(x1, mod_tab, y, wts, lp['ln2_g'], lp['ln2_b'], ncb, first)


def kernel(x, c, ctx, c_ctx, w_mod, b_mod, w_in, rwkv_conv, rwkv_w0, rwkv_w2, rwkv_a0, rwkv_a2, rwkv_g2, rwkv_k_k, rwkv_k_a, rwkv_r_k, rwkv_gn_w, rwkv_gn_b, s5_lam_re, s5_lam_im, s5_log_dt, s5_b_re, s5_b_im, s5_c_re, s5_c_im, s5_d, s5_glu_w, s5_glu_b, attn_sink, w_out, ln1_g, ln1_b, ln2_g, ln2_b, router_group_w, router_group_b, router_expert_w, router_expert_b, expert_w_gate, expert_w_up, expert_w_down):
    bsz, seq, d = x.shape
    n_ctx = ctx.shape[1]
    depth = w_mod.shape[0]
    W = RWKV_WIDTH
    xt = jnp.transpose(jnp.concatenate([ctx, x], axis=1), (1, 0, 2))
    c_all = jnp.zeros((bsz + 8, d), F32).at[:bsz].set(c).at[bsz].set(c_ctx)
    rope = _rope_tables(n_ctx, seq)
    hsum = jnp.kron(jnp.eye(RWKV_HEADS, dtype=F32), jnp.ones((HEAD_DIM, HEAD_DIM), F32)).astype(BF16)
    row = lambda a: a.reshape(1, -1)
    zeros_lo = jnp.zeros((RWKV_LORA // 2, 2 * W), F32)
    for i in range(depth):
        n_r = N_GROUPS + N_EXPERTS
        lp = {
            'w_mod': w_mod[i], 'b_mod': b_mod[i], 'w_in': w_in[i].astype(BF16),
            'conv': rwkv_conv[i], 'w0': rwkv_w0[i], 'a0': rwkv_a0[i],
            'w2': jnp.concatenate([jnp.concatenate([rwkv_w2[i, 0], rwkv_w2[i, 1]], axis=1), zeros_lo], axis=0),
            'a2': jnp.concatenate([zeros_lo, jnp.concatenate([rwkv_a2[i, 0], rwkv_a2[i, 1]], axis=1)], axis=0),
            'g2': rwkv_g2[i].astype(BF16), 'k_k': row(rwkv_k_k[i]), 'k_a': row(rwkv_k_a[i]), 'r_k': row(rwkv_r_k[i]),
            'hsum': hsum, 'gn_w': row(rwkv_gn_w[i]), 'gn_b': row(rwkv_gn_b[i]),
            's5': _s5_weights(s5_lam_re[i], s5_lam_im[i], s5_log_dt[i], s5_b_re[i], s5_b_im[i], s5_c_re[i], s5_c_im[i]),
            's5_d': row(s5_d[i]), 'glu_w': s5_glu_w[i].astype(BF16), 'glu_b': row(s5_glu_b[i]),
            'sink': attn_sink[i], 'w_out': w_out[i].astype(BF16),
            'ln1_g': row(ln1_g[i]), 'ln1_b': row(ln1_b[i]), 'ln2_g': row(ln2_g[i]), 'ln2_b': row(ln2_b[i]),
            'w_router': jnp.zeros((d, LANES), F32).at[:, :N_GROUPS].set(router_group_w[i]).at[:, N_GROUPS:n_r].set(router_expert_w[i]),
            'b_router': jnp.zeros((1, LANES), F32).at[0, :N_GROUPS].set(router_group_b[i]).at[0, N_GROUPS:n_r].set(router_expert_b[i]),
            'w_gate': expert_w_gate, 'w_up': expert_w_up, 'w_down': expert_w_down, 'layer': i,
        }
        xt = _layer(xt, c_all, lp, rope, n_ctx, i == depth - 1)
    return jnp.transpose(xt, (1, 0, 2))
```
